```python
import math
import jax
import jax.numpy as jnp
from jax import lax
import numpy as np


D_MODEL = 1024
BATCH = 2
SEQ = 8192
DEPTH = 2

CHUNK = 64
Q_BLOCK = 128
NORM_EPS = 1e-6

MLA_HEADS = 4
MLA_Q_RANK = 256
MLA_KV_RANK = 128
MLA_NOPE = 128
MLA_ROPE = 64
MLA_V = 128
MLA_WIDTH = MLA_HEADS * MLA_V
ROPE_THETA = 10000.0

SSD_HEADS = 8
SSD_HEAD_DIM = 64
SSD_WIDTH = SSD_HEADS * SSD_HEAD_DIM
SSD_GROUPS = 2
SSD_STATE = 128
SSD_CONV = 4
SSD_XBC = SSD_WIDTH + 2 * SSD_GROUPS * SSD_STATE

RWKV_HEADS = 8
RWKV_HEAD_DIM = 64
RWKV_WIDTH = RWKV_HEADS * RWKV_HEAD_DIM
DECAY_LORA = 64
ICLR_LORA = 64
DECAY_SCALE = 0.606531
GN_EPS = 64e-5

D_MIX = MLA_WIDTH + SSD_WIDTH + RWKV_WIDTH
IN_SPLIT_SIZES = (MLA_Q_RANK, MLA_KV_RANK, MLA_ROPE, MLA_WIDTH,
                  SSD_WIDTH, SSD_XBC, SSD_HEADS,
                  RWKV_WIDTH, RWKV_WIDTH, RWKV_WIDTH, DECAY_LORA, ICLR_LORA, RWKV_WIDTH)
D_IN = (MLA_Q_RANK + MLA_KV_RANK + MLA_ROPE + MLA_WIDTH + SSD_WIDTH + SSD_XBC + SSD_HEADS
        + 3 * RWKV_WIDTH + DECAY_LORA + ICLR_LORA + RWKV_WIDTH)

kernel_name = 'hybrid_mla_ssd_rwkv7_adaln_chunk_causal'


def rms_norm(x, w):
    xf = x.astype(jnp.float32)
    y = xf * lax.rsqrt(jnp.mean(xf * xf, axis=-1, keepdims=True) + NORM_EPS)
    return (y * w.astype(jnp.float32)).astype(x.dtype)


def split_cols(t, sizes):
    idx = np.cumsum(np.array(sizes))[:-1].tolist()
    return jnp.split(t, idx, axis=-1)


def rope_angles(positions, dim):
    half = dim // 2
    inv_freq = ROPE_THETA ** (-jnp.arange(half, dtype=jnp.float32) / half)
    ang = positions.astype(jnp.float32)[..., None] * inv_freq
    return jnp.cos(ang), jnp.sin(ang)


def apply_rope(x, cos, sin):
    x1, x2 = jnp.split(x, 2, axis=-1)
    cos = cos.astype(x.dtype)
    sin = sin.astype(x.dtype)
    return jnp.concatenate([x1 * cos - x2 * sin, x1 * sin + x2 * cos], axis=-1)


def chunk_causal_attention(q, k, v):
    b, s, h, dqk = q.shape
    n_blk = s // Q_BLOCK
    scale = dqk ** -0.5
    q_blocks = jnp.moveaxis(q.reshape(b, n_blk, Q_BLOCK, h, dqk), 1, 0)
    key_chunk = jnp.arange(s) // CHUNK

    def one_block(args):
        q_blk, blk = args
        scores = jnp.einsum('bqhd,bkhd->bhqk', q_blk, k).astype(jnp.float32) * scale
        query_chunk = (blk * Q_BLOCK + jnp.arange(Q_BLOCK)) // CHUNK
        mask = key_chunk[None, :] <= query_chunk[:, None]
        probs = jax.nn.softmax(jnp.where(mask, scores, -jnp.inf), axis=-1)
        return jnp.einsum('bhqk,bkhd->bqhd', probs.astype(v.dtype), v)

    out = lax.map(one_block, (q_blocks, jnp.arange(n_blk)))
    return jnp.moveaxis(out, 0, 1).reshape(b, s, h, v.shape[-1])


def mla_branch(q_lat, kv_lat, k_pe, cos, sin, q_norm_w, w_uq, kv_norm_w, w_ukv):
    b, s, _ = q_lat.shape
    q = (rms_norm(q_lat, q_norm_w) @ w_uq).reshape(b, s, MLA_HEADS, MLA_NOPE + MLA_ROPE)
    q_nope, q_pe = jnp.split(q, [MLA_NOPE], axis=-1)
    q_pe = apply_rope(q_pe, cos[:, :, None, :], sin[:, :, None, :])
    kv = (rms_norm(kv_lat, kv_norm_w) @ w_ukv).reshape(b, s, MLA_HEADS, MLA_NOPE + MLA_V)
    k_nope, v = jnp.split(kv, [MLA_NOPE], axis=-1)
    k_pe = apply_rope(k_pe, cos, sin)
    k_pe = jnp.broadcast_to(k_pe[:, :, None, :], (b, s, MLA_HEADS, MLA_ROPE))
    q = jnp.concatenate([q_nope, q_pe], axis=-1)
    k = jnp.concatenate([k_nope, k_pe], axis=-1)
    o = chunk_causal_attention(q, k, v)
    return o.reshape(b, s, MLA_WIDTH)


def causal_depthwise_conv(u, w, bias):
    out = lax.conv_general_dilated(
        u, w[:, None, :], window_strides=(1,), padding=[(w.shape[0] - 1, 0)],
        dimension_numbers=('NWC', 'WIO', 'NWC'), feature_group_count=u.shape[-1])
    return out + bias


def ssd_chunked(xs, dt, a, bm, cm):
    b, s, h, p = xs.shape
    g, n = bm.shape[2], bm.shape[3]
    r = h // g
    nc, L = s // CHUNK, CHUNK
    xdt = (xs * dt[..., None]).reshape(b, nc, L, g, r, p)
    a_cum = jnp.cumsum((dt * a).reshape(b, nc, L, g, r), axis=2)
    bg = bm.reshape(b, nc, L, g, n)
    cg = cm.reshape(b, nc, L, g, n)
    causal = jnp.tril(jnp.ones((L, L), dtype=bool))[None, None, :, :, None, None]
    seg = a_cum[:, :, :, None] - a_cum[:, :, None, :]
    decay = jnp.exp(jnp.where(causal, seg, -jnp.inf))
    cb = jnp.einsum('bctgn,bcsgn->bctsg', cg, bg)
    y_diag = jnp.einsum('bctsg,bctsgr,bcsgrp->bctgrp', cb, decay, xdt)
    decay_to_end = jnp.exp(a_cum[:, :, -1:] - a_cum)
    states = jnp.einsum('bcsgn,bcsgr,bcsgrp->bcgrpn', bg, decay_to_end, xdt)
    chunk_decay = jnp.exp(a_cum[:, :, -1])

    def carry_state(state, inp):
        st, dec = inp
        return state * dec[..., None, None] + st, state

    init = jnp.zeros((b, g, r, p, n), jnp.float32)
    _, prev = lax.scan(carry_state, init, (jnp.moveaxis(states, 1, 0), jnp.moveaxis(chunk_decay, 1, 0)))
    prev = jnp.moveaxis(prev, 0, 1)
    y_off = jnp.einsum('bctgn,bcgrpn,bctgr->bctgrp', cg, prev, jnp.exp(a_cum))
    return (y_diag + y_off).reshape(b, s, h, p)


def ssd_branch(z, xbc, dt_raw, conv_w, conv_b, dt_bias, a_log, d_skip, ssd_norm_w):
    f32 = jnp.float32
    b, s, _ = z.shape
    xbc = jax.nn.silu(causal_depthwise_conv(xbc, conv_w, conv_b))
    xs, bm, cm = split_cols(xbc, (SSD_WIDTH, SSD_GROUPS * SSD_STATE, SSD_GROUPS * SSD_STATE))
    xs = xs.reshape(b, s, SSD_HEADS, SSD_HEAD_DIM).astype(f32)
    bm = bm.reshape(b, s, SSD_GROUPS, SSD_STATE).astype(f32)
    cm = cm.reshape(b, s, SSD_GROUPS, SSD_STATE).astype(f32)
    dt = jax.nn.softplus((dt_raw + dt_bias).astype(f32))
    a = -jnp.exp(a_log.astype(f32))
    y = ssd_chunked(xs, dt, a, bm, cm) + d_skip.astype(f32)[:, None] * xs
    y = y.reshape(b, s, SSD_WIDTH).astype(z.dtype) * jax.nn.silu(z)
    return rms_norm(y, ssd_norm_w)


def token_shift(f, mu):
    prev = jnp.pad(f, ((0, 0), (1, 0), (0, 0)))[:, :-1]
    return f + (prev - f) * mu


def wkv7_scan(r, w, k, v, a, bb):
    def step(state, inp):
        r_t, w_t, k_t, v_t, a_t, b_t = inp
        sa = jnp.einsum('bhij,bhj->bhi', state, a_t)
        state = (state * w_t[:, :, None, :] + sa[..., None] * b_t[:, :, None, :]
                 + v_t[..., None] * k_t[:, :, None, :])
        return state, jnp.einsum('bhij,bhj->bhi', state, r_t)

    bsz, s, h, n = r.shape
    init = jnp.zeros((bsz, h, n, n), jnp.float32)
    xs = tuple(jnp.moveaxis(t, 1, 0) for t in (r, w, k, v, a, bb))
    _, y = lax.scan(step, init, xs)
    return jnp.moveaxis(y, 0, 1)


def rwkv7_branch(r, k, v, w_lo, a_lo, mu_rkv, mu_w, mu_a, w0, w_lora_b, a0, a_lora_b,
                 k_k, k_a, r_k, lnx_w, lnx_b):
    f32 = jnp.float32
    b, s, _ = r.shape
    out_dtype = r.dtype
    r = token_shift(r, mu_rkv[0])
    k = token_shift(k, mu_rkv[1])
    v = token_shift(v, mu_rkv[2])
    w_lo = token_shift(w_lo, mu_w)
    a_lo = token_shift(a_lo, mu_a)
    w = jnp.exp(-DECAY_SCALE * jax.nn.sigmoid((w0 + jnp.tanh(w_lo) @ w_lora_b).astype(f32)))
    a = jax.nn.sigmoid((a0 + a_lo @ a_lora_b).astype(f32))

    def heads(t):
        return t.astype(f32).reshape(t.shape[:-1] + (RWKV_HEADS, RWKV_HEAD_DIM))

    r, k, v, w, a = heads(r), heads(k), heads(v), heads(w), heads(a)
    kk = k * heads(k_k)
    kk = kk / jnp.maximum(jnp.sqrt(jnp.sum(kk * kk, axis=-1, keepdims=True)), 1e-12)
    k = k * (1.0 + (a - 1.0) * heads(k_a))
    y = wkv7_scan(r, w, k, v, -kk, kk * a)
    mean = jnp.mean(y, axis=-1, keepdims=True)
    var = jnp.mean(jnp.square(y - mean), axis=-1, keepdims=True)
    y = (y - mean) * lax.rsqrt(var + GN_EPS) * heads(lnx_w) + heads(lnx_b)
    y = y + jnp.sum(r * k * r_k.astype(f32), axis=-1, keepdims=True) * v
    return y.reshape(b, s, RWKV_WIDTH).astype(out_dtype)


def setup_inputs(seed: int = 0) -> dict:
    key = jax.random.key(seed)
    ks = list(jax.random.split(key, 40))
    f32 = jnp.float32

    def normal(shape, scale):
        return scale * jax.random.normal(ks.pop(), shape, f32)

    def gain(shape):
        return 1.0 + 0.02 * jax.random.normal(ks.pop(), shape, f32)

    def uniform(shape, lo, hi):
        return jax.random.uniform(ks.pop(), shape, f32, lo, hi)

    x = normal((BATCH, SEQ, D_MODEL), 1.0)
    c = normal((BATCH, D_MODEL), 1.0)
    offsets = jax.random.randint(ks.pop(), (BATCH, 1), 0, 4096, dtype=jnp.int32)
    positions = jnp.arange(SEQ, dtype=jnp.int32)[None, :] + offsets
    dt0 = jnp.exp(uniform((DEPTH, SSD_HEADS), math.log(1e-3), math.log(1e-1)))
    return {
        'x': x,
        'c': c,
        'positions': positions,
        'ada_w': normal((DEPTH, D_MODEL, 3 * D_MODEL), 0.5 * D_MODEL ** -0.5),
        'ada_b': normal((DEPTH, 3 * D_MODEL), 0.02),
        'norm_w': gain((DEPTH, D_MODEL)),
        'w_in': normal((DEPTH, D_MODEL, D_IN), D_MODEL ** -0.5),
        'q_norm_w': gain((DEPTH, MLA_Q_RANK)),
        'w_uq': normal((DEPTH, MLA_Q_RANK, MLA_HEADS * (MLA_NOPE + MLA_ROPE)), MLA_Q_RANK ** -0.5),
        'kv_norm_w': gain((DEPTH, MLA_KV_RANK)),
        'w_ukv': normal((DEPTH, MLA_KV_RANK, MLA_HEADS * (MLA_NOPE + MLA_V)), MLA_KV_RANK ** -0.5),
        'conv_w': normal((DEPTH, SSD_CONV, SSD_XBC), SSD_CONV ** -0.5),
        'conv_b': normal((DEPTH, SSD_XBC), 0.02),
        'dt_bias': dt0 + jnp.log(-jnp.expm1(-dt0)),
        'a_log': jnp.log(uniform((DEPTH, SSD_HEADS), 1.0, 16.0)),
        'd_skip': 1.0 + normal((DEPTH, SSD_HEADS), 0.1),
        'ssd_norm_w': gain((DEPTH, SSD_WIDTH)),
        'mu_rkv': uniform((DEPTH, 3, RWKV_WIDTH), 0.0, 1.0),
        'mu_w': uniform((DEPTH, DECAY_LORA), 0.0, 1.0),
        'mu_a': uniform((DEPTH, ICLR_LORA), 0.0, 1.0),
        'w0': uniform((DEPTH, RWKV_WIDTH), -4.0, 1.0),
        'w_lora_b': normal((DEPTH, DECAY_LORA, RWKV_WIDTH), 0.1),
        'a0': normal((DEPTH, RWKV_WIDTH), 0.5),
        'a_lora_b': normal((DEPTH, ICLR_LORA, RWKV_WIDTH), 0.5 * ICLR_LORA ** -0.5),
        'k_k': 0.85 + normal((DEPTH, RWKV_WIDTH), 0.05),
        'k_a': 1.0 + normal((DEPTH, RWKV_WIDTH), 0.05),
        'r_k': normal((DEPTH, RWKV_HEADS, RWKV_HEAD_DIM), 0.1),
        'lnx_w': gain((DEPTH, RWKV_WIDTH)),
        'lnx_b': normal((DEPTH, RWKV_WIDTH), 0.02),
        'w_out': normal((DEPTH, D_MIX, D_MODEL), D_MIX ** -0.5),
        'final_norm_w': gain((D_MODEL,)),
    }


def reference(x, c, positions, ada_w, ada_b, norm_w, w_in, q_norm_w, w_uq, kv_norm_w, w_ukv,
              conv_w, conv_b, dt_bias, a_log, d_skip, ssd_norm_w, mu_rkv, mu_w, mu_a, w0,
              w_lora_b, a0, a_lora_b, k_k, k_a, r_k, lnx_w, lnx_b, w_out, final_norm_w):
    cos, sin = rope_angles(positions, MLA_ROPE)
    c_act = jax.nn.silu(c)
    for l in range(DEPTH):
        mod = c_act @ ada_w[l] + ada_b[l]
        shift, scale, gate = jnp.split(mod[:, None, :], 3, axis=-1)
        h = rms_norm(x, norm_w[l]) * (1.0 + scale) + shift
        (q_lat, kv_lat, k_pe, g_mla, z, xbc, dt_raw,
         r, k, v, w_lo, a_lo, g_rwkv) = split_cols(h @ w_in[l], IN_SPLIT_SIZES)
        y_mla = mla_branch(q_lat, kv_lat, k_pe, cos, sin, q_norm_w[l], w_uq[l],
                           kv_norm_w[l], w_ukv[l]) * jax.nn.silu(g_mla)
        y_ssd = ssd_branch(z, xbc, dt_raw, conv_w[l], conv_b[l], dt_bias[l], a_log[l],
                           d_skip[l], ssd_norm_w[l])
        y_rwkv = rwkv7_branch(r, k, v, w_lo, a_lo, mu_rkv[l], mu_w[l], mu_a[l], w0[l],
                              w_lora_b[l], a0[l], a_lora_b[l], k_k[l], k_a[l], r_k[l],
                              lnx_w[l], lnx_b[l]) * jax.nn.silu(g_rwkv)
        y = jnp.concatenate([y_mla, y_ssd, y_rwkv], axis=-1)
        x = x + gate * (y @ w_out[l])
    return rms_norm(x, final_norm_w)
```

```python
import functools

import jax
import jax.numpy as jnp
import numpy as np
from jax import lax
from jax.experimental import pallas as pl
from jax.experimental.pallas import tpu as pltpu

F32 = jnp.float32
BF16 = jnp.bfloat16

D_MODEL = 1024
CHUNK = 64
CHUNK_SHIFT = 6
NORM_EPS = 1e-6

MLA_HEADS = 4
MLA_Q_RANK = 256
MLA_KV_RANK = 128
MLA_NOPE = 128
MLA_ROPE = 64
MLA_V = 128
MLA_WIDTH = MLA_HEADS * MLA_V
MLA_QK_PAD = 256
ROPE_THETA = 10000.0

SSD_HEADS = 8
SSD_HEAD_DIM = 64
SSD_WIDTH = SSD_HEADS * SSD_HEAD_DIM
SSD_GROUPS = 2
SSD_STATE = 128
SSD_CONV = 4
SSD_XBC = SSD_WIDTH + 2 * SSD_GROUPS * SSD_STATE

RWKV_HEADS = 8
RWKV_HEAD_DIM = 64
RWKV_WIDTH = RWKV_HEADS * RWKV_HEAD_DIM
DECAY_LORA = 64
ICLR_LORA = 64
DECAY_SCALE = 0.606531
GN_EPS = 64e-5

D_MIX = MLA_WIDTH + SSD_WIDTH + RWKV_WIDTH

LANES = 128
PAIR = 2 * RWKV_HEAD_DIM
N_PAIRS = RWKV_HEADS // 2

COL_XBC = 0
COL_GMLA = 1024
COL_Z = 1536
COL_R = 2048
COL_K = 2560
COL_V = 3072
COL_GRWKV = 3584
COL_LAT = 4096
COL_MISC = 4608
PROJ_COLS = 5120
PROJ_BLOCK = 512

VMEM_LIMIT_BYTES = 56 * 1024 * 1024


def _params(n_axes):
    return pltpu.CompilerParams(dimension_semantics=("arbitrary",) * n_axes,
                                vmem_limit_bytes=VMEM_LIMIT_BYTES)


def _dot(a, b):
    return jnp.dot(a.astype(BF16), b.astype(BF16), preferred_element_type=F32)


def _dot_nt(a, b):
    return lax.dot_general(a.astype(BF16), b.astype(BF16), (((1,), (1,)), ((), ())),
                           preferred_element_type=F32)


def _dot_tn(a, b):
    return lax.dot_general(a.astype(BF16), b.astype(BF16), (((0,), (0,)), ((), ())),
                           preferred_element_type=F32)


def _split(x):
    hi = x.astype(BF16)
    lo = (x - hi.astype(F32)).astype(BF16)
    return hi, lo


def _dot_exact_rhs(a, b_bf16):
    hi, lo = _split(a)
    return (jnp.dot(hi, b_bf16, preferred_element_type=F32)
            + jnp.dot(lo, b_bf16, preferred_element_type=F32))


def _dot_exact_lhs(a_bf16, b):
    hi, lo = _split(b)
    return (jnp.dot(a_bf16, hi, preferred_element_type=F32)
            + jnp.dot(a_bf16, lo, preferred_element_type=F32))


def _silu(x):
    return x * jax.nn.sigmoid(x)


def _iota(shape, dim):
    return lax.broadcasted_iota(jnp.int32, shape, dim)


def _mod_kernel(c_ref, w_ref, b_ref, o_ref):
    c = c_ref[...]
    ch, cl = _split(_silu(c))
    wh, wl = _split(w_ref[0])
    acc = (jnp.dot(ch, wh, preferred_element_type=F32)
           + jnp.dot(cl, wh, preferred_element_type=F32)
           + jnp.dot(ch, wl, preferred_element_type=F32))
    o_ref[0] = acc + b_ref[0]


def _mod_call(c8, ada_w, ada_b):
    depth, d, d3 = ada_w.shape
    tn = 512
    return pl.pallas_call(
        _mod_kernel,
        out_shape=jax.ShapeDtypeStruct((depth, 8, d3), F32),
        grid=(depth, d3 // tn),
        in_specs=[pl.BlockSpec((8, d), lambda l, j: (0, 0)),
                  pl.BlockSpec((1, d, tn), lambda l, j: (l, 0, j)),
                  pl.BlockSpec((1, 1, tn), lambda l, j: (l, 0, j))],
        out_specs=pl.BlockSpec((1, 8, tn), lambda l, j: (l, 0, j)),
        compiler_params=_params(2), name="adaln_mod",
    )(c8, ada_w, ada_b.reshape(depth, 1, d3))


def _rope_kernel(pos_ref, freq_ref, sign_ref, o_ref):
    ang = pos_ref[...].astype(F32) * freq_ref[...]
    o_ref[:, 0:MLA_ROPE] = jnp.cos(ang)
    o_ref[:, MLA_ROPE:2 * MLA_ROPE] = jnp.sin(ang) * sign_ref[...]


def _rope_call(pos_col):
    t = pos_col.shape[0]
    tr = min(t, 1024)
    half = MLA_ROPE // 2
    inv_freq = ROPE_THETA ** (-jnp.arange(half, dtype=F32) / half)
    freq = jnp.concatenate([inv_freq, inv_freq]).reshape(1, MLA_ROPE)
    sign = jnp.concatenate([-jnp.ones((half,), F32), jnp.ones((half,), F32)]).reshape(1, MLA_ROPE)
    return pl.pallas_call(
        _rope_kernel,
        out_shape=jax.ShapeDtypeStruct((t, 2 * MLA_ROPE), F32),
        grid=(t // tr,),
        in_specs=[pl.BlockSpec((tr, 1), lambda i: (i, 0)),
                  pl.BlockSpec((1, MLA_ROPE), lambda i: (0, 0)),
                  pl.BlockSpec((1, MLA_ROPE), lambda i: (0, 0))],
        out_specs=pl.BlockSpec((tr, 2 * MLA_ROPE), lambda i: (i, 0)),
        compiler_params=_params(1), name="rope_table",
    )(pos_col, freq, sign)


def _inproj_kernel(x_ref, scale_ref, shift_ref, nw_ref, w_ref, o_ref, h_scr):
    @pl.when(pl.program_id(1) == 0)
    def _():
        x = x_ref[...]
        ms = jnp.mean(x * x, axis=-1, keepdims=True)
        y = x * lax.rsqrt(ms + NORM_EPS) * nw_ref[...]
        h = y * (1.0 + scale_ref[0]) + shift_ref[0]
        h_scr[...] = h.astype(BF16)

    o_ref[...] = jnp.dot(h_scr[...], w_ref[...], preferred_element_type=F32)


def _inproj_call(x2, scale, shift, norm_w, wp, seq):
    t, d = x2.shape
    tm = min(seq, 1024)
    per_batch = seq // tm
    return pl.pallas_call(
        _inproj_kernel,
        out_shape=jax.ShapeDtypeStruct((t, PROJ_COLS), F32),
        grid=(t // tm, PROJ_COLS // PROJ_BLOCK),
        in_specs=[pl.BlockSpec((tm, d), lambda i, j: (i, 0)),
                  pl.BlockSpec((1, 1, d), lambda i, j: (i // per_batch, 0, 0)),
                  pl.BlockSpec((1, 1, d), lambda i, j: (i // per_batch, 0, 0)),
                  pl.BlockSpec((1, d), lambda i, j: (0, 0)),
                  pl.BlockSpec((d, PROJ_BLOCK), lambda i, j: (0, j))],
        out_specs=pl.BlockSpec((tm, PROJ_BLOCK), lambda i, j: (i, j)),
        scratch_shapes=[pltpu.VMEM((tm, d), BF16)],
        compiler_params=_params(2), name="in_proj",
    )(x2, scale, shift, norm_w.reshape(1, d), wp)


def _pack_w_in(w):
    d = w.shape[0]
    o = np.cumsum((0, MLA_Q_RANK, MLA_KV_RANK, MLA_ROPE, MLA_WIDTH, SSD_WIDTH, SSD_XBC, SSD_HEADS,
                   RWKV_WIDTH, RWKV_WIDTH, RWKV_WIDTH, DECAY_LORA, ICLR_LORA, RWKV_WIDTH))
    q_lat, kv_lat, k_pe, g_mla, z, xbc, dt, r, k, v, w_lo, a_lo, g_rwkv = (
        w[:, o[i]:o[i + 1]] for i in range(13))
    half = MLA_ROPE // 2
    k_pe_sw = jnp.concatenate([k_pe[:, half:], k_pe[:, :half]], axis=1)
    misc_pad = jnp.zeros((d, PROJ_BLOCK - DECAY_LORA - ICLR_LORA - SSD_HEADS), w.dtype)
    return jnp.concatenate([xbc, g_mla, z, r, k, v, g_rwkv, q_lat, kv_lat, k_pe, k_pe_sw,
                            w_lo, a_lo, dt, misc_pad], axis=1).astype(BF16)


def _rms(x, w):
    ms = jnp.mean(x * x, axis=-1, keepdims=True)
    return x * lax.rsqrt(ms + NORM_EPS) * w


def _mla_prep_kernel(lat_ref, rope_ref, qnw_ref, wuq_ref, kvnw_ref, wukv_ref, q_ref, k_ref, v_ref):
    lat = lat_ref[...]
    cs = rope_ref[...]
    q = _dot(_rms(lat[:, 0:MLA_Q_RANK], qnw_ref[...]), wuq_ref[...])
    kv = _dot(_rms(lat[:, MLA_Q_RANK:MLA_Q_RANK + MLA_KV_RANK], kvnw_ref[...]), wukv_ref[...])
    kp = lat[:, MLA_Q_RANK + MLA_KV_RANK:] * cs
    k_pe = (kp[:, 0:MLA_ROPE] + kp[:, MLA_ROPE:]).astype(BF16)
    rows = lat.shape[0]
    zpad = jnp.zeros((rows, MLA_QK_PAD - MLA_NOPE - MLA_ROPE), BF16)
    scale = (MLA_NOPE + MLA_ROPE) ** -0.5
    for h in range(MLA_HEADS):
        qh = q[:, 256 * h:256 * (h + 1)]
        qp = qh[:, MLA_NOPE:] * cs
        q_pe = qp[:, 0:MLA_ROPE] + qp[:, MLA_ROPE:]
        q_ref[0, h] = jnp.concatenate([(qh[:, 0:MLA_NOPE] * scale).astype(BF16),
                                       (q_pe * scale).astype(BF16), zpad], axis=1)
        k_ref[0, h] = jnp.concatenate([kv[:, 256 * h:256 * h + MLA_NOPE].astype(BF16), k_pe, zpad],
                                      axis=1)
        v_ref[0, h] = kv[:, 256 * h + MLA_NOPE:256 * (h + 1)].astype(BF16)


def _pack_w_uq(w_uq):
    half = MLA_ROPE // 2
    per = MLA_NOPE + MLA_ROPE
    cols = []
    for h in range(MLA_HEADS):
        wh = w_uq[:, per * h:per * (h + 1)]
        pe = wh[:, MLA_NOPE:]
        cols += [wh, pe[:, half:], pe[:, :half]]
    return jnp.concatenate(cols, axis=1).astype(BF16)


def _mla_prep_call(proj, rope_tab, q_norm_w, w_uq, kv_norm_w, w_ukv, batch, seq):
    tm = min(seq, 512)
    per_batch = seq // tm
    lat_blk = COL_LAT // PROJ_BLOCK
    qk_shape = jax.ShapeDtypeStruct((batch, MLA_HEADS, seq, MLA_QK_PAD), BF16)
    qk_spec = pl.BlockSpec((1, MLA_HEADS, tm, MLA_QK_PAD), lambda i: (i // per_batch, 0, i % per_batch, 0))
    return pl.pallas_call(
        _mla_prep_kernel,
        out_shape=(qk_shape, qk_shape, jax.ShapeDtypeStruct((batch, MLA_HEADS, seq, MLA_V), BF16)),
        grid=(batch * per_batch,),
        in_specs=[pl.BlockSpec((tm, PROJ_BLOCK), lambda i: (i, lat_blk)),
                  pl.BlockSpec((tm, 2 * MLA_ROPE), lambda i: (i, 0)),
                  pl.BlockSpec((1, MLA_Q_RANK), lambda i: (0, 0)),
                  pl.BlockSpec((MLA_Q_RANK, 256 * MLA_HEADS), lambda i: (0, 0)),
                  pl.BlockSpec((1, MLA_KV_RANK), lambda i: (0, 0)),
                  pl.BlockSpec((MLA_KV_RANK, 256 * MLA_HEADS), lambda i: (0, 0))],
        out_specs=(qk_spec, qk_spec,
                   pl.BlockSpec((1, MLA_HEADS, tm, MLA_V), lambda i: (i // per_batch, 0, i % per_batch, 0))),
        compiler_params=_params(1), name="mla_prep",
    )(proj, rope_tab, q_norm_w.reshape(1, -1), _pack_w_uq(w_uq), kv_norm_w.reshape(1, -1),
      w_ukv.astype(BF16))


def _attn_kernel(qi_tab, ki_tab, q_ref, k_ref, v_ref, g_ref, o_ref, m_scr, l_scr, acc_scr):
    p = pl.program_id(2)
    qi = qi_tab[p]
    ki = ki_tab[p]

    @pl.when(ki == 0)
    def _():
        m_scr[...] = jnp.full(m_scr.shape, -jnp.inf, F32)
        l_scr[...] = jnp.zeros(l_scr.shape, F32)
        acc_scr[...] = jnp.zeros(acc_scr.shape, F32)

    def step(diagonal):
        s = lax.dot_general(q_ref[0, 0], k_ref[0, 0], (((1,), (1,)), ((), ())),
                            preferred_element_type=F32)
        if diagonal:
            row = lax.shift_right_logical(_iota(s.shape, 0), CHUNK_SHIFT)
            col = lax.shift_right_logical(_iota(s.shape, 1), CHUNK_SHIFT)
            s = jnp.where(col <= row, s, -jnp.inf)
        m_prev = m_scr[...]
        m_new = jnp.maximum(m_prev, jnp.max(s, axis=-1, keepdims=True))
        alpha = jnp.exp(m_prev - m_new)
        pm = jnp.exp(s - m_new)
        l_scr[...] = alpha * l_scr[...] + jnp.sum(pm, axis=-1, keepdims=True)
        acc_scr[...] = alpha * acc_scr[...] + jnp.dot(pm.astype(BF16), v_ref[0, 0],
                                                      preferred_element_type=F32)
        m_scr[...] = m_new

    @pl.when(ki < qi)
    def _():
        step(False)

    @pl.when(ki == qi)
    def _():
        step(True)
        o_ref[...] = acc_scr[...] / l_scr[...] * _silu(g_ref[...])


def _attn_call(q, k, v, proj, batch, seq):
    tq = min(seq, 1024)
    nq = seq // tq
    pairs = [(a, b) for a in range(nq) for b in range(a + 1)]
    qi_tab = jnp.asarray([a for a, _ in pairs], jnp.int32)
    ki_tab = jnp.asarray([b for _, b in pairs], jnp.int32)
    g_blk = COL_GMLA // MLA_V
    grid_spec = pltpu.PrefetchScalarGridSpec(
        num_scalar_prefetch=2,
        grid=(batch, MLA_HEADS, len(pairs)),
        in_specs=[pl.BlockSpec((1, 1, tq, MLA_QK_PAD), lambda b, h, p, qt, kt: (b, h, qt[p], 0)),
                  pl.BlockSpec((1, 1, tq, MLA_QK_PAD), lambda b, h, p, qt, kt: (b, h, kt[p], 0)),
                  pl.BlockSpec((1, 1, tq, MLA_V), lambda b, h, p, qt, kt: (b, h, kt[p], 0)),
                  pl.BlockSpec((tq, MLA_V), lambda b, h, p, qt, kt: (b * nq + qt[p], g_blk + h))],
        out_specs=pl.BlockSpec((tq, MLA_V), lambda b, h, p, qt, kt: (b * nq + qt[p], h)),
        scratch_shapes=[pltpu.VMEM((tq, 1), F32), pltpu.VMEM((tq, 1), F32),
                        pltpu.VMEM((tq, MLA_V), F32)],
    )
    return pl.pallas_call(
        _attn_kernel,
        out_shape=jax.ShapeDtypeStruct((batch * seq, MLA_WIDTH), F32),
        grid_spec=grid_spec,
        compiler_params=_params(3), name="mla_attn",
    )(qi_tab, ki_tab, q, k, v, proj)


def _tri_incl_bf16():
    return (_iota((CHUNK, CHUNK), 0) >= _iota((CHUNK, CHUNK), 1)).astype(BF16)


def _ssd_kernel(xbc_ref, z_ref, misc_ref, convw_ref, convb_ref, dtb_ref, alog_ref, dskip_ref, nw_ref,
                expand_ref, o_ref, xpad_scr, state_scr, xc_scr, xdt_scr, dta_scr, y_scr):
    ts = xbc_ref.shape[0]
    halo = 8

    @pl.when(pl.program_id(1) == 0)
    def _():
        xpad_scr[0:halo, :] = jnp.zeros((halo, SSD_XBC), F32)
        state_scr[...] = jnp.zeros(state_scr.shape, F32)

    xpad_scr[halo:halo + ts, :] = xbc_ref[...]
    conv = convb_ref[...]
    for i in range(SSD_CONV):
        start = halo - (SSD_CONV - 1) + i
        conv = conv + convw_ref[i:i + 1, :] * xpad_scr[start:start + ts, :]
    xpad_scr[0:halo, :] = xpad_scr[ts:ts + halo, :]
    xc_scr[...] = _silu(conv)

    dt = jax.nn.softplus(misc_ref[:, LANES:2 * LANES] + dtb_ref[...])
    dt_x = _dot_exact_rhs(dt, expand_ref[...])
    xdt_scr[...] = xc_scr[:, 0:SSD_WIDTH] * dt_x
    dta_scr[...] = dt_x * (-jnp.exp(alog_ref[...]))

    tri = _tri_incl_bf16()
    lane = _iota((CHUNK, PAIR), 1)
    row = _iota((CHUNK, PAIR), 0)
    in_first = lane < SSD_HEAD_DIM
    src = jnp.where(in_first, lane, lane - SSD_HEAD_DIM)
    diag2 = row == src
    causal2 = row >= src
    gw = SSD_WIDTH // SSD_GROUPS

    def chunk_body(c, carry):
        r0 = pl.multiple_of(c * CHUNK, CHUNK)
        rows = pl.ds(r0, CHUNK)
        cum = _dot_exact_lhs(tri, dta_scr[rows, :])
        tot = cum[CHUNK - 1:CHUNK, :]
        e_in = jnp.exp(cum)
        xdt = xdt_scr[rows, :]
        xw = xdt * jnp.exp(tot - cum)
        for g in range(SSD_GROUPS):
            bm = xc_scr[rows, SSD_WIDTH + SSD_STATE * g:SSD_WIDTH + SSD_STATE * (g + 1)]
            cm = xc_scr[rows, SSD_WIDTH + SSD_GROUPS * SSD_STATE + SSD_STATE * g:
                        SSD_WIDTH + SSD_GROUPS * SSD_STATE + SSD_STATE * (g + 1)]
            gl = slice(gw * g, gw * (g + 1))
            cb2 = _dot_nt(cm, jnp.concatenate([bm, bm], axis=0))
            prev = state_scr[:, gl]
            y_off = _dot(cm, prev) * e_in[:, gl]
            state_scr[:, gl] = prev * jnp.exp(tot[:, gl]) + _dot_tn(bm, xw[:, gl])
            for j in range(2):
                pl_ = slice(gw * g + PAIR * j, gw * g + PAIR * (j + 1))
                colp = cum[:, pl_]
                rowp = jnp.sum(jnp.where(diag2, colp, 0.0), axis=0, keepdims=True)
                dec = jnp.exp(jnp.where(causal2, colp - rowp, -jnp.inf))
                xp = xdt[:, pl_]
                stacked = jnp.concatenate([jnp.where(in_first, xp, 0.0), jnp.where(in_first, 0.0, xp)],
                                          axis=0)
                y_scr[rows, pl_] = _dot(cb2 * dec, stacked) + y_off[:, PAIR * j:PAIR * (j + 1)]
        return carry

    lax.fori_loop(0, ts // CHUNK, chunk_body, 0)

    y = y_scr[...] + dskip_ref[...] * xc_scr[:, 0:SSD_WIDTH]
    o_ref[...] = _rms(y * _silu(z_ref[...]), nw_ref[...])


def _head_expand(vec, width):
    return jnp.repeat(vec.astype(F32), width).reshape(1, -1)


def _ssd_call(proj, conv_w, conv_b, dt_bias, a_log, d_skip, ssd_norm_w, batch, seq):
    ts = min(seq, 512)
    nt = seq // ts
    dtb = jnp.zeros((1, LANES), F32).at[0, :SSD_HEADS].set(dt_bias)
    expand = (jnp.arange(LANES)[:, None] == (jnp.arange(SSD_WIDTH) // SSD_HEAD_DIM)[None, :]).astype(BF16)
    row = lambda w: pl.BlockSpec((1, w), lambda b, i: (0, 0))
    return pl.pallas_call(
        _ssd_kernel,
        out_shape=jax.ShapeDtypeStruct((batch * seq, SSD_WIDTH), F32),
        grid=(batch, nt),
        in_specs=[pl.BlockSpec((ts, SSD_XBC), lambda b, i: (b * nt + i, COL_XBC // SSD_XBC)),
                  pl.BlockSpec((ts, PROJ_BLOCK), lambda b, i: (b * nt + i, COL_Z // PROJ_BLOCK)),
                  pl.BlockSpec((ts, PROJ_BLOCK), lambda b, i: (b * nt + i, COL_MISC // PROJ_BLOCK)),
                  pl.BlockSpec((SSD_CONV, SSD_XBC), lambda b, i: (0, 0)),
                  row(SSD_XBC), row(LANES), row(SSD_WIDTH), row(SSD_WIDTH), row(SSD_WIDTH),
                  pl.BlockSpec((LANES, SSD_WIDTH), lambda b, i: (0, 0))],
        out_specs=pl.BlockSpec((ts, SSD_WIDTH), lambda b, i: (b * nt + i, 0)),
        scratch_shapes=[pltpu.VMEM((ts + 8, SSD_XBC), F32),
                        pltpu.VMEM((SSD_STATE, SSD_WIDTH), F32),
                        pltpu.VMEM((ts, SSD_XBC), F32),
                        pltpu.VMEM((ts, SSD_WIDTH), F32),
                        pltpu.VMEM((ts, SSD_WIDTH), F32),
                        pltpu.VMEM((ts, SSD_WIDTH), F32)],
        compiler_params=_params(2), name="ssd",
    )(proj, proj, proj, conv_w, conv_b.reshape(1, -1), dtb, _head_expand(a_log, SSD_HEAD_DIM),
      _head_expand(d_skip, SSD_HEAD_DIM), ssd_norm_w.reshape(1, -1), expand)


def _pair_sum(x, in_first):
    s0 = jnp.sum(jnp.where(in_first, x, 0.0), axis=-1, keepdims=True)
    s1 = jnp.sum(jnp.where(in_first, 0.0, x), axis=-1, keepdims=True)
    return jnp.where(in_first, s0, s1)


def _stack_heads(x, in_first):
    return jnp.concatenate([jnp.where(in_first, x, 0.0), jnp.where(in_first, 0.0, x)], axis=0)


def _rwkv_kernel(r_ref, k_ref, v_ref, misc_ref, g_ref, mur_ref, muk_ref, muv_ref, muwa_ref, w0_ref,
                 wlb_ref, a0_ref, alb_ref, kk_ref, ka_ref, rk_ref, lnw_ref, lnb_ref, o_ref,
                 prev_scr, state_scr, r_scr, lw_scr, k_scr, v_scr, a_scr, b_scr, y_scr):
    ts = r_ref.shape[0]

    @pl.when(pl.program_id(1) == 0)
    def _():
        prev_scr[...] = jnp.zeros(prev_scr.shape, F32)
        state_scr[...] = jnp.zeros(state_scr.shape, F32)

    first_row = _iota((ts, 1), 0) == 0

    def shift(x, mu, slot):
        w = x.shape[1]
        prev = jnp.where(first_row, prev_scr[slot:slot + 1, 0:w], pltpu.roll(x, 1, 0))
        prev_scr[slot:slot + 1, 0:w] = x[ts - 1:ts, :]
        return x + (prev - x) * mu

    r = shift(r_ref[...], mur_ref[...], 0)
    k = shift(k_ref[...], muk_ref[...], 1)
    v = shift(v_ref[...], muv_ref[...], 2)
    wa = shift(misc_ref[:, 0:LANES], muwa_ref[...], 3)

    lw_scr[...] = -DECAY_SCALE * jax.nn.sigmoid(w0_ref[...] + _dot(jnp.tanh(wa), wlb_ref[...]))
    a = jax.nn.sigmoid(a0_ref[...] + _dot(wa, alb_ref[...]))

    in_first_t = _iota((ts, PAIR), 1) < RWKV_HEAD_DIM
    kk = k * kk_ref[...]
    k2 = k * (1.0 + (a - 1.0) * ka_ref[...])
    rkr = r * k2 * rk_ref[...]
    for p in range(N_PAIRS):
        sl = slice(PAIR * p, PAIR * (p + 1))
        kkp = kk[:, sl]
        kkn = kkp / jnp.maximum(jnp.sqrt(_pair_sum(kkp * kkp, in_first_t)), 1e-12)
        a_scr[:, sl] = -kkn
        b_scr[:, sl] = kkn * a[:, sl]
        y_scr[:, sl] = _pair_sum(rkr[:, sl], in_first_t)
    bonus = y_scr[...] * v
    r_scr[...] = r
    k_scr[...] = k2
    v_scr[...] = v

    tri = _tri_incl_bf16()
    in_first = _iota((CHUNK, PAIR), 1) < RWKV_HEAD_DIM
    rr = _iota((PAIR, PAIR), 0)
    cc = _iota((PAIR, PAIR), 1)
    same = lax.shift_right_logical(rr, CHUNK_SHIFT) == lax.shift_right_logical(cc, CHUNK_SHIFT)
    strict = same & (rr > cc)
    incl = same & (rr >= cc)
    eye_b = rr == cc
    eye = eye_b.astype(F32)
    zeros_pp = jnp.zeros((PAIR, PAIR), F32)

    def chunk_body(c, carry):
        r0 = pl.multiple_of(c * CHUNK, CHUNK)
        rows = pl.ds(r0, CHUNK)
        lw = lw_scr[rows, :]
        cum = _dot_exact_lhs(tri, lw)
        tot = cum[CHUNK - 1:CHUNK, :]
        e_in = jnp.exp(cum)
        e_ex = jnp.exp(cum - lw)
        e_neg = jnp.exp(-cum)
        e_end = jnp.exp(tot - cum)
        w_end = jnp.exp(tot)
        rc, kc, vc, ac, bc = r_scr[rows, :], k_scr[rows, :], v_scr[rows, :], a_scr[rows, :], b_scr[rows, :]
        a_t = ac * e_ex
        r_t = rc * e_in
        b_h = bc * e_neg
        k_h = kc * e_neg
        b_e = bc * e_end
        k_e = kc * e_end
        for p in range(N_PAIRS):
            sl = slice(PAIR * p, PAIR * (p + 1))
            st = lambda x: _stack_heads(x[:, sl], in_first)
            atm, rtm, vm = st(a_t), st(r_t), st(vc)
            amat = _dot_nt(jnp.concatenate([atm, rtm], axis=0),
                           jnp.concatenate([st(b_h), st(k_h)], axis=0))
            n_ab = jnp.where(strict, amat[0:PAIR, 0:PAIR], 0.0)
            a_ak = jnp.where(strict, amat[0:PAIR, PAIR:], 0.0)
            a_rb = jnp.where(incl, amat[PAIR:, 0:PAIR], 0.0)
            a_rk = jnp.where(incl, amat[PAIR:, PAIR:], 0.0)
            akv = _dot(a_ak, vm)
            npow = _dot(n_ab, n_ab)
            tinv = eye + n_ab
            for _ in range(CHUNK_SHIFT - 2):
                both = _dot(npow, jnp.concatenate([npow, tinv], axis=1))
                npow = both[:, 0:PAIR]
                tinv = tinv + both[:, PAIR:]
            tinv = tinv + _dot(npow, tinv)
            au = _dot(tinv, jnp.concatenate([atm, akv], axis=1))
            rhs = jnp.concatenate([au, jnp.concatenate([zeros_pp, vm], axis=1)], axis=0)
            ry = _dot(jnp.concatenate([a_rb, a_rk], axis=1), rhs)
            gh = _dot_tn(jnp.concatenate([st(b_e), st(k_e)], axis=0), rhs)
            gmat = gh[:, 0:PAIR] + jnp.where(eye_b, w_end[:, sl], 0.0)
            state = state_scr[p]
            s_hi, s_lo = _split(state)
            s2 = jnp.concatenate([s_hi, s_lo], axis=1)
            y2 = _dot(rtm + ry[:, 0:PAIR], s2)
            g2 = _dot(gmat, s2)
            state_scr[p] = gh[:, PAIR:] + g2[:, 0:PAIR] + g2[:, PAIR:]
            yst = ry[:, PAIR:] + y2[:, 0:PAIR] + y2[:, PAIR:]
            y_scr[rows, sl] = yst[0:CHUNK, :] + yst[CHUNK:, :]
        return carry

    lax.fori_loop(0, ts // CHUNK, chunk_body, 0)

    g = g_ref[...]
    for p in range(N_PAIRS):
        sl = slice(PAIR * p, PAIR * (p + 1))
        y = y_scr[:, sl]
        mean = _pair_sum(y, in_first_t) * (1.0 / RWKV_HEAD_DIM)
        d = y - mean
        var = _pair_sum(d * d, in_first_t) * (1.0 / RWKV_HEAD_DIM)
        yn = d * lax.rsqrt(var + GN_EPS) * lnw_ref[:, sl] + lnb_ref[:, sl]
        o_ref[:, sl] = (yn + bonus[:, sl]) * _silu(g[:, sl])


def _rwkv_call(proj, mu_rkv, mu_w, mu_a, w0, w_lora_b, a0, a_lora_b, k_k, k_a, r_k, lnx_w, lnx_b,
               batch, seq):
    ts = min(seq, 512)
    nt = seq // ts
    w = RWKV_WIDTH
    muwa = jnp.concatenate([mu_w, mu_a]).reshape(1, LANES)
    zl = jnp.zeros((DECAY_LORA, w), F32)
    wlb = jnp.concatenate([w_lora_b, zl], axis=0).astype(BF16)
    alb = jnp.concatenate([zl, a_lora_b], axis=0).astype(BF16)
    blk = lambda col: pl.BlockSpec((ts, PROJ_BLOCK), lambda b, i: (b * nt + i, col // PROJ_BLOCK))
    row = lambda width: pl.BlockSpec((1, width), lambda b, i: (0, 0))
    lora = pl.BlockSpec((LANES, w), lambda b, i: (0, 0))
    tile = pltpu.VMEM((ts, w), F32)
    return pl.pallas_call(
        _rwkv_kernel,
        out_shape=jax.ShapeDtypeStruct((batch * seq, w), F32),
        grid=(batch, nt),
        in_specs=[blk(COL_R), blk(COL_K), blk(COL_V), blk(COL_MISC), blk(COL_GRWKV),
                  row(w), row(w), row(w), row(LANES), row(w), lora, row(w), lora,
                  row(w), row(w), row(w), row(w), row(w)],
        out_specs=pl.BlockSpec((ts, w), lambda b, i: (b * nt + i, 0)),
        scratch_shapes=[pltpu.VMEM((8, w), F32),
                        pltpu.VMEM((N_PAIRS, PAIR, PAIR), F32),
                        tile, tile, tile, tile, tile, tile, tile],
        compiler_params=_params(2), name="rwkv7",
    )(proj, proj, proj, proj, proj,
      mu_rkv[0:1], mu_rkv[1:2], mu_rkv[2:3], muwa, w0.reshape(1, w), wlb, a0.reshape(1, w), alb,
      k_k.reshape(1, w), k_a.reshape(1, w), r_k.reshape(1, w), lnx_w.reshape(1, w), lnx_b.reshape(1, w))


def _outproj_kernel(ym_ref, ys_ref, yr_ref, x_ref, gate_ref, w_ref, fnw_ref, o_ref, *, final):
    acc = (_dot(ym_ref[...], w_ref[0:MLA_WIDTH, :])
           + _dot(ys_ref[...], w_ref[MLA_WIDTH:MLA_WIDTH + SSD_WIDTH, :])
           + _dot(yr_ref[...], w_ref[MLA_WIDTH + SSD_WIDTH:, :]))
    xn = x_ref[...] + gate_ref[0] * acc
    o_ref[...] = _rms(xn, fnw_ref[...]) if final else xn


def _outproj_call(y_mla, y_ssd, y_rwkv, x2, gate, w_out, final_norm_w, seq, final):
    t, d = x2.shape
    tm = min(seq, 512)
    per_batch = seq // tm
    yspec = lambda wd: pl.BlockSpec((tm, wd), lambda i: (i, 0))
    return pl.pallas_call(
        functools.partial(_outproj_kernel, final=final),
        out_shape=jax.ShapeDtypeStruct((t, d), F32),
        grid=(t // tm,),
        in_specs=[yspec(MLA_WIDTH), yspec(SSD_WIDTH), yspec(RWKV_WIDTH), yspec(d),
                  pl.BlockSpec((1, 1, d), lambda i: (i // per_batch, 0, 0)),
                  pl.BlockSpec((D_MIX, d), lambda i: (0, 0)),
                  pl.BlockSpec((1, d), lambda i: (0, 0))],
        out_specs=yspec(d),
        compiler_params=_params(1), name="out_proj",
    )(y_mla, y_ssd, y_rwkv, x2, gate, w_out.astype(BF16), final_norm_w.reshape(1, d))


def kernel(x, c, positions, ada_w, ada_b, norm_w, w_in, q_norm_w, w_uq, kv_norm_w, w_ukv, conv_w, conv_b, dt_bias, a_log, d_skip, ssd_norm_w, mu_rkv, mu_w, mu_a, w0, w_lora_b, a0, a_lora_b, k_k, k_a, r_k, lnx_w, lnx_b, w_out, final_norm_w):
    batch, seq, d = x.shape
    depth = ada_w.shape[0]
    assert d == D_MODEL and batch <= 8 and seq % CHUNK == 0
    t = batch * seq
    x2 = x.reshape(t, d)
    mod = _mod_call(jnp.pad(c, ((0, 8 - batch), (0, 0))), ada_w, ada_b)
    rope_tab = _rope_call(positions.reshape(t, 1))
    for l in range(depth):
        shift = mod[l, :batch, 0:d].reshape(batch, 1, d)
        scale = mod[l, :batch, d:2 * d].reshape(batch, 1, d)
        gate = mod[l, :batch, 2 * d:].reshape(batch, 1, d)
        proj = _inproj_call(x2, scale, shift, norm_w[l], _pack_w_in(w_in[l]), seq)
        q, k, v = _mla_prep_call(proj, rope_tab, q_norm_w[l], w_uq[l], kv_norm_w[l], w_ukv[l], batch, seq)
        y_mla = _attn_call(q, k, v, proj, batch, seq)
        y_ssd = _ssd_call(proj, conv_w[l], conv_b[l], dt_bias[l], a_log[l], d_skip[l], ssd_norm_w[l],
                          batch, seq)
        y_rwkv = _rwkv_call(proj, mu_rkv[l], mu_w[l], mu_a[l], w0[l], w_lora_b[l], a0[l], a_lora_b[l],
                            k_k[l], k_a[l], r_k[l], lnx_w[l], lnx_b[l], batch, seq)
        x2 = _outproj_call(y_mla, y_ssd, y_rwkv, x2, gate, w_out[l], final_norm_w, seq,
                           final=(l == depth - 1))
    return x2.reshape(batch, seq, d)
```

```python
import functools

import jax
import jax.numpy as jnp
import numpy as np
from jax import lax
from jax.experimental import pallas as pl
from jax.experimental.pallas import tpu as pltpu

F32 = jnp.float32
BF16 = jnp.bfloat16

D_MODEL = 1024
CHUNK = 64
CHUNK_SHIFT = 6
NORM_EPS = 1e-6

MLA_HEADS = 4
MLA_Q_RANK = 256
MLA_KV_RANK = 128
MLA_NOPE = 128
MLA_ROPE = 64
MLA_V = 128
MLA_WIDTH = MLA_HEADS * MLA_V
MLA_QK_PAD = 256
MLA_VT_ROWS = 144
ATTN_Q_SUB = 256
ATTN_HEADS_PER_ITER = 4
LOG2_E = 1.4426950408889634
ROPE_THETA = 10000.0

SSD_HEADS = 8
SSD_HEAD_DIM = 64
SSD_WIDTH = SSD_HEADS * SSD_HEAD_DIM
SSD_GROUPS = 2
SSD_STATE = 128
SSD_CONV = 4
SSD_XBC = SSD_WIDTH + 2 * SSD_GROUPS * SSD_STATE

RWKV_HEADS = 8
RWKV_HEAD_DIM = 64
RWKV_WIDTH = RWKV_HEADS * RWKV_HEAD_DIM
DECAY_LORA = 64
ICLR_LORA = 64
DECAY_SCALE = 0.606531
GN_EPS = 64e-5

D_MIX = MLA_WIDTH + SSD_WIDTH + RWKV_WIDTH

LANES = 128
PAIR = 2 * RWKV_HEAD_DIM
N_PAIRS = RWKV_HEADS // 2
RWKV_GROUP_CHUNKS = 2

COL_XBC = 0
COL_GMLA = 1024
COL_Z = 1536
COL_R = 2048
COL_K = 2560
COL_V = 3072
COL_GRWKV = 3584
COL_LAT = 4096
COL_MISC = 4608
PROJ_COLS = 5120
PROJ_BLOCK = 512

VMEM_LIMIT_BYTES = 56 * 1024 * 1024


def _params(n_axes):
    return pltpu.CompilerParams(dimension_semantics=("arbitrary",) * n_axes,
                                vmem_limit_bytes=VMEM_LIMIT_BYTES)


def _dot(a, b):
    return jnp.dot(a.astype(BF16), b.astype(BF16), preferred_element_type=F32)


def _dot_nt(a, b):
    return lax.dot_general(a.astype(BF16), b.astype(BF16), (((1,), (1,)), ((), ())),
                           preferred_element_type=F32)


def _dot_tn(a, b):
    return lax.dot_general(a.astype(BF16), b.astype(BF16), (((0,), (0,)), ((), ())),
                           preferred_element_type=F32)


def _split(x):
    hi = x.astype(BF16)
    lo = (x - hi.astype(F32)).astype(BF16)
    return hi, lo


def _dot_exact_rhs(a, b_bf16):
    hi, lo = _split(a)
    return (jnp.dot(hi, b_bf16, preferred_element_type=F32)
            + jnp.dot(lo, b_bf16, preferred_element_type=F32))


def _dot_exact_lhs(a_bf16, b):
    hi, lo = _split(b)
    return (jnp.dot(a_bf16, hi, preferred_element_type=F32)
            + jnp.dot(a_bf16, lo, preferred_element_type=F32))


def _silu(x):
    return x * jax.nn.sigmoid(x)


def _iota(shape, dim):
    return lax.broadcasted_iota(jnp.int32, shape, dim)


def _mod_kernel(c_ref, w_ref, b_ref, o_ref):
    c = c_ref[...]
    ch, cl = _split(_silu(c))
    wh, wl = _split(w_ref[0])
    acc = (jnp.dot(ch, wh, preferred_element_type=F32)
           + jnp.dot(cl, wh, preferred_element_type=F32)
           + jnp.dot(ch, wl, preferred_element_type=F32))
    o_ref[0] = acc + b_ref[0]


def _mod_call(c8, ada_w, ada_b):
    depth, d, d3 = ada_w.shape
    tn = 512
    return pl.pallas_call(
        _mod_kernel,
        out_shape=jax.ShapeDtypeStruct((depth, 8, d3), F32),
        grid=(depth, d3 // tn),
        in_specs=[pl.BlockSpec((8, d), lambda l, j: (0, 0)),
                  pl.BlockSpec((1, d, tn), lambda l, j: (l, 0, j)),
                  pl.BlockSpec((1, 1, tn), lambda l, j: (l, 0, j))],
        out_specs=pl.BlockSpec((1, 8, tn), lambda l, j: (l, 0, j)),
        compiler_params=_params(2), name="adaln_mod",
    )(c8, ada_w, ada_b.reshape(depth, 1, d3))


def _rope_kernel(pos_ref, freq_ref, sign_ref, o_ref):
    ang = pos_ref[...].astype(F32) * freq_ref[...]
    o_ref[:, 0:MLA_ROPE] = jnp.cos(ang)
    o_ref[:, MLA_ROPE:2 * MLA_ROPE] = jnp.sin(ang) * sign_ref[...]


def _rope_call(pos_col):
    t = pos_col.shape[0]
    tr = min(t, 1024)
    half = MLA_ROPE // 2
    inv_freq = ROPE_THETA ** (-jnp.arange(half, dtype=F32) / half)
    freq = jnp.concatenate([inv_freq, inv_freq]).reshape(1, MLA_ROPE)
    sign = jnp.concatenate([-jnp.ones((half,), F32), jnp.ones((half,), F32)]).reshape(1, MLA_ROPE)
    return pl.pallas_call(
        _rope_kernel,
        out_shape=jax.ShapeDtypeStruct((t, 2 * MLA_ROPE), F32),
        grid=(t // tr,),
        in_specs=[pl.BlockSpec((tr, 1), lambda i: (i, 0)),
                  pl.BlockSpec((1, MLA_ROPE), lambda i: (0, 0)),
                  pl.BlockSpec((1, MLA_ROPE), lambda i: (0, 0))],
        out_specs=pl.BlockSpec((tr, 2 * MLA_ROPE), lambda i: (i, 0)),
        compiler_params=_params(1), name="rope_table",
    )(pos_col, freq, sign)


def _inproj_kernel(x_ref, scale_ref, shift_ref, nw_ref, w_ref, o_ref, h_scr):
    @pl.when(pl.program_id(1) == 0)
    def _():
        x = x_ref[...]
        ms = jnp.mean(x * x, axis=-1, keepdims=True)
        y = x * lax.rsqrt(ms + NORM_EPS) * nw_ref[...]
        h = y * (1.0 + scale_ref[0]) + shift_ref[0]
        h_scr[...] = h.astype(BF16)

    o_ref[...] = jnp.dot(h_scr[...], w_ref[...], preferred_element_type=F32)


def _inproj_call(x2, scale, shift, norm_w, wp, seq):
    t, d = x2.shape
    tm = min(seq, 1024)
    per_batch = seq // tm
    return pl.pallas_call(
        _inproj_kernel,
        out_shape=jax.ShapeDtypeStruct((t, PROJ_COLS), F32),
        grid=(t // tm, PROJ_COLS // PROJ_BLOCK),
        in_specs=[pl.BlockSpec((tm, d), lambda i, j: (i, 0)),
                  pl.BlockSpec((1, 1, d), lambda i, j: (i // per_batch, 0, 0)),
                  pl.BlockSpec((1, 1, d), lambda i, j: (i // per_batch, 0, 0)),
                  pl.BlockSpec((1, d), lambda i, j: (0, 0)),
                  pl.BlockSpec((d, PROJ_BLOCK), lambda i, j: (0, j))],
        out_specs=pl.BlockSpec((tm, PROJ_BLOCK), lambda i, j: (i, j)),
        scratch_shapes=[pltpu.VMEM((tm, d), BF16)],
        compiler_params=_params(2), name="in_proj",
    )(x2, scale, shift, norm_w.reshape(1, d), wp)


def _pack_w_in(w):
    d = w.shape[0]
    o = np.cumsum((0, MLA_Q_RANK, MLA_KV_RANK, MLA_ROPE, MLA_WIDTH, SSD_WIDTH, SSD_XBC, SSD_HEADS,
                   RWKV_WIDTH, RWKV_WIDTH, RWKV_WIDTH, DECAY_LORA, ICLR_LORA, RWKV_WIDTH))
    q_lat, kv_lat, k_pe, g_mla, z, xbc, dt, r, k, v, w_lo, a_lo, g_rwkv = (
        w[:, o[i]:o[i + 1]] for i in range(13))
    half = MLA_ROPE // 2
    k_pe_sw = jnp.concatenate([k_pe[:, half:], k_pe[:, :half]], axis=1)
    misc_pad = jnp.zeros((d, PROJ_BLOCK - DECAY_LORA - ICLR_LORA - SSD_HEADS), w.dtype)
    return jnp.concatenate([xbc, g_mla, z, r, k, v, g_rwkv, q_lat, kv_lat, k_pe, k_pe_sw,
                            w_lo, a_lo, dt, misc_pad], axis=1).astype(BF16)


def _rms(x, w):
    ms = jnp.mean(x * x, axis=-1, keepdims=True)
    return x * lax.rsqrt(ms + NORM_EPS) * w


def _mla_prep_kernel(lat_ref, rope_ref, qnw_ref, wuq_ref, kvnw_ref, wukv_ref, q_ref, k_ref, v_ref):
    lat = lat_ref[...]
    cs = rope_ref[...]
    q = _dot(_rms(lat[:, 0:MLA_Q_RANK], qnw_ref[...]), wuq_ref[...])
    kv = _dot(_rms(lat[:, MLA_Q_RANK:MLA_Q_RANK + MLA_KV_RANK], kvnw_ref[...]), wukv_ref[...])
    kp = lat[:, MLA_Q_RANK + MLA_KV_RANK:] * cs
    k_pe = (kp[:, 0:MLA_ROPE] + kp[:, MLA_ROPE:]).astype(BF16)
    rows = lat.shape[0]
    zpad = jnp.zeros((rows, MLA_QK_PAD - MLA_NOPE - MLA_ROPE), BF16)
    scale = (MLA_NOPE + MLA_ROPE) ** -0.5 * LOG2_E
    ones_row = (_iota((MLA_VT_ROWS - MLA_V, rows), 0) == 0).astype(BF16)
    for h in range(MLA_HEADS):
        qh = q[:, 256 * h:256 * (h + 1)]
        qp = qh[:, MLA_NOPE:] * cs
        q_pe = qp[:, 0:MLA_ROPE] + qp[:, MLA_ROPE:]
        q_ref[0, h] = jnp.concatenate([(qh[:, 0:MLA_NOPE] * scale).astype(BF16),
                                       (q_pe * scale).astype(BF16), zpad], axis=1)
        k_ref[0, h] = jnp.concatenate([kv[:, 256 * h:256 * h + MLA_NOPE].astype(BF16), k_pe, zpad],
                                      axis=1)
        v_ref[0, h, 0:MLA_V, :] = kv[:, 256 * h + MLA_NOPE:256 * (h + 1)].T.astype(BF16)
        v_ref[0, h, MLA_V:, :] = ones_row


def _pack_w_uq(w_uq):
    half = MLA_ROPE // 2
    per = MLA_NOPE + MLA_ROPE
    cols = []
    for h in range(MLA_HEADS):
        wh = w_uq[:, per * h:per * (h + 1)]
        pe = wh[:, MLA_NOPE:]
        cols += [wh, pe[:, half:], pe[:, :half]]
    return jnp.concatenate(cols, axis=1).astype(BF16)


def _mla_prep_call(proj, rope_tab, q_norm_w, w_uq, kv_norm_w, w_ukv, batch, seq):
    tm = min(seq, 512)
    per_batch = seq // tm
    lat_blk = COL_LAT // PROJ_BLOCK
    qk_shape = jax.ShapeDtypeStruct((batch, MLA_HEADS, seq, MLA_QK_PAD), BF16)
    qk_spec = pl.BlockSpec((1, MLA_HEADS, tm, MLA_QK_PAD), lambda i: (i // per_batch, 0, i % per_batch, 0))
    return pl.pallas_call(
        _mla_prep_kernel,
        out_shape=(qk_shape, qk_shape, jax.ShapeDtypeStruct((batch, MLA_HEADS, MLA_VT_ROWS, seq), BF16)),
        grid=(batch * per_batch,),
        in_specs=[pl.BlockSpec((tm, PROJ_BLOCK), lambda i: (i, lat_blk)),
                  pl.BlockSpec((tm, 2 * MLA_ROPE), lambda i: (i, 0)),
                  pl.BlockSpec((1, MLA_Q_RANK), lambda i: (0, 0)),
                  pl.BlockSpec((MLA_Q_RANK, 256 * MLA_HEADS), lambda i: (0, 0)),
                  pl.BlockSpec((1, MLA_KV_RANK), lambda i: (0, 0)),
                  pl.BlockSpec((MLA_KV_RANK, 256 * MLA_HEADS), lambda i: (0, 0))],
        out_specs=(qk_spec, qk_spec,
                   pl.BlockSpec((1, MLA_HEADS, MLA_VT_ROWS, tm), lambda i: (i // per_batch, 0, 0, i % per_batch))),
        compiler_params=_params(1), name="mla_prep",
    )(proj, rope_tab, q_norm_w.reshape(1, -1), _pack_w_uq(w_uq), kv_norm_w.reshape(1, -1),
      w_ukv.astype(BF16))


def _attn_kernel(qi_tab, ki_tab, q_ref, k_ref, vt_ref, g_ref, o_ref, m_scr, acc_scr):
    p = pl.program_id(1)
    qi = qi_tab[p]
    ki = ki_tab[p]
    tq = q_ref.shape[2]
    sub = min(ATTN_Q_SUB, tq)

    @pl.when(ki == 0)
    def _():
        m_scr[...] = jnp.full(m_scr.shape, -jnp.inf, F32)
        acc_scr[...] = jnp.zeros(acc_scr.shape, F32)

    def step(diagonal):
        def head_body(hi, carry):
            chains = [(hi * ATTN_HEADS_PER_ITER + hh, j) for hh in range(ATTN_HEADS_PER_ITER)
                      for j in range(tq // sub)]
            ids = range(len(chains))
            head = [h for h, _ in chains]
            cols = [slice(sub * j, sub * (j + 1)) for _, j in chains]
            n_keys = [sub * (j + 1) if diagonal else tq for _, j in chains]
            s = [lax.dot_general(k_ref[0, head[i], 0:n_keys[i], :], q_ref[0, head[i], cols[i], :],
                                 (((1,), (1,)), ((), ())), preferred_element_type=F32) for i in ids]
            if diagonal:
                for i in ids:
                    key = lax.shift_right_logical(_iota(s[i].shape, 0), CHUNK_SHIFT)
                    qry = lax.shift_right_logical(_iota(s[i].shape, 1) + sub * chains[i][1], CHUNK_SHIFT)
                    s[i] = jnp.where(key <= qry, s[i], -jnp.inf)
            m_prev = [m_scr[head[i], 0:1, cols[i]] for i in ids]
            m_new = [jnp.maximum(m_prev[i], jnp.max(s[i], axis=0, keepdims=True)) for i in ids]
            pm = [jnp.exp2(s[i] - m_new[i]).astype(BF16) for i in ids]
            pv = [jnp.dot(vt_ref[0, head[i], :, 0:n_keys[i]], pm[i], preferred_element_type=F32)
                  for i in ids]
            for i in ids:
                acc_scr[head[i], :, cols[i]] = (jnp.exp2(m_prev[i] - m_new[i]) * acc_scr[head[i], :, cols[i]]
                                                + pv[i])
                m_scr[head[i], 0:1, cols[i]] = m_new[i]
            return carry

        lax.fori_loop(0, MLA_HEADS // ATTN_HEADS_PER_ITER, head_body, 0)

    @pl.when(ki < qi)
    def _():
        step(False)

    @pl.when(ki == qi)
    def _():
        step(True)
        g = g_ref[...]
        for h in range(MLA_HEADS):
            acc = acc_scr[h]
            o = (acc[0:MLA_V, :] / acc[MLA_V:MLA_V + 1, :]).T
            o_ref[:, MLA_V * h:MLA_V * (h + 1)] = o * _silu(g[:, MLA_V * h:MLA_V * (h + 1)])


def _attn_call(q, k, vt, proj, batch, seq):
    tq = min(seq, 1024)
    nq = seq // tq
    pairs = [(a, b) for a in range(nq) for b in range(a + 1)]
    qi_tab = jnp.asarray([a for a, _ in pairs], jnp.int32)
    ki_tab = jnp.asarray([b for _, b in pairs], jnp.int32)
    grid_spec = pltpu.PrefetchScalarGridSpec(
        num_scalar_prefetch=2,
        grid=(batch, len(pairs)),
        in_specs=[pl.BlockSpec((1, MLA_HEADS, tq, MLA_QK_PAD), lambda b, p, qt, kt: (b, 0, qt[p], 0)),
                  pl.BlockSpec((1, MLA_HEADS, tq, MLA_QK_PAD), lambda b, p, qt, kt: (b, 0, kt[p], 0)),
                  pl.BlockSpec((1, MLA_HEADS, MLA_VT_ROWS, tq), lambda b, p, qt, kt: (b, 0, 0, kt[p])),
                  pl.BlockSpec((tq, MLA_WIDTH), lambda b, p, qt, kt: (b * nq + qt[p], COL_GMLA // MLA_WIDTH))],
        out_specs=pl.BlockSpec((tq, MLA_WIDTH), lambda b, p, qt, kt: (b * nq + qt[p], 0)),
        scratch_shapes=[pltpu.VMEM((MLA_HEADS, 8, tq), F32),
                        pltpu.VMEM((MLA_HEADS, MLA_VT_ROWS, tq), F32)],
    )
    return pl.pallas_call(
        _attn_kernel,
        out_shape=jax.ShapeDtypeStruct((batch * seq, MLA_WIDTH), F32),
        grid_spec=grid_spec,
        compiler_params=_params(2), name="mla_attn",
    )(qi_tab, ki_tab, q, k, vt, proj)


def _tri_incl_bf16():
    return (_iota((CHUNK, CHUNK), 0) >= _iota((CHUNK, CHUNK), 1)).astype(BF16)


def _ssd_kernel(xbc_ref, z_ref, misc_ref, convw_ref, convb_ref, dtb_ref, alog_ref, dskip_ref, nw_ref,
                expand_ref, o_ref, xpad_scr, state_scr, xc_scr, xdt_scr, dta_scr, y_scr):
    ts = xbc_ref.shape[0]
    halo = 8

    @pl.when(pl.program_id(1) == 0)
    def _():
        xpad_scr[0:halo, :] = jnp.zeros((halo, SSD_XBC), F32)
        state_scr[...] = jnp.zeros(state_scr.shape, F32)

    xpad_scr[halo:halo + ts, :] = xbc_ref[...]
    conv = convb_ref[...]
    for i in range(SSD_CONV):
        start = halo - (SSD_CONV - 1) + i
        conv = conv + convw_ref[i:i + 1, :] * xpad_scr[start:start + ts, :]
    xpad_scr[0:halo, :] = xpad_scr[ts:ts + halo, :]
    xc_scr[...] = _silu(conv)

    dt = jax.nn.softplus(misc_ref[:, LANES:2 * LANES] + dtb_ref[...])
    dt_x = _dot_exact_rhs(dt, expand_ref[...])
    xdt_scr[...] = xc_scr[:, 0:SSD_WIDTH] * dt_x
    dta_scr[...] = dt_x * (-jnp.exp(alog_ref[...]))

    tri = _tri_incl_bf16()
    lane = _iota((CHUNK, PAIR), 1)
    row = _iota((CHUNK, PAIR), 0)
    in_first = lane < SSD_HEAD_DIM
    src = jnp.where(in_first, lane, lane - SSD_HEAD_DIM)
    diag2 = row == src
    causal2 = row >= src
    gw = SSD_WIDTH // SSD_GROUPS

    def chunk_body(c, carry):
        r0 = pl.multiple_of(c * CHUNK, CHUNK)
        rows = pl.ds(r0, CHUNK)
        cum = _dot_exact_lhs(tri, dta_scr[rows, :])
        tot = cum[CHUNK - 1:CHUNK, :]
        e_in = jnp.exp(cum)
        xdt = xdt_scr[rows, :]
        xw = xdt * jnp.exp(tot - cum)
        for g in range(SSD_GROUPS):
            bm = xc_scr[rows, SSD_WIDTH + SSD_STATE * g:SSD_WIDTH + SSD_STATE * (g + 1)]
            cm = xc_scr[rows, SSD_WIDTH + SSD_GROUPS * SSD_STATE + SSD_STATE * g:
                        SSD_WIDTH + SSD_GROUPS * SSD_STATE + SSD_STATE * (g + 1)]
            gl = slice(gw * g, gw * (g + 1))
            cb2 = _dot_nt(cm, jnp.concatenate([bm, bm], axis=0))
            prev = state_scr[:, gl]
            y_off = _dot(cm, prev) * e_in[:, gl]
            state_scr[:, gl] = prev * jnp.exp(tot[:, gl]) + _dot_tn(bm, xw[:, gl])
            for j in range(2):
                pl_ = slice(gw * g + PAIR * j, gw * g + PAIR * (j + 1))
                colp = cum[:, pl_]
                rowp = jnp.sum(jnp.where(diag2, colp, 0.0), axis=0, keepdims=True)
                dec = jnp.exp(jnp.where(causal2, colp - rowp, -jnp.inf))
                xp = xdt[:, pl_]
                stacked = jnp.concatenate([jnp.where(in_first, xp, 0.0), jnp.where(in_first, 0.0, xp)],
                                          axis=0)
                y_scr[rows, pl_] = _dot(cb2 * dec, stacked) + y_off[:, PAIR * j:PAIR * (j + 1)]
        return carry

    lax.fori_loop(0, ts // CHUNK, chunk_body, 0)

    y = y_scr[...] + dskip_ref[...] * xc_scr[:, 0:SSD_WIDTH]
    o_ref[...] = _rms(y * _silu(z_ref[...]), nw_ref[...])


def _head_expand(vec, width):
    return jnp.repeat(vec.astype(F32), width).reshape(1, -1)


def _ssd_call(proj, conv_w, conv_b, dt_bias, a_log, d_skip, ssd_norm_w, batch, seq):
    ts = min(seq, 512)
    nt = seq // ts
    dtb = jnp.zeros((1, LANES), F32).at[0, :SSD_HEADS].set(dt_bias)
    expand = (jnp.arange(LANES)[:, None] == (jnp.arange(SSD_WIDTH) // SSD_HEAD_DIM)[None, :]).astype(BF16)
    row = lambda w: pl.BlockSpec((1, w), lambda b, i: (0, 0))
    return pl.pallas_call(
        _ssd_kernel,
        out_shape=jax.ShapeDtypeStruct((batch * seq, SSD_WIDTH), F32),
        grid=(batch, nt),
        in_specs=[pl.BlockSpec((ts, SSD_XBC), lambda b, i: (b * nt + i, COL_XBC // SSD_XBC)),
                  pl.BlockSpec((ts, PROJ_BLOCK), lambda b, i: (b * nt + i, COL_Z // PROJ_BLOCK)),
                  pl.BlockSpec((ts, PROJ_BLOCK), lambda b, i: (b * nt + i, COL_MISC // PROJ_BLOCK)),
                  pl.BlockSpec((SSD_CONV, SSD_XBC), lambda b, i: (0, 0)),
                  row(SSD_XBC), row(LANES), row(SSD_WIDTH), row(SSD_WIDTH), row(SSD_WIDTH),
                  pl.BlockSpec((LANES, SSD_WIDTH), lambda b, i: (0, 0))],
        out_specs=pl.BlockSpec((ts, SSD_WIDTH), lambda b, i: (b * nt + i, 0)),
        scratch_shapes=[pltpu.VMEM((ts + 8, SSD_XBC), F32),
                        pltpu.VMEM((SSD_STATE, SSD_WIDTH), F32),
                        pltpu.VMEM((ts, SSD_XBC), F32),
                        pltpu.VMEM((ts, SSD_WIDTH), F32),
                        pltpu.VMEM((ts, SSD_WIDTH), F32),
                        pltpu.VMEM((ts, SSD_WIDTH), F32)],
        compiler_params=_params(2), name="ssd",
    )(proj, proj, proj, conv_w, conv_b.reshape(1, -1), dtb, _head_expand(a_log, SSD_HEAD_DIM),
      _head_expand(d_skip, SSD_HEAD_DIM), ssd_norm_w.reshape(1, -1), expand)


def _pair_sum(x, in_first):
    s0 = jnp.sum(jnp.where(in_first, x, 0.0), axis=-1, keepdims=True)
    s1 = jnp.sum(jnp.where(in_first, 0.0, x), axis=-1, keepdims=True)
    return jnp.where(in_first, s0, s1)


def _stack_heads(x, in_first):
    return jnp.concatenate([jnp.where(in_first, x, 0.0), jnp.where(in_first, 0.0, x)], axis=0)


def _rwkv_kernel(r_ref, k_ref, v_ref, misc_ref, g_ref, mur_ref, muk_ref, muv_ref, muwa_ref, w0_ref,
                 wlb_ref, a0_ref, alb_ref, kk_ref, ka_ref, rk_ref, lnw_ref, lnb_ref, o_ref,
                 prev_scr, state_scr, r_scr, lw_scr, k_scr, v_scr, a_scr, b_scr, y_scr, rg_scr, yh_scr):
    ts = r_ref.shape[0]

    @pl.when(pl.program_id(1) == 0)
    def _():
        prev_scr[...] = jnp.zeros(prev_scr.shape, F32)
        state_scr[...] = jnp.zeros(state_scr.shape, F32)

    first_row = _iota((ts, 1), 0) == 0

    def shift(x, mu, slot):
        w = x.shape[1]
        prev = jnp.where(first_row, prev_scr[slot:slot + 1, 0:w], pltpu.roll(x, 1, 0))
        prev_scr[slot:slot + 1, 0:w] = x[ts - 1:ts, :]
        return x + (prev - x) * mu

    r = shift(r_ref[...], mur_ref[...], 0)
    k = shift(k_ref[...], muk_ref[...], 1)
    v = shift(v_ref[...], muv_ref[...], 2)
    wa = shift(misc_ref[:, 0:LANES], muwa_ref[...], 3)

    lw_scr[...] = -DECAY_SCALE * jax.nn.sigmoid(w0_ref[...] + _dot(jnp.tanh(wa), wlb_ref[...]))
    a = jax.nn.sigmoid(a0_ref[...] + _dot(wa, alb_ref[...]))

    in_first_t = _iota((ts, PAIR), 1) < RWKV_HEAD_DIM
    kk = k * kk_ref[...]
    k2 = k * (1.0 + (a - 1.0) * ka_ref[...])
    rkr = r * k2 * rk_ref[...]
    for p in range(N_PAIRS):
        sl = slice(PAIR * p, PAIR * (p + 1))
        kkp = kk[:, sl]
        kkn = kkp / jnp.maximum(jnp.sqrt(_pair_sum(kkp * kkp, in_first_t)), 1e-12)
        a_scr[:, sl] = -kkn
        b_scr[:, sl] = kkn * a[:, sl]
        y_scr[:, sl] = _pair_sum(rkr[:, sl], in_first_t)
    bonus = y_scr[...] * v
    r_scr[...] = r
    k_scr[...] = k2
    v_scr[...] = v

    tri = _tri_incl_bf16()
    in_first = _iota((CHUNK, PAIR), 1) < RWKV_HEAD_DIM
    rr = _iota((PAIR, PAIR), 0)
    cc = _iota((PAIR, PAIR), 1)
    same = lax.shift_right_logical(rr, CHUNK_SHIFT) == lax.shift_right_logical(cc, CHUNK_SHIFT)
    strict = same & (rr > cc)
    incl = same & (rr >= cc)
    eye_b = rr == cc
    eye = eye_b.astype(F32)
    zeros_pp = jnp.zeros((PAIR, PAIR), BF16)
    n_chunks = ts // CHUNK
    group = RWKV_GROUP_CHUNKS if n_chunks % RWKV_GROUP_CHUNKS == 0 else 1

    def group_body(gi, carry):
        atm, rtm, vm, bkh, bke, wend = [], [], [], [], [], []
        for ci in range(group):
            r0 = pl.multiple_of((gi * group + ci) * CHUNK, CHUNK)
            rows = pl.ds(r0, CHUNK)
            lw = lw_scr[rows, :]
            cum = _dot_exact_lhs(tri, lw)
            tot = cum[CHUNK - 1:CHUNK, :]
            e_neg = jnp.exp(-cum)
            e_end = jnp.exp(tot - cum)
            w_end = jnp.exp(tot)
            rc, kc, vc, ac, bc = (r_scr[rows, :], k_scr[rows, :], v_scr[rows, :], a_scr[rows, :],
                                  b_scr[rows, :])
            a_t = ac * jnp.exp(cum - lw)
            r_t = rc * jnp.exp(cum)
            b_h = bc * e_neg
            k_h = kc * e_neg
            b_e = bc * e_end
            k_e = kc * e_end
            for p in range(N_PAIRS):
                sl = slice(PAIR * p, PAIR * (p + 1))
                st = lambda x: _stack_heads(x[:, sl], in_first)
                atm.append(st(a_t))
                rtm.append(st(r_t))
                vm.append(st(vc).astype(BF16))
                bkh.append(jnp.concatenate([st(b_h), st(k_h)], axis=0).astype(BF16))
                bke.append(jnp.concatenate([st(b_e), st(k_e)], axis=0).astype(BF16))
                wend.append(w_end[:, sl])
        ids = range(len(atm))
        amat = [_dot_nt(jnp.concatenate([atm[i], rtm[i]], axis=0), bkh[i]) for i in ids]
        n_ab = [jnp.where(strict, amat[i][0:PAIR, 0:PAIR], 0.0) for i in ids]
        a_ak = [jnp.where(strict, amat[i][0:PAIR, PAIR:], 0.0) for i in ids]
        a_r = [jnp.concatenate([jnp.where(incl, amat[i][PAIR:, 0:PAIR], 0.0),
                                jnp.where(incl, amat[i][PAIR:, PAIR:], 0.0)], axis=1).astype(BF16)
               for i in ids]
        akv = [_dot(a_ak[i], vm[i]) for i in ids]
        npow = [_dot(n_ab[i], n_ab[i]) for i in ids]
        tinv = [eye + n_ab[i] for i in ids]
        for _ in range(CHUNK_SHIFT - 2):
            both = [_dot(npow[i], jnp.concatenate([npow[i], tinv[i]], axis=1)) for i in ids]
            npow = [both[i][:, 0:PAIR] for i in ids]
            tinv = [tinv[i] + both[i][:, PAIR:] for i in ids]
        last = [_dot(npow[i], tinv[i]) for i in ids]
        tinv = [tinv[i] + last[i] for i in ids]
        au = [_dot(tinv[i], jnp.concatenate([atm[i], akv[i]], axis=1)) for i in ids]
        rhs = [jnp.concatenate([au[i].astype(BF16), jnp.concatenate([zeros_pp, vm[i]], axis=1)], axis=0)
               for i in ids]
        ry = [jnp.dot(a_r[i], rhs[i], preferred_element_type=F32) for i in ids]
        gh = [_dot_tn(bke[i], rhs[i]) for i in ids]
        for i in ids:
            c = gi * group + i // N_PAIRS
            p = i % N_PAIRS
            gmat = gh[i][:, 0:PAIR] + jnp.where(eye_b, wend[i], 0.0)
            rg_scr[c, p] = jnp.concatenate([rtm[i] + ry[i][:, 0:PAIR], gmat], axis=0).astype(BF16)
            yh_scr[c, p] = jnp.concatenate([ry[i][:, PAIR:], gh[i][:, PAIR:]], axis=0)
        return carry

    lax.fori_loop(0, n_chunks // group, group_body, 0)

    def sweep_body(c, carry):
        r0 = pl.multiple_of(c * CHUNK, CHUNK)
        s2 = []
        for p in range(N_PAIRS):
            s_hi, s_lo = _split(state_scr[p])
            s2.append(jnp.concatenate([s_hi, s_lo], axis=1))
        out = [jnp.dot(rg_scr[c, p], s2[p], preferred_element_type=F32) for p in range(N_PAIRS)]
        for p in range(N_PAIRS):
            tot = yh_scr[c, p] + out[p][:, 0:PAIR] + out[p][:, PAIR:]
            state_scr[p] = tot[PAIR:, :]
            y_scr[pl.ds(r0, CHUNK), PAIR * p:PAIR * (p + 1)] = tot[0:CHUNK, :] + tot[CHUNK:PAIR, :]
        return carry

    lax.fori_loop(0, n_chunks, sweep_body, 0)

    g = g_ref[...]
    for p in range(N_PAIRS):
        sl = slice(PAIR * p, PAIR * (p + 1))
        y = y_scr[:, sl]
        mean = _pair_sum(y, in_first_t) * (1.0 / RWKV_HEAD_DIM)
        d = y - mean
        var = _pair_sum(d * d, in_first_t) * (1.0 / RWKV_HEAD_DIM)
        yn = d * lax.rsqrt(var + GN_EPS) * lnw_ref[:, sl] + lnb_ref[:, sl]
        o_ref[:, sl] = (yn + bonus[:, sl]) * _silu(g[:, sl])


def _rwkv_call(proj, mu_rkv, mu_w, mu_a, w0, w_lora_b, a0, a_lora_b, k_k, k_a, r_k, lnx_w, lnx_b,
               batch, seq):
    ts = min(seq, 512)
    nt = seq // ts
    w = RWKV_WIDTH
    muwa = jnp.concatenate([mu_w, mu_a]).reshape(1, LANES)
    zl = jnp.zeros((DECAY_LORA, w), F32)
    wlb = jnp.concatenate([w_lora_b, zl], axis=0).astype(BF16)
    alb = jnp.concatenate([zl, a_lora_b], axis=0).astype(BF16)
    blk = lambda col: pl.BlockSpec((ts, PROJ_BLOCK), lambda b, i: (b * nt + i, col // PROJ_BLOCK))
    row = lambda width: pl.BlockSpec((1, width), lambda b, i: (0, 0))
    lora = pl.BlockSpec((LANES, w), lambda b, i: (0, 0))
    tile = pltpu.VMEM((ts, w), F32)
    return pl.pallas_call(
        _rwkv_kernel,
        out_shape=jax.ShapeDtypeStruct((batch * seq, w), F32),
        grid=(batch, nt),
        in_specs=[blk(COL_R), blk(COL_K), blk(COL_V), blk(COL_MISC), blk(COL_GRWKV),
                  row(w), row(w), row(w), row(LANES), row(w), lora, row(w), lora,
                  row(w), row(w), row(w), row(w), row(w)],
        out_specs=pl.BlockSpec((ts, w), lambda b, i: (b * nt + i, 0)),
        scratch_shapes=[pltpu.VMEM((8, w), F32),
                        pltpu.VMEM((N_PAIRS, PAIR, PAIR), F32),
                        tile, tile, tile, tile, tile, tile, tile,
                        pltpu.VMEM((ts // CHUNK, N_PAIRS, 2 * PAIR, PAIR), BF16),
                        pltpu.VMEM((ts // CHUNK, N_PAIRS, 2 * PAIR, PAIR), F32)],
        compiler_params=_params(2), name="rwkv7",
    )(proj, proj, proj, proj, proj,
      mu_rkv[0:1], mu_rkv[1:2], mu_rkv[2:3], muwa, w0.reshape(1, w), wlb, a0.reshape(1, w), alb,
      k_k.reshape(1, w), k_a.reshape(1, w), r_k.reshape(1, w), lnx_w.reshape(1, w), lnx_b.reshape(1, w))


def _outproj_kernel(ym_ref, ys_ref, yr_ref, x_ref, gate_ref, w_ref, fnw_ref, o_ref, *, final):
    acc = (_dot(ym_ref[...], w_ref[0:MLA_WIDTH, :])
           + _dot(ys_ref[...], w_ref[MLA_WIDTH:MLA_WIDTH + SSD_WIDTH, :])
           + _dot(yr_ref[...], w_ref[MLA_WIDTH + SSD_WIDTH:, :]))
    xn = x_ref[...] + gate_ref[0] * acc
    o_ref[...] = _rms(xn, fnw_ref[...]) if final else xn


def _outproj_call(y_mla, y_ssd, y_rwkv, x2, gate, w_out, final_norm_w, seq, final):
    t, d = x2.shape
    tm = min(seq, 512)
    per_batch = seq // tm
    yspec = lambda wd: pl.BlockSpec((tm, wd), lambda i: (i, 0))
    return pl.pallas_call(
        functools.partial(_outproj_kernel, final=final),
        out_shape=jax.ShapeDtypeStruct((t, d), F32),
        grid=(t // tm,),
        in_specs=[yspec(MLA_WIDTH), yspec(SSD_WIDTH), yspec(RWKV_WIDTH), yspec(d),
                  pl.BlockSpec((1, 1, d), lambda i: (i // per_batch, 0, 0)),
                  pl.BlockSpec((D_MIX, d), lambda i: (0, 0)),
                  pl.BlockSpec((1, d), lambda i: (0, 0))],
        out_specs=yspec(d),
        compiler_params=_params(1), name="out_proj",
    )(y_mla, y_ssd, y_rwkv, x2, gate, w_out.astype(BF16), final_norm_w.reshape(1, d))


def kernel(x, c, positions, ada_w, ada_b, norm_w, w_in, q_norm_w, w_uq, kv_norm_w, w_ukv, conv_w, conv_b, dt_bias, a_log, d_skip, ssd_norm_w, mu_rkv, mu_w, mu_a, w0, w_lora_b, a0, a_lora_b, k_k, k_a, r_k, lnx_w, lnx_b, w_out, final_norm_w):
    batch, seq, d = x.shape
    depth = ada_w.shape[0]
    assert d == D_MODEL and batch <= 8 and seq % CHUNK == 0
    t = batch * seq
    x2 = x.reshape(t, d)
    mod = _mod_call(jnp.pad(c, ((0, 8 - batch), (0, 0))), ada_w, ada_b)
    rope_tab = _rope_call(positions.reshape(t, 1))
    for l in range(depth):
        shift = mod[l, :batch, 0:d].reshape(batch, 1, d)
        scale = mod[l, :batch, d:2 * d].reshape(batch, 1, d)
        gate = mod[l, :batch, 2 * d:].reshape(batch, 1, d)
        proj = _inproj_call(x2, scale, shift, norm_w[l], _pack_w_in(w_in[l]), seq)
        q, k, v = _mla_prep_call(proj, rope_tab, q_norm_w[l], w_uq[l], kv_norm_w[l], w_ukv[l], batch, seq)
        y_mla = _attn_call(q, k, v, proj, batch, seq)
        y_ssd = _ssd_call(proj, conv_w[l], conv_b[l], dt_bias[l], a_log[l], d_skip[l], ssd_norm_w[l],
                          batch, seq)
        y_rwkv = _rwkv_call(proj, mu_rkv[l], mu_w[l], mu_a[l], w0[l], w_lora_b[l], a0[l], a_lora_b[l],
                            k_k[l], k_a[l], r_k[l], lnx_w[l], lnx_b[l], batch, seq)
        x2 = _outproj_call(y_mla, y_ssd, y_rwkv, x2, gate, w_out[l], final_norm_w, seq,
                           final=(l == depth - 1))
    return x2.reshape(batch, seq, d)
```

```python
import functools

import jax
import jax.numpy as jnp
import numpy as np
from jax import lax
from jax.experimental import pallas as pl
from jax.experimental.pallas import tpu as pltpu

F32 = jnp.float32
BF16 = jnp.bfloat16

D_MODEL = 1024
CHUNK = 64
CHUNK_SHIFT = 6
NORM_EPS = 1e-6

MLA_HEADS = 4
MLA_Q_RANK = 256
MLA_KV_RANK = 128
MLA_NOPE = 128
MLA_ROPE = 64
MLA_V = 128
MLA_WIDTH = MLA_HEADS * MLA_V
MLA_QK_PAD = 256
MLA_VT_ROWS = 144
ATTN_Q_SUB = 256
ATTN_HEADS_PER_ITER = 4
LOG2_E = 1.4426950408889634
ROPE_THETA = 10000.0

SSD_HEADS = 8
SSD_HEAD_DIM = 64
SSD_WIDTH = SSD_HEADS * SSD_HEAD_DIM
SSD_GROUPS = 2
SSD_STATE = 128
SSD_CONV = 4
SSD_XBC = SSD_WIDTH + 2 * SSD_GROUPS * SSD_STATE

RWKV_HEADS = 8
RWKV_HEAD_DIM = 64
RWKV_WIDTH = RWKV_HEADS * RWKV_HEAD_DIM
DECAY_LORA = 64
ICLR_LORA = 64
DECAY_SCALE = 0.606531
GN_EPS = 64e-5

D_MIX = MLA_WIDTH + SSD_WIDTH + RWKV_WIDTH

LANES = 128
PAIR = 2 * RWKV_HEAD_DIM
N_PAIRS = RWKV_HEADS // 2
RWKV_GROUP_CHUNKS = 2
SSD_GROUP_CHUNKS = 4

COL_XBC = 0
COL_GMLA = 1024
COL_Z = 1536
COL_R = 2048
COL_K = 2560
COL_V = 3072
COL_GRWKV = 3584
COL_LAT = 4096
COL_MISC = 4608
PROJ_COLS = 5120
PROJ_BLOCK = 512

VMEM_LIMIT_BYTES = 56 * 1024 * 1024


def _params(n_axes):
    return pltpu.CompilerParams(dimension_semantics=("arbitrary",) * n_axes,
                                vmem_limit_bytes=VMEM_LIMIT_BYTES)


def _dot(a, b):
    return jnp.dot(a.astype(BF16), b.astype(BF16), preferred_element_type=F32)


def _dot_nt(a, b):
    return lax.dot_general(a.astype(BF16), b.astype(BF16), (((1,), (1,)), ((), ())),
                           preferred_element_type=F32)


def _dot_tn(a, b):
    return lax.dot_general(a.astype(BF16), b.astype(BF16), (((0,), (0,)), ((), ())),
                           preferred_element_type=F32)


def _split(x):
    hi = x.astype(BF16)
    lo = (x - hi.astype(F32)).astype(BF16)
    return hi, lo


def _dot_exact_rhs(a, b_bf16):
    hi, lo = _split(a)
    return (jnp.dot(hi, b_bf16, preferred_element_type=F32)
            + jnp.dot(lo, b_bf16, preferred_element_type=F32))


def _dot_exact_lhs(a_bf16, b):
    hi, lo = _split(b)
    return (jnp.dot(a_bf16, hi, preferred_element_type=F32)
            + jnp.dot(a_bf16, lo, preferred_element_type=F32))


def _sigmoid(x):
    return 0.5 * jnp.tanh(0.5 * x) + 0.5


def _silu(x):
    h = 0.5 * x
    return h + h * jnp.tanh(h)


def _iota(shape, dim):
    return lax.broadcasted_iota(jnp.int32, shape, dim)


def _mod_kernel(c_ref, w_ref, b_ref, o_ref):
    c = c_ref[...]
    ch, cl = _split(_silu(c))
    wh, wl = _split(w_ref[0])
    acc = (jnp.dot(ch, wh, preferred_element_type=F32)
           + jnp.dot(cl, wh, preferred_element_type=F32)
           + jnp.dot(ch, wl, preferred_element_type=F32))
    o_ref[0] = acc + b_ref[0]


def _mod_call(c8, ada_w, ada_b):
    depth, d, d3 = ada_w.shape
    tn = 512
    return pl.pallas_call(
        _mod_kernel,
        out_shape=jax.ShapeDtypeStruct((depth, 8, d3), F32),
        grid=(depth, d3 // tn),
        in_specs=[pl.BlockSpec((8, d), lambda l, j: (0, 0)),
                  pl.BlockSpec((1, d, tn), lambda l, j: (l, 0, j)),
                  pl.BlockSpec((1, 1, tn), lambda l, j: (l, 0, j))],
        out_specs=pl.BlockSpec((1, 8, tn), lambda l, j: (l, 0, j)),
        compiler_params=_params(2), name="adaln_mod",
    )(c8, ada_w, ada_b.reshape(depth, 1, d3))


def _rope_kernel(pos_ref, freq_ref, sign_ref, o_ref):
    ang = pos_ref[...].astype(F32) * freq_ref[...]
    o_ref[:, 0:MLA_ROPE] = jnp.cos(ang)
    o_ref[:, MLA_ROPE:2 * MLA_ROPE] = jnp.sin(ang) * sign_ref[...]


def _rope_call(pos_col):
    t = pos_col.shape[0]
    tr = min(t, 1024)
    half = MLA_ROPE // 2
    inv_freq = ROPE_THETA ** (-jnp.arange(half, dtype=F32) / half)
    freq = jnp.concatenate([inv_freq, inv_freq]).reshape(1, MLA_ROPE)
    sign = jnp.concatenate([-jnp.ones((half,), F32), jnp.ones((half,), F32)]).reshape(1, MLA_ROPE)
    return pl.pallas_call(
        _rope_kernel,
        out_shape=jax.ShapeDtypeStruct((t, 2 * MLA_ROPE), F32),
        grid=(t // tr,),
        in_specs=[pl.BlockSpec((tr, 1), lambda i: (i, 0)),
                  pl.BlockSpec((1, MLA_ROPE), lambda i: (0, 0)),
                  pl.BlockSpec((1, MLA_ROPE), lambda i: (0, 0))],
        out_specs=pl.BlockSpec((tr, 2 * MLA_ROPE), lambda i: (i, 0)),
        compiler_params=_params(1), name="rope_table",
    )(pos_col, freq, sign)


def _inproj_kernel(x_ref, scale_ref, shift_ref, nw_ref, w_ref, o_ref, h_scr):
    @pl.when(pl.program_id(1) == 0)
    def _():
        x = x_ref[...]
        ms = jnp.mean(x * x, axis=-1, keepdims=True)
        y = x * lax.rsqrt(ms + NORM_EPS) * nw_ref[...]
        h = y * (1.0 + scale_ref[0]) + shift_ref[0]
        h_scr[...] = h.astype(BF16)

    o_ref[...] = jnp.dot(h_scr[...], w_ref[...], preferred_element_type=F32).astype(o_ref.dtype)


def _inproj_call(x2, scale, shift, norm_w, wp, seq):
    t, d = x2.shape
    tm = min(seq, 1024)
    per_batch = seq // tm
    return pl.pallas_call(
        _inproj_kernel,
        out_shape=jax.ShapeDtypeStruct((t, PROJ_COLS), BF16),
        grid=(t // tm, PROJ_COLS // PROJ_BLOCK),
        in_specs=[pl.BlockSpec((tm, d), lambda i, j: (i, 0)),
                  pl.BlockSpec((1, 1, d), lambda i, j: (i // per_batch, 0, 0)),
                  pl.BlockSpec((1, 1, d), lambda i, j: (i // per_batch, 0, 0)),
                  pl.BlockSpec((1, d), lambda i, j: (0, 0)),
                  pl.BlockSpec((d, PROJ_BLOCK), lambda i, j: (0, j))],
        out_specs=pl.BlockSpec((tm, PROJ_BLOCK), lambda i, j: (i, j)),
        scratch_shapes=[pltpu.VMEM((tm, d), BF16)],
        compiler_params=_params(2), name="in_proj",
    )(x2, scale, shift, norm_w.reshape(1, d), wp)


def _pack_w_in(w):
    d = w.shape[0]
    o = np.cumsum((0, MLA_Q_RANK, MLA_KV_RANK, MLA_ROPE, MLA_WIDTH, SSD_WIDTH, SSD_XBC, SSD_HEADS,
                   RWKV_WIDTH, RWKV_WIDTH, RWKV_WIDTH, DECAY_LORA, ICLR_LORA, RWKV_WIDTH))
    q_lat, kv_lat, k_pe, g_mla, z, xbc, dt, r, k, v, w_lo, a_lo, g_rwkv = (
        w[:, o[i]:o[i + 1]] for i in range(13))
    half = MLA_ROPE // 2
    k_pe_sw = jnp.concatenate([k_pe[:, half:], k_pe[:, :half]], axis=1)
    misc_pad = jnp.zeros((d, PROJ_BLOCK - DECAY_LORA - ICLR_LORA - SSD_HEADS), w.dtype)
    return jnp.concatenate([xbc, g_mla, z, r, k, v, g_rwkv, q_lat, kv_lat, k_pe, k_pe_sw,
                            w_lo, a_lo, dt, misc_pad], axis=1).astype(BF16)


def _rms(x, w):
    ms = jnp.mean(x * x, axis=-1, keepdims=True)
    return x * lax.rsqrt(ms + NORM_EPS) * w


def _mla_prep_kernel(lat_ref, rope_ref, qnw_ref, wuq_ref, kvnw_ref, wukv_ref, q_ref, k_ref, v_ref):
    lat = lat_ref[...].astype(F32)
    cs = rope_ref[...]
    q = _dot(_rms(lat[:, 0:MLA_Q_RANK], qnw_ref[...]), wuq_ref[...])
    kv = _dot(_rms(lat[:, MLA_Q_RANK:MLA_Q_RANK + MLA_KV_RANK], kvnw_ref[...]), wukv_ref[...])
    kp = lat[:, MLA_Q_RANK + MLA_KV_RANK:] * cs
    k_pe = (kp[:, 0:MLA_ROPE] + kp[:, MLA_ROPE:]).astype(BF16)
    rows = lat.shape[0]
    zpad = jnp.zeros((rows, MLA_QK_PAD - MLA_NOPE - MLA_ROPE), BF16)
    scale = (MLA_NOPE + MLA_ROPE) ** -0.5 * LOG2_E
    ones_row = (_iota((MLA_VT_ROWS - MLA_V, rows), 0) == 0).astype(BF16)
    for h in range(MLA_HEADS):
        qh = q[:, 256 * h:256 * (h + 1)]
        qp = qh[:, MLA_NOPE:] * cs
        q_pe = qp[:, 0:MLA_ROPE] + qp[:, MLA_ROPE:]
        q_ref[0, h] = jnp.concatenate([(qh[:, 0:MLA_NOPE] * scale).astype(BF16),
                                       (q_pe * scale).astype(BF16), zpad], axis=1)
        k_ref[0, h] = jnp.concatenate([kv[:, 256 * h:256 * h + MLA_NOPE].astype(BF16), k_pe, zpad],
                                      axis=1)
        v_ref[0, h, 0:MLA_V, :] = kv[:, 256 * h + MLA_NOPE:256 * (h + 1)].T.astype(BF16)
        v_ref[0, h, MLA_V:, :] = ones_row


def _pack_w_uq(w_uq):
    half = MLA_ROPE // 2
    per = MLA_NOPE + MLA_ROPE
    cols = []
    for h in range(MLA_HEADS):
        wh = w_uq[:, per * h:per * (h + 1)]
        pe = wh[:, MLA_NOPE:]
        cols += [wh, pe[:, half:], pe[:, :half]]
    return jnp.concatenate(cols, axis=1).astype(BF16)


def _mla_prep_call(proj, rope_tab, q_norm_w, w_uq, kv_norm_w, w_ukv, batch, seq):
    tm = min(seq, 512)
    per_batch = seq // tm
    lat_blk = COL_LAT // PROJ_BLOCK
    qk_shape = jax.ShapeDtypeStruct((batch, MLA_HEADS, seq, MLA_QK_PAD), BF16)
    qk_spec = pl.BlockSpec((1, MLA_HEADS, tm, MLA_QK_PAD), lambda i: (i // per_batch, 0, i % per_batch, 0))
    return pl.pallas_call(
        _mla_prep_kernel,
        out_shape=(qk_shape, qk_shape, jax.ShapeDtypeStruct((batch, MLA_HEADS, MLA_VT_ROWS, seq), BF16)),
        grid=(batch * per_batch,),
        in_specs=[pl.BlockSpec((tm, PROJ_BLOCK), lambda i: (i, lat_blk)),
                  pl.BlockSpec((tm, 2 * MLA_ROPE), lambda i: (i, 0)),
                  pl.BlockSpec((1, MLA_Q_RANK), lambda i: (0, 0)),
                  pl.BlockSpec((MLA_Q_RANK, 256 * MLA_HEADS), lambda i: (0, 0)),
                  pl.BlockSpec((1, MLA_KV_RANK), lambda i: (0, 0)),
                  pl.BlockSpec((MLA_KV_RANK, 256 * MLA_HEADS), lambda i: (0, 0))],
        out_specs=(qk_spec, qk_spec,
                   pl.BlockSpec((1, MLA_HEADS, MLA_VT_ROWS, tm), lambda i: (i // per_batch, 0, 0, i % per_batch))),
        compiler_params=_params(1), name="mla_prep",
    )(proj, rope_tab, q_norm_w.reshape(1, -1), _pack_w_uq(w_uq), kv_norm_w.reshape(1, -1),
      w_ukv.astype(BF16))


def _attn_kernel(qi_tab, ki_tab, q_ref, k_ref, vt_ref, g_ref, o_ref, m_scr, acc_scr):
    p = pl.program_id(1)
    qi = qi_tab[p]
    ki = ki_tab[p]
    tq = q_ref.shape[2]
    sub = min(ATTN_Q_SUB, tq)

    @pl.when(ki == 0)
    def _():
        m_scr[...] = jnp.full(m_scr.shape, -jnp.inf, F32)
        acc_scr[...] = jnp.zeros(acc_scr.shape, F32)

    def step(diagonal):
        def head_body(hi, carry):
            chains = [(hi * ATTN_HEADS_PER_ITER + hh, j) for hh in range(ATTN_HEADS_PER_ITER)
                      for j in range(tq // sub)]
            ids = range(len(chains))
            head = [h for h, _ in chains]
            cols = [slice(sub * j, sub * (j + 1)) for _, j in chains]
            n_keys = [sub * (j + 1) if diagonal else tq for _, j in chains]
            s = [lax.dot_general(k_ref[0, head[i], 0:n_keys[i], :], q_ref[0, head[i], cols[i], :],
                                 (((1,), (1,)), ((), ())), preferred_element_type=F32) for i in ids]
            if diagonal:
                for i in ids:
                    key = lax.shift_right_logical(_iota(s[i].shape, 0), CHUNK_SHIFT)
                    qry = lax.shift_right_logical(_iota(s[i].shape, 1) + sub * chains[i][1], CHUNK_SHIFT)
                    s[i] = jnp.where(key <= qry, s[i], -jnp.inf)
            m_prev = [m_scr[head[i], 0:1, cols[i]] for i in ids]
            m_new = [jnp.maximum(m_prev[i], jnp.max(s[i], axis=0, keepdims=True)) for i in ids]
            pm = [jnp.exp2(s[i] - m_new[i]).astype(BF16) for i in ids]
            pv = [jnp.dot(vt_ref[0, head[i], :, 0:n_keys[i]], pm[i], preferred_element_type=F32)
                  for i in ids]
            for i in ids:
                acc_scr[head[i], :, cols[i]] = (jnp.exp2(m_prev[i] - m_new[i]) * acc_scr[head[i], :, cols[i]]
                                                + pv[i])
                m_scr[head[i], 0:1, cols[i]] = m_new[i]
            return carry

        lax.fori_loop(0, MLA_HEADS // ATTN_HEADS_PER_ITER, head_body, 0)

    @pl.when(ki < qi)
    def _():
        step(False)

    @pl.when(ki == qi)
    def _():
        step(True)
        g = g_ref[...].astype(F32)
        for h in range(MLA_HEADS):
            acc = acc_scr[h]
            o = (acc[0:MLA_V, :] / acc[MLA_V:MLA_V + 1, :]).T
            o_ref[:, MLA_V * h:MLA_V * (h + 1)] = (o * _silu(g[:, MLA_V * h:MLA_V * (h + 1)])).astype(o_ref.dtype)


def _attn_call(q, k, vt, proj, batch, seq):
    tq = min(seq, 1024)
    nq = seq // tq
    pairs = [(a, b) for a in range(nq) for b in range(a + 1)]
    qi_tab = jnp.asarray([a for a, _ in pairs], jnp.int32)
    ki_tab = jnp.asarray([b for _, b in pairs], jnp.int32)
    grid_spec = pltpu.PrefetchScalarGridSpec(
        num_scalar_prefetch=2,
        grid=(batch, len(pairs)),
        in_specs=[pl.BlockSpec((1, MLA_HEADS, tq, MLA_QK_PAD), lambda b, p, qt, kt: (b, 0, qt[p], 0)),
                  pl.BlockSpec((1, MLA_HEADS, tq, MLA_QK_PAD), lambda b, p, qt, kt: (b, 0, kt[p], 0)),
                  pl.BlockSpec((1, MLA_HEADS, MLA_VT_ROWS, tq), lambda b, p, qt, kt: (b, 0, 0, kt[p])),
                  pl.BlockSpec((tq, MLA_WIDTH), lambda b, p, qt, kt: (b * nq + qt[p], COL_GMLA // MLA_WIDTH))],
        out_specs=pl.BlockSpec((tq, MLA_WIDTH), lambda b, p, qt, kt: (b * nq + qt[p], 0)),
        scratch_shapes=[pltpu.VMEM((MLA_HEADS, 8, tq), F32),
                        pltpu.VMEM((MLA_HEADS, MLA_VT_ROWS, tq), F32)],
    )
    return pl.pallas_call(
        _attn_kernel,
        out_shape=jax.ShapeDtypeStruct((batch * seq, MLA_WIDTH), BF16),
        grid_spec=grid_spec,
        compiler_params=_params(2), name="mla_attn",
    )(qi_tab, ki_tab, q, k, vt, proj)


def _tri_incl_bf16():
    return (_iota((CHUNK, CHUNK), 0) >= _iota((CHUNK, CHUNK), 1)).astype(BF16)


def _ssd_kernel(xbc_ref, z_ref, misc_ref, convw_ref, convb_ref, dtb_ref, alog_ref, dskip_ref, nw_ref,
                expand_ref, o_ref, xpad_scr, state_scr, xc_scr, xdt_scr, dta_scr, y_scr, inc_scr, dec_scr,
                pre_scr):
    ts = xbc_ref.shape[0]
    halo = 8

    @pl.when(pl.program_id(1) == 0)
    def _():
        xpad_scr[0:halo, :] = jnp.zeros((halo, SSD_XBC), F32)
        state_scr[...] = jnp.zeros(state_scr.shape, F32)

    xpad_scr[halo:halo + ts, :] = xbc_ref[...].astype(F32)
    conv = convb_ref[...]
    for i in range(SSD_CONV):
        start = halo - (SSD_CONV - 1) + i
        conv = conv + convw_ref[i:i + 1, :] * xpad_scr[start:start + ts, :]
    xpad_scr[0:halo, :] = xpad_scr[ts:ts + halo, :]
    xc_scr[...] = _silu(conv)

    dt = jax.nn.softplus(misc_ref[:, LANES:2 * LANES].astype(F32) + dtb_ref[...])
    dt_x = _dot_exact_rhs(dt, expand_ref[...])
    xdt_scr[...] = xc_scr[:, 0:SSD_WIDTH] * dt_x
    dta_scr[...] = dt_x * (-jnp.exp(alog_ref[...]))

    tri = _tri_incl_bf16()
    lane = _iota((CHUNK, PAIR), 1)
    row = _iota((CHUNK, PAIR), 0)
    in_first = lane < SSD_HEAD_DIM
    src = jnp.where(in_first, lane, lane - SSD_HEAD_DIM)
    diag2 = row == src
    causal2 = row >= src
    gw = SSD_WIDTH // SSD_GROUPS

    n_chunks = ts // CHUNK
    group = SSD_GROUP_CHUNKS if n_chunks % SSD_GROUP_CHUNKS == 0 else 1
    b_cols = [slice(SSD_WIDTH + SSD_STATE * g, SSD_WIDTH + SSD_STATE * (g + 1)) for g in range(SSD_GROUPS)]
    c_cols = [slice(SSD_WIDTH + SSD_STATE * (SSD_GROUPS + g), SSD_WIDTH + SSD_STATE * (SSD_GROUPS + g + 1))
              for g in range(SSD_GROUPS)]
    g_lanes = [slice(gw * g, gw * (g + 1)) for g in range(SSD_GROUPS)]

    def group_body(gi, carry):
        cs = range(group)
        rows = [pl.ds(pl.multiple_of((gi * group + ci) * CHUNK, CHUNK), CHUNK) for ci in cs]
        cum = [_dot_exact_lhs(tri, dta_scr[rows[ci], :]) for ci in cs]
        xdt = [xdt_scr[rows[ci], :] for ci in cs]
        bm = [[xc_scr[rows[ci], b_cols[g]] for g in range(SSD_GROUPS)] for ci in cs]
        cm = [[xc_scr[rows[ci], c_cols[g]] for g in range(SSD_GROUPS)] for ci in cs]
        tot = [cum[ci][CHUNK - 1:CHUNK, :] for ci in cs]
        xw = [xdt[ci] * jnp.exp(tot[ci] - cum[ci]) for ci in cs]
        cb2 = [[_dot_nt(cm[ci][g], jnp.concatenate([bm[ci][g], bm[ci][g]], axis=0))
                for g in range(SSD_GROUPS)] for ci in cs]
        inc = [[_dot_tn(bm[ci][g], xw[ci][:, g_lanes[g]]) for g in range(SSD_GROUPS)] for ci in cs]
        lhs, rhs = [], []
        for ci in cs:
            for p in range(N_PAIRS):
                pl_ = slice(PAIR * p, PAIR * (p + 1))
                colp = cum[ci][:, pl_]
                rowp = jnp.sum(jnp.where(diag2, colp, 0.0), axis=0, keepdims=True)
                dec = jnp.exp(jnp.where(causal2, colp - rowp, -jnp.inf))
                lhs.append(cb2[ci][p // 2] * dec)
                rhs.append(_stack_heads(xdt[ci][:, pl_], in_first))
        y_diag = [_dot(lhs[i], rhs[i]) for i in range(len(lhs))]
        for ci in cs:
            c = gi * group + ci
            for p in range(N_PAIRS):
                y_scr[rows[ci], PAIR * p:PAIR * (p + 1)] = y_diag[ci * N_PAIRS + p]
            for g in range(SSD_GROUPS):
                inc_scr[c, :, g_lanes[g]] = inc[ci][g]
            dec_scr[c] = jnp.broadcast_to(jnp.exp(tot[ci]), (8, SSD_WIDTH))
            dta_scr[rows[ci], :] = jnp.exp(cum[ci])
        return carry

    lax.fori_loop(0, n_chunks // group, group_body, 0)

    state = state_scr[...]
    for c in range(n_chunks):
        pre_scr[c] = state.astype(BF16)
        state = state * dec_scr[c, 0:1, :] + inc_scr[c]
    state_scr[...] = state

    def off_body(gi, carry):
        cs = range(group)
        rows = [pl.ds(pl.multiple_of((gi * group + ci) * CHUNK, CHUNK), CHUNK) for ci in cs]
        y_off = [[jnp.dot(xc_scr[rows[ci], c_cols[g]].astype(BF16), pre_scr[gi * group + ci, :, g_lanes[g]],
                          preferred_element_type=F32) for g in range(SSD_GROUPS)] for ci in cs]
        for ci in cs:
            for g in range(SSD_GROUPS):
                y_scr[rows[ci], g_lanes[g]] = (y_scr[rows[ci], g_lanes[g]]
                                               + y_off[ci][g] * dta_scr[rows[ci], g_lanes[g]])
        return carry

    lax.fori_loop(0, n_chunks // group, off_body, 0)

    y = y_scr[...] + dskip_ref[...] * xc_scr[:, 0:SSD_WIDTH]
    o_ref[...] = _rms(y * _silu(z_ref[...].astype(F32)), nw_ref[...]).astype(o_ref.dtype)


def _head_expand(vec, width):
    return jnp.repeat(vec.astype(F32), width).reshape(1, -1)


def _ssd_call(proj, conv_w, conv_b, dt_bias, a_log, d_skip, ssd_norm_w, batch, seq):
    ts = min(seq, 512)
    nt = seq // ts
    dtb = jnp.zeros((1, LANES), F32).at[0, :SSD_HEADS].set(dt_bias)
    expand = (jnp.arange(LANES)[:, None] == (jnp.arange(SSD_WIDTH) // SSD_HEAD_DIM)[None, :]).astype(BF16)
    row = lambda w: pl.BlockSpec((1, w), lambda b, i: (0, 0))
    return pl.pallas_call(
        _ssd_kernel,
        out_shape=jax.ShapeDtypeStruct((batch * seq, SSD_WIDTH), BF16),
        grid=(batch, nt),
        in_specs=[pl.BlockSpec((ts, SSD_XBC), lambda b, i: (b * nt + i, COL_XBC // SSD_XBC)),
                  pl.BlockSpec((ts, PROJ_BLOCK), lambda b, i: (b * nt + i, COL_Z // PROJ_BLOCK)),
                  pl.BlockSpec((ts, PROJ_BLOCK), lambda b, i: (b * nt + i, COL_MISC // PROJ_BLOCK)),
                  pl.BlockSpec((SSD_CONV, SSD_XBC), lambda b, i: (0, 0)),
                  row(SSD_XBC), row(LANES), row(SSD_WIDTH), row(SSD_WIDTH), row(SSD_WIDTH),
                  pl.BlockSpec((LANES, SSD_WIDTH), lambda b, i: (0, 0))],
        out_specs=pl.BlockSpec((ts, SSD_WIDTH), lambda b, i: (b * nt + i, 0)),
        scratch_shapes=[pltpu.VMEM((ts + 8, SSD_XBC), F32),
                        pltpu.VMEM((SSD_STATE, SSD_WIDTH), F32),
                        pltpu.VMEM((ts, SSD_XBC), F32),
                        pltpu.VMEM((ts, SSD_WIDTH), F32),
                        pltpu.VMEM((ts, SSD_WIDTH), F32),
                        pltpu.VMEM((ts, SSD_WIDTH), F32),
                        pltpu.VMEM((ts // CHUNK, SSD_STATE, SSD_WIDTH), F32),
                        pltpu.VMEM((ts // CHUNK, 8, SSD_WIDTH), F32),
                        pltpu.VMEM((ts // CHUNK, SSD_STATE, SSD_WIDTH), BF16)],
        compiler_params=_params(2), name="ssd",
    )(proj, proj, proj, conv_w, conv_b.reshape(1, -1), dtb, _head_expand(a_log, SSD_HEAD_DIM),
      _head_expand(d_skip, SSD_HEAD_DIM), ssd_norm_w.reshape(1, -1), expand)


def _pair_sum(x, in_first):
    s0 = jnp.sum(jnp.where(in_first, x, 0.0), axis=-1, keepdims=True)
    s1 = jnp.sum(jnp.where(in_first, 0.0, x), axis=-1, keepdims=True)
    return jnp.where(in_first, s0, s1)


def _stack_heads(x, in_first):
    return jnp.concatenate([jnp.where(in_first, x, 0.0), jnp.where(in_first, 0.0, x)], axis=0)


def _rwkv_kernel(r_ref, k_ref, v_ref, misc_ref, g_ref, mur_ref, muk_ref, muv_ref, muwa_ref, w0_ref,
                 wlb_ref, a0_ref, alb_ref, kk_ref, ka_ref, rk_ref, lnw_ref, lnb_ref, o_ref,
                 prev_scr, state_scr, r_scr, lw_scr, k_scr, v_scr, a_scr, b_scr, y_scr, rg_scr, yh_scr):
    ts = r_ref.shape[0]

    @pl.when(pl.program_id(1) == 0)
    def _():
        prev_scr[...] = jnp.zeros(prev_scr.shape, F32)
        state_scr[...] = jnp.zeros(state_scr.shape, F32)

    first_row = _iota((8, 1), 0) == 0

    def shift(x, mu, slot):
        w = x.shape[1]
        rolled = pltpu.roll(x, 1, 0)
        head = jnp.where(first_row, prev_scr[slot:slot + 1, 0:w], rolled[0:8, :])
        prev = jnp.concatenate([head, rolled[8:, :]], axis=0)
        prev_scr[slot:slot + 1, 0:w] = x[ts - 1:ts, :]
        return x + (prev - x) * mu

    r = shift(r_ref[...].astype(F32), mur_ref[...], 0)
    k = shift(k_ref[...].astype(F32), muk_ref[...], 1)
    v = shift(v_ref[...].astype(F32), muv_ref[...], 2)
    wa = shift(misc_ref[:, 0:LANES].astype(F32), muwa_ref[...], 3)

    lw_scr[...] = -DECAY_SCALE * _sigmoid(w0_ref[...] + _dot(jnp.tanh(wa), wlb_ref[...]))
    a = _sigmoid(a0_ref[...] + _dot(wa, alb_ref[...]))

    in_first_t = _iota((ts, PAIR), 1) < RWKV_HEAD_DIM
    kk = k * kk_ref[...]
    k2 = k * (1.0 + (a - 1.0) * ka_ref[...])
    rkr = r * k2 * rk_ref[...]
    for p in range(N_PAIRS):
        sl = slice(PAIR * p, PAIR * (p + 1))
        kkp = kk[:, sl]
        kkn = kkp * lax.rsqrt(jnp.maximum(_pair_sum(kkp * kkp, in_first_t), 1e-24))
        a_scr[:, sl] = -kkn
        b_scr[:, sl] = kkn * a[:, sl]
        y_scr[:, sl] = _pair_sum(rkr[:, sl], in_first_t)
    bonus = y_scr[...] * v
    r_scr[...] = r
    k_scr[...] = k2
    v_scr[...] = v

    tri = _tri_incl_bf16()
    in_first = _iota((CHUNK, PAIR), 1) < RWKV_HEAD_DIM
    rr = _iota((PAIR, PAIR), 0)
    cc = _iota((PAIR, PAIR), 1)
    same = lax.shift_right_logical(rr, CHUNK_SHIFT) == lax.shift_right_logical(cc, CHUNK_SHIFT)
    strict = same & (rr > cc)
    incl = same & (rr >= cc)
    eye_b = rr == cc
    eye = eye_b.astype(F32)
    zeros_pp = jnp.zeros((PAIR, PAIR), BF16)
    n_chunks = ts // CHUNK
    group = RWKV_GROUP_CHUNKS if n_chunks % RWKV_GROUP_CHUNKS == 0 else 1

    def group_body(gi, carry):
        atm, rtm, vm, bkh, bke, wend = [], [], [], [], [], []
        for ci in range(group):
            r0 = pl.multiple_of((gi * group + ci) * CHUNK, CHUNK)
            rows = pl.ds(r0, CHUNK)
            lw = lw_scr[rows, :]
            cum = _dot_exact_lhs(tri, lw)
            tot = cum[CHUNK - 1:CHUNK, :]
            e_neg = jnp.exp(-cum)
            e_end = jnp.exp(tot - cum)
            w_end = jnp.exp(tot)
            rc, kc, vc, ac, bc = (r_scr[rows, :], k_scr[rows, :], v_scr[rows, :], a_scr[rows, :],
                                  b_scr[rows, :])
            a_t = ac * jnp.exp(cum - lw)
            r_t = rc * jnp.exp(cum)
            b_h = bc * e_neg
            k_h = kc * e_neg
            b_e = bc * e_end
            k_e = kc * e_end
            for p in range(N_PAIRS):
                sl = slice(PAIR * p, PAIR * (p + 1))
                st = lambda x: _stack_heads(x[:, sl], in_first)
                atm.append(st(a_t))
                rtm.append(st(r_t))
                vm.append(st(vc).astype(BF16))
                bkh.append(jnp.concatenate([st(b_h), st(k_h)], axis=0).astype(BF16))
                bke.append(jnp.concatenate([st(b_e), st(k_e)], axis=0).astype(BF16))
                wend.append(w_end[:, sl])
        ids = range(len(atm))
        amat = [_dot_nt(jnp.concatenate([atm[i], rtm[i]], axis=0), bkh[i]) for i in ids]
        n_ab = [jnp.where(strict, amat[i][0:PAIR, 0:PAIR], 0.0) for i in ids]
        a_ak = [jnp.where(strict, amat[i][0:PAIR, PAIR:], 0.0) for i in ids]
        a_r = [jnp.concatenate([jnp.where(incl, amat[i][PAIR:, 0:PAIR], 0.0),
                                jnp.where(incl, amat[i][PAIR:, PAIR:], 0.0)], axis=1).astype(BF16)
               for i in ids]
        akv = [_dot(a_ak[i], vm[i]) for i in ids]
        npow = [_dot(n_ab[i], n_ab[i]) for i in ids]
        tinv = [eye + n_ab[i] for i in ids]
        for _ in range(CHUNK_SHIFT - 2):
            both = [_dot(npow[i], jnp.concatenate([npow[i], tinv[i]], axis=1)) for i in ids]
            npow = [both[i][:, 0:PAIR] for i in ids]
            tinv = [tinv[i] + both[i][:, PAIR:] for i in ids]
        last = [_dot(npow[i], tinv[i]) for i in ids]
        tinv = [tinv[i] + last[i] for i in ids]
        au = [_dot(tinv[i], jnp.concatenate([atm[i], akv[i]], axis=1)) for i in ids]
        rhs = [jnp.concatenate([au[i].astype(BF16), jnp.concatenate([zeros_pp, vm[i]], axis=1)], axis=0)
               for i in ids]
        ry = [jnp.dot(a_r[i], rhs[i], preferred_element_type=F32) for i in ids]
        gh = [_dot_tn(bke[i], rhs[i]) for i in ids]
        for i in ids:
            c = gi * group + i // N_PAIRS
            p = i % N_PAIRS
            gmat = gh[i][:, 0:PAIR] + jnp.where(eye_b, wend[i], 0.0)
            rg_scr[c, p] = jnp.concatenate([rtm[i] + ry[i][:, 0:PAIR], gmat], axis=0).astype(BF16)
            yh_scr[c, p] = jnp.concatenate([ry[i][:, PAIR:], gh[i][:, PAIR:]], axis=0)
        return carry

    lax.fori_loop(0, n_chunks // group, group_body, 0)

    def sweep_body(c, carry):
        r0 = pl.multiple_of(c * CHUNK, CHUNK)
        s2 = []
        for p in range(N_PAIRS):
            s_hi, s_lo = _split(state_scr[p])
            s2.append(jnp.concatenate([s_hi, s_lo], axis=1))
        out = [jnp.dot(rg_scr[c, p], s2[p], preferred_element_type=F32) for p in range(N_PAIRS)]
        for p in range(N_PAIRS):
            tot = yh_scr[c, p] + out[p][:, 0:PAIR] + out[p][:, PAIR:]
            state_scr[p] = tot[PAIR:, :]
            y_scr[pl.ds(r0, CHUNK), PAIR * p:PAIR * (p + 1)] = tot[0:CHUNK, :] + tot[CHUNK:PAIR, :]
        return carry

    lax.fori_loop(0, n_chunks, sweep_body, 0)

    g = g_ref[...].astype(F32)
    for p in range(N_PAIRS):
        sl = slice(PAIR * p, PAIR * (p + 1))
        y = y_scr[:, sl]
        mean = _pair_sum(y, in_first_t) * (1.0 / RWKV_HEAD_DIM)
        d = y - mean
        var = _pair_sum(d * d, in_first_t) * (1.0 / RWKV_HEAD_DIM)
        yn = d * lax.rsqrt(var + GN_EPS) * lnw_ref[:, sl] + lnb_ref[:, sl]
        o_ref[:, sl] = ((yn + bonus[:, sl]) * _silu(g[:, sl])).astype(o_ref.dtype)


def _rwkv_call(proj, mu_rkv, mu_w, mu_a, w0, w_lora_b, a0, a_lora_b, k_k, k_a, r_k, lnx_w, lnx_b,
               batch, seq):
    ts = min(seq, 512)
    nt = seq // ts
    w = RWKV_WIDTH
    muwa = jnp.concatenate([mu_w, mu_a]).reshape(1, LANES)
    zl = jnp.zeros((DECAY_LORA, w), F32)
    wlb = jnp.concatenate([w_lora_b, zl], axis=0).astype(BF16)
    alb = jnp.concatenate([zl, a_lora_b], axis=0).astype(BF16)
    blk = lambda col: pl.BlockSpec((ts, PROJ_BLOCK), lambda b, i: (b * nt + i, col // PROJ_BLOCK))
    row = lambda width: pl.BlockSpec((1, width), lambda b, i: (0, 0))
    lora = pl.BlockSpec((LANES, w), lambda b, i: (0, 0))
    tile = pltpu.VMEM((ts, w), F32)
    return pl.pallas_call(
        _rwkv_kernel,
        out_shape=jax.ShapeDtypeStruct((batch * seq, w), BF16),
        grid=(batch, nt),
        in_specs=[blk(COL_R), blk(COL_K), blk(COL_V), blk(COL_MISC), blk(COL_GRWKV),
                  row(w), row(w), row(w), row(LANES), row(w), lora, row(w), lora,
                  row(w), row(w), row(w), row(w), row(w)],
        out_specs=pl.BlockSpec((ts, w), lambda b, i: (b * nt + i, 0)),
        scratch_shapes=[pltpu.VMEM((8, w), F32),
                        pltpu.VMEM((N_PAIRS, PAIR, PAIR), F32),
                        tile, tile, tile, tile, tile, tile, tile,
                        pltpu.VMEM((ts // CHUNK, N_PAIRS, 2 * PAIR, PAIR), BF16),
                        pltpu.VMEM((ts // CHUNK, N_PAIRS, 2 * PAIR, PAIR), F32)],
        compiler_params=_params(2), name="rwkv7",
    )(proj, proj, proj, proj, proj,
      mu_rkv[0:1], mu_rkv[1:2], mu_rkv[2:3], muwa, w0.reshape(1, w), wlb, a0.reshape(1, w), alb,
      k_k.reshape(1, w), k_a.reshape(1, w), r_k.reshape(1, w), lnx_w.reshape(1, w), lnx_b.reshape(1, w))


def _outproj_kernel(ym_ref, ys_ref, yr_ref, x_ref, gate_ref, w_ref, fnw_ref, o_ref, *, final):
    acc = (_dot(ym_ref[...], w_ref[0:MLA_WIDTH, :])
           + _dot(ys_ref[...], w_ref[MLA_WIDTH:MLA_WIDTH + SSD_WIDTH, :])
           + _dot(yr_ref[...], w_ref[MLA_WIDTH + SSD_WIDTH:, :]))
    xn = x_ref[...] + gate_ref[0] * acc
    o_ref[...] = _rms(xn, fnw_ref[...]) if final else xn


def _outproj_call(y_mla, y_ssd, y_rwkv, x2, gate, w_out, final_norm_w, seq, final):
    t, d = x2.shape
    tm = min(seq, 512)
    per_batch = seq // tm
    yspec = lambda wd: pl.BlockSpec((tm, wd), lambda i: (i, 0))
    return pl.pallas_call(
        functools.partial(_outproj_kernel, final=final),
        out_shape=jax.ShapeDtypeStruct((t, d), F32),
        grid=(t // tm,),
        in_specs=[yspec(MLA_WIDTH), yspec(SSD_WIDTH), yspec(RWKV_WIDTH), yspec(d),
                  pl.BlockSpec((1, 1, d), lambda i: (i // per_batch, 0, 0)),
                  pl.BlockSpec((D_MIX, d), lambda i: (0, 0)),
                  pl.BlockSpec((1, d), lambda i: (0, 0))],
        out_specs=yspec(d),
        compiler_params=_params(1), name="out_proj",
    )(y_mla, y_ssd, y_rwkv, x2, gate, w_out.astype(BF16), final_norm_w.reshape(1, d))


def kernel(x, c, positions, ada_w, ada_b, norm_w, w_in, q_norm_w, w_uq, kv_norm_w, w_ukv, conv_w, conv_b, dt_bias, a_log, d_skip, ssd_norm_w, mu_rkv, mu_w, mu_a, w0, w_lora_b, a0, a_lora_b, k_k, k_a, r_k, lnx_w, lnx_b, w_out, final_norm_w):
    batch, seq, d = x.shape
    depth = ada_w.shape[0]
    assert d == D_MODEL and batch <= 8 and seq % CHUNK == 0
    t = batch * seq
    x2 = x.reshape(t, d)
    mod = _mod_call(jnp.pad(c, ((0, 8 - batch), (0, 0))), ada_w, ada_b)
    rope_tab = _rope_call(positions.reshape(t, 1))
    for l in range(depth):
        shift = mod[l, :batch, 0:d].reshape(batch, 1, d)
        scale = mod[l, :batch, d:2 * d].reshape(batch, 1, d)
        gate = mod[l, :batch, 2 * d:].reshape(batch, 1, d)
        proj = _inproj_call(x2, scale, shift, norm_w[l], _pack_w_in(w_in[l]), seq)
        q, k, v = _mla_prep_call(proj, rope_tab, q_norm_w[l], w_uq[l], kv_norm_w[l], w_ukv[l], batch, seq)
        y_mla = _attn_call(q, k, v, proj, batch, seq)
        y_ssd = _ssd_call(proj, conv_w[l], conv_b[l], dt_bias[l], a_log[l], d_skip[l], ssd_norm_w[l],
                          batch, seq)
        y_rwkv = _rwkv_call(proj, mu_rkv[l], mu_w[l], mu_a[l], w0[l], w_lora_b[l], a0[l], a_lora_b[l],
                            k_k[l], k_a[l], r_k[l], lnx_w[l], lnx_b[l], batch, seq)
        x2 = _outproj_call(y_mla, y_ssd, y_rwkv, x2, gate, w_out[l], final_norm_w, seq,
                           final=(l == depth - 1))
    return x2.reshape(batch, seq, d)
```

```python
import functools

import jax
import jax.numpy as jnp
import numpy as np
from jax import lax
from jax.experimental import pallas as pl
from jax.experimental.pallas import tpu as pltpu

F32 = jnp.float32
BF16 = jnp.bfloat16

D_MODEL = 1024
CHUNK = 64
CHUNK_SHIFT = 6
NORM_EPS = 1e-6

MLA_HEADS = 4
MLA_Q_RANK = 256
MLA_KV_RANK = 128
MLA_NOPE = 128
MLA_ROPE = 64
MLA_V = 128
MLA_WIDTH = MLA_HEADS * MLA_V
MLA_QK_PAD = 256
MLA_VT_ROWS = 144
ATTN_Q_SUB = 256
ATTN_HEADS_PER_ITER = 4
LOG2_E = 1.4426950408889634
ROPE_THETA = 10000.0

SSD_HEADS = 8
SSD_HEAD_DIM = 64
SSD_WIDTH = SSD_HEADS * SSD_HEAD_DIM
SSD_GROUPS = 2
SSD_STATE = 128
SSD_CONV = 4
SSD_XBC = SSD_WIDTH + 2 * SSD_GROUPS * SSD_STATE

RWKV_HEADS = 8
RWKV_HEAD_DIM = 64
RWKV_WIDTH = RWKV_HEADS * RWKV_HEAD_DIM
DECAY_LORA = 64
ICLR_LORA = 64
DECAY_SCALE = 0.606531
GN_EPS = 64e-5

D_MIX = MLA_WIDTH + SSD_WIDTH + RWKV_WIDTH

LANES = 128
PAIR = 2 * RWKV_HEAD_DIM
N_PAIRS = RWKV_HEADS // 2
RWKV_GROUP_CHUNKS = 8
SSD_GROUP_CHUNKS = 8

COL_XBC = 0
COL_GMLA = 1024
COL_Z = 1536
COL_R = 2048
COL_K = 2560
COL_V = 3072
COL_GRWKV = 3584
COL_LAT = 4096
COL_MISC = 4608
PROJ_COLS = 5120
PROJ_BLOCK = 512
INPROJ_BLOCK = 2560

VMEM_LIMIT_BYTES = 56 * 1024 * 1024


def _params(n_axes):
    return pltpu.CompilerParams(dimension_semantics=("arbitrary",) * n_axes,
                                vmem_limit_bytes=VMEM_LIMIT_BYTES)


def _dot(a, b):
    return jnp.dot(a.astype(BF16), b.astype(BF16), preferred_element_type=F32)


def _dot_nt(a, b):
    return lax.dot_general(a.astype(BF16), b.astype(BF16), (((1,), (1,)), ((), ())),
                           preferred_element_type=F32)


def _dot_tn(a, b):
    return lax.dot_general(a.astype(BF16), b.astype(BF16), (((0,), (0,)), ((), ())),
                           preferred_element_type=F32)


def _split(x):
    hi = x.astype(BF16)
    lo = (x - hi.astype(F32)).astype(BF16)
    return hi, lo


def _dot_exact_rhs(a, b_bf16):
    hi, lo = _split(a)
    return (jnp.dot(hi, b_bf16, preferred_element_type=F32)
            + jnp.dot(lo, b_bf16, preferred_element_type=F32))


def _dot_exact_lhs(a_bf16, b):
    hi, lo = _split(b)
    return (jnp.dot(a_bf16, hi, preferred_element_type=F32)
            + jnp.dot(a_bf16, lo, preferred_element_type=F32))


def _sigmoid(x):
    return 0.5 * jnp.tanh(0.5 * x) + 0.5


def _silu(x):
    h = 0.5 * x
    return h + h * jnp.tanh(h)


def _iota(shape, dim):
    return lax.broadcasted_iota(jnp.int32, shape, dim)


def _mod_kernel(c_ref, w_ref, b_ref, o_ref):
    c = c_ref[...]
    ch, cl = _split(_silu(c))
    wh, wl = _split(w_ref[0])
    acc = (jnp.dot(ch, wh, preferred_element_type=F32)
           + jnp.dot(cl, wh, preferred_element_type=F32)
           + jnp.dot(ch, wl, preferred_element_type=F32))
    o_ref[0] = acc + b_ref[0]


def _mod_call(c8, ada_w, ada_b):
    depth, d, d3 = ada_w.shape
    tn = 512
    return pl.pallas_call(
        _mod_kernel,
        out_shape=jax.ShapeDtypeStruct((depth, 8, d3), F32),
        grid=(depth, d3 // tn),
        in_specs=[pl.BlockSpec((8, d), lambda l, j: (0, 0)),
                  pl.BlockSpec((1, d, tn), lambda l, j: (l, 0, j)),
                  pl.BlockSpec((1, 1, tn), lambda l, j: (l, 0, j))],
        out_specs=pl.BlockSpec((1, 8, tn), lambda l, j: (l, 0, j)),
        compiler_params=_params(2), name="adaln_mod",
    )(c8, ada_w, ada_b.reshape(depth, 1, d3))


def _rope_kernel(pos_ref, freq_ref, sign_ref, o_ref):
    ang = pos_ref[...].astype(F32) * freq_ref[...]
    o_ref[:, 0:MLA_ROPE] = jnp.cos(ang)
    o_ref[:, MLA_ROPE:2 * MLA_ROPE] = jnp.sin(ang) * sign_ref[...]


def _rope_call(pos_col):
    t = pos_col.shape[0]
    tr = min(t, 1024)
    half = MLA_ROPE // 2
    inv_freq = ROPE_THETA ** (-jnp.arange(half, dtype=F32) / half)
    freq = jnp.concatenate([inv_freq, inv_freq]).reshape(1, MLA_ROPE)
    sign = jnp.concatenate([-jnp.ones((half,), F32), jnp.ones((half,), F32)]).reshape(1, MLA_ROPE)
    return pl.pallas_call(
        _rope_kernel,
        out_shape=jax.ShapeDtypeStruct((t, 2 * MLA_ROPE), F32),
        grid=(t // tr,),
        in_specs=[pl.BlockSpec((tr, 1), lambda i: (i, 0)),
                  pl.BlockSpec((1, MLA_ROPE), lambda i: (0, 0)),
                  pl.BlockSpec((1, MLA_ROPE), lambda i: (0, 0))],
        out_specs=pl.BlockSpec((tr, 2 * MLA_ROPE), lambda i: (i, 0)),
        compiler_params=_params(1), name="rope_table",
    )(pos_col, freq, sign)


def _inproj_kernel(x_ref, scale_ref, shift_ref, nw_ref, w_ref, o_ref, h_scr):
    @pl.when(pl.program_id(1) == 0)
    def _():
        x = x_ref[...]
        ms = jnp.mean(x * x, axis=-1, keepdims=True)
        y = x * lax.rsqrt(ms + NORM_EPS) * nw_ref[...]
        h = y * (1.0 + scale_ref[0]) + shift_ref[0]
        h_scr[...] = h.astype(BF16)

    o_ref[...] = jnp.dot(h_scr[...], w_ref[0], preferred_element_type=F32).astype(o_ref.dtype)


def _inproj_call(x2, scale, shift, norm_w, wp_all, layer, seq):
    t, d = x2.shape
    tm = min(seq, 1024)
    per_batch = seq // tm
    return pl.pallas_call(
        _inproj_kernel,
        out_shape=jax.ShapeDtypeStruct((t, PROJ_COLS), BF16),
        grid=(t // tm, PROJ_COLS // INPROJ_BLOCK),
        in_specs=[pl.BlockSpec((tm, d), lambda i, j: (i, 0)),
                  pl.BlockSpec((1, 1, d), lambda i, j: (i // per_batch, 0, 0)),
                  pl.BlockSpec((1, 1, d), lambda i, j: (i // per_batch, 0, 0)),
                  pl.BlockSpec((1, d), lambda i, j: (0, 0)),
                  pl.BlockSpec((1, d, INPROJ_BLOCK), lambda i, j: (layer, 0, j))],
        out_specs=pl.BlockSpec((tm, INPROJ_BLOCK), lambda i, j: (i, j)),
        scratch_shapes=[pltpu.VMEM((tm, d), BF16)],
        compiler_params=_params(2), name="in_proj",
    )(x2, scale, shift, norm_w.reshape(1, d), wp_all)


def _pack_w_in(w):
    w = w.astype(BF16)
    o = np.cumsum((0, MLA_Q_RANK, MLA_KV_RANK, MLA_ROPE, MLA_WIDTH, SSD_WIDTH, SSD_XBC, SSD_HEADS,
                   RWKV_WIDTH, RWKV_WIDTH, RWKV_WIDTH, DECAY_LORA, ICLR_LORA, RWKV_WIDTH))
    q_lat, kv_lat, k_pe, g_mla, z, xbc, dt, r, k, v, w_lo, a_lo, g_rwkv = (
        w[..., o[i]:o[i + 1]] for i in range(13))
    half = MLA_ROPE // 2
    misc_pad = jnp.zeros(w.shape[:-1] + (PROJ_BLOCK - DECAY_LORA - ICLR_LORA - SSD_HEADS,), BF16)
    return jnp.concatenate([xbc, g_mla, z, r, k, v, g_rwkv, q_lat, kv_lat, k_pe, k_pe[..., half:],
                            k_pe[..., :half], w_lo, a_lo, dt, misc_pad], axis=-1)


def _rms(x, w):
    ms = jnp.mean(x * x, axis=-1, keepdims=True)
    return x * lax.rsqrt(ms + NORM_EPS) * w


def _mla_prep_kernel(lat_ref, rope_ref, qnw_ref, wuq_ref, kvnw_ref, wukv_ref, q_ref, k_ref, v_ref):
    lat = lat_ref[...].astype(F32)
    cs = rope_ref[...]
    q = _dot(_rms(lat[:, 0:MLA_Q_RANK], qnw_ref[...]), wuq_ref[...])
    kv = _dot(_rms(lat[:, MLA_Q_RANK:MLA_Q_RANK + MLA_KV_RANK], kvnw_ref[...]), wukv_ref[...])
    kp = lat[:, MLA_Q_RANK + MLA_KV_RANK:] * cs
    k_pe = (kp[:, 0:MLA_ROPE] + kp[:, MLA_ROPE:]).astype(BF16)
    rows = lat.shape[0]
    zpad = jnp.zeros((rows, MLA_QK_PAD - MLA_NOPE - MLA_ROPE), BF16)
    scale = (MLA_NOPE + MLA_ROPE) ** -0.5 * LOG2_E
    ones_row = (_iota((MLA_VT_ROWS - MLA_V, rows), 0) == 0).astype(BF16)
    for h in range(MLA_HEADS):
        qh = q[:, 256 * h:256 * (h + 1)]
        qp = qh[:, MLA_NOPE:] * cs
        q_pe = qp[:, 0:MLA_ROPE] + qp[:, MLA_ROPE:]
        q_ref[0, h] = jnp.concatenate([(qh[:, 0:MLA_NOPE] * scale).astype(BF16),
                                       (q_pe * scale).astype(BF16), zpad], axis=1)
        k_ref[0, h] = jnp.concatenate([kv[:, 256 * h:256 * h + MLA_NOPE].astype(BF16), k_pe, zpad],
                                      axis=1)
        v_ref[0, h, 0:MLA_V, :] = kv[:, 256 * h + MLA_NOPE:256 * (h + 1)].T.astype(BF16)
        v_ref[0, h, MLA_V:, :] = ones_row


def _pack_w_uq(w_uq):
    half = MLA_ROPE // 2
    per = MLA_NOPE + MLA_ROPE
    cols = []
    for h in range(MLA_HEADS):
        wh = w_uq[:, per * h:per * (h + 1)]
        pe = wh[:, MLA_NOPE:]
        cols += [wh, pe[:, half:], pe[:, :half]]
    return jnp.concatenate(cols, axis=1).astype(BF16)


def _mla_prep_call(proj, rope_tab, q_norm_w, w_uq, kv_norm_w, w_ukv, batch, seq):
    tm = min(seq, 1024)
    per_batch = seq // tm
    lat_blk = COL_LAT // PROJ_BLOCK
    qk_shape = jax.ShapeDtypeStruct((batch, MLA_HEADS, seq, MLA_QK_PAD), BF16)
    qk_spec = pl.BlockSpec((1, MLA_HEADS, tm, MLA_QK_PAD), lambda i: (i // per_batch, 0, i % per_batch, 0))
    return pl.pallas_call(
        _mla_prep_kernel,
        out_shape=(qk_shape, qk_shape, jax.ShapeDtypeStruct((batch, MLA_HEADS, MLA_VT_ROWS, seq), BF16)),
        grid=(batch * per_batch,),
        in_specs=[pl.BlockSpec((tm, PROJ_BLOCK), lambda i: (i, lat_blk)),
                  pl.BlockSpec((tm, 2 * MLA_ROPE), lambda i: (i, 0)),
                  pl.BlockSpec((1, MLA_Q_RANK), lambda i: (0, 0)),
                  pl.BlockSpec((MLA_Q_RANK, 256 * MLA_HEADS), lambda i: (0, 0)),
                  pl.BlockSpec((1, MLA_KV_RANK), lambda i: (0, 0)),
                  pl.BlockSpec((MLA_KV_RANK, 256 * MLA_HEADS), lambda i: (0, 0))],
        out_specs=(qk_spec, qk_spec,
                   pl.BlockSpec((1, MLA_HEADS, MLA_VT_ROWS, tm), lambda i: (i // per_batch, 0, 0, i % per_batch))),
        compiler_params=_params(1), name="mla_prep",
    )(proj, rope_tab, q_norm_w.reshape(1, -1), _pack_w_uq(w_uq), kv_norm_w.reshape(1, -1),
      w_ukv.astype(BF16))


def _attn_kernel(qi_tab, ki_tab, q_ref, k_ref, vt_ref, g_ref, o_ref, m_scr, acc_scr):
    p = pl.program_id(1)
    qi = qi_tab[p]
    ki = ki_tab[p]
    tq = q_ref.shape[2]
    sub = min(ATTN_Q_SUB, tq)

    @pl.when(ki == 0)
    def _():
        m_scr[...] = jnp.full(m_scr.shape, -jnp.inf, F32)
        acc_scr[...] = jnp.zeros(acc_scr.shape, F32)

    def step(diagonal):
        def head_body(hi, carry):
            chains = [(hi * ATTN_HEADS_PER_ITER + hh, j) for hh in range(ATTN_HEADS_PER_ITER)
                      for j in range(tq // sub)]
            ids = range(len(chains))
            head = [h for h, _ in chains]
            cols = [slice(sub * j, sub * (j + 1)) for _, j in chains]
            n_keys = [sub * (j + 1) if diagonal else tq for _, j in chains]
            s = [lax.dot_general(k_ref[0, head[i], 0:n_keys[i], :], q_ref[0, head[i], cols[i], :],
                                 (((1,), (1,)), ((), ())), preferred_element_type=F32) for i in ids]
            if diagonal:
                for i in ids:
                    key = lax.shift_right_logical(_iota(s[i].shape, 0), CHUNK_SHIFT)
                    qry = lax.shift_right_logical(_iota(s[i].shape, 1) + sub * chains[i][1], CHUNK_SHIFT)
                    s[i] = jnp.where(key <= qry, s[i], -jnp.inf)
            m_prev = [m_scr[head[i], 0:1, cols[i]] for i in ids]
            m_new = [jnp.maximum(m_prev[i], jnp.max(s[i], axis=0, keepdims=True)) for i in ids]
            pm = [jnp.exp2(s[i] - m_new[i]).astype(BF16) for i in ids]
            pv = [jnp.dot(vt_ref[0, head[i], :, 0:n_keys[i]], pm[i], preferred_element_type=F32)
                  for i in ids]
            for i in ids:
                acc_scr[head[i], :, cols[i]] = (jnp.exp2(m_prev[i] - m_new[i]) * acc_scr[head[i], :, cols[i]]
                                                + pv[i])
                m_scr[head[i], 0:1, cols[i]] = m_new[i]
            return carry

        lax.fori_loop(0, MLA_HEADS // ATTN_HEADS_PER_ITER, head_body, 0)

    @pl.when(ki < qi)
    def _():
        step(False)

    @pl.when(ki == qi)
    def _():
        step(True)
        g = g_ref[...].astype(F32)
        for h in range(MLA_HEADS):
            acc = acc_scr[h]
            o = (acc[0:MLA_V, :] / acc[MLA_V:MLA_V + 1, :]).T
            o_ref[:, MLA_V * h:MLA_V * (h + 1)] = (o * _silu(g[:, MLA_V * h:MLA_V * (h + 1)])).astype(o_ref.dtype)


def _attn_call(q, k, vt, proj, batch, seq):
    tq = min(seq, 1024)
    nq = seq // tq
    pairs = [(a, b) for a in range(nq) for b in range(a + 1)]
    qi_tab = jnp.asarray([a for a, _ in pairs], jnp.int32)
    ki_tab = jnp.asarray([b for _, b in pairs], jnp.int32)
    grid_spec = pltpu.PrefetchScalarGridSpec(
        num_scalar_prefetch=2,
        grid=(batch, len(pairs)),
        in_specs=[pl.BlockSpec((1, MLA_HEADS, tq, MLA_QK_PAD), lambda b, p, qt, kt: (b, 0, qt[p], 0)),
                  pl.BlockSpec((1, MLA_HEADS, tq, MLA_QK_PAD), lambda b, p, qt, kt: (b, 0, kt[p], 0)),
                  pl.BlockSpec((1, MLA_HEADS, MLA_VT_ROWS, tq), lambda b, p, qt, kt: (b, 0, 0, kt[p])),
                  pl.BlockSpec((tq, MLA_WIDTH), lambda b, p, qt, kt: (b * nq + qt[p], COL_GMLA // MLA_WIDTH))],
        out_specs=pl.BlockSpec((tq, MLA_WIDTH), lambda b, p, qt, kt: (b * nq + qt[p], 0)),
        scratch_shapes=[pltpu.VMEM((MLA_HEADS, 8, tq), F32),
                        pltpu.VMEM((MLA_HEADS, MLA_VT_ROWS, tq), F32)],
    )
    return pl.pallas_call(
        _attn_kernel,
        out_shape=jax.ShapeDtypeStruct((batch * seq, MLA_WIDTH), BF16),
        grid_spec=grid_spec,
        compiler_params=_params(2), name="mla_attn",
    )(qi_tab, ki_tab, q, k, vt, proj)


def _tri_incl_bf16():
    return (_iota((CHUNK, CHUNK), 0) >= _iota((CHUNK, CHUNK), 1)).astype(BF16)


def _ssd_kernel(xbc_ref, z_ref, misc_ref, convw_ref, convb_ref, dtb_ref, alog_ref, dskip_ref, nw_ref,
                expand_ref, o_ref, xpad_scr, state_scr, xc_scr, xdt_scr, dta_scr, y_scr, inc_scr, dec_scr,
                pre_scr):
    ts = xbc_ref.shape[0]
    halo = 8

    @pl.when(pl.program_id(1) == 0)
    def _():
        xpad_scr[0:halo, :] = jnp.zeros((halo, SSD_XBC), F32)
        state_scr[...] = jnp.zeros(state_scr.shape, F32)

    xpad_scr[halo:halo + ts, :] = xbc_ref[...].astype(F32)
    conv = convb_ref[...]
    for i in range(SSD_CONV):
        start = halo - (SSD_CONV - 1) + i
        conv = conv + convw_ref[i:i + 1, :] * xpad_scr[start:start + ts, :]
    xpad_scr[0:halo, :] = xpad_scr[ts:ts + halo, :]
    xc_scr[...] = _silu(conv)

    dt = jax.nn.softplus(misc_ref[:, LANES:2 * LANES].astype(F32) + dtb_ref[...])
    dt_x = _dot_exact_rhs(dt, expand_ref[...])
    xdt_scr[...] = xc_scr[:, 0:SSD_WIDTH] * dt_x
    dta_scr[...] = dt_x * (-jnp.exp(alog_ref[...]))

    tri = _tri_incl_bf16()
    lane = _iota((CHUNK, PAIR), 1)
    row = _iota((CHUNK, PAIR), 0)
    in_first = lane < SSD_HEAD_DIM
    src = jnp.where(in_first, lane, lane - SSD_HEAD_DIM)
    diag2 = row == src
    causal2 = row >= src
    gw = SSD_WIDTH // SSD_GROUPS

    n_chunks = ts // CHUNK
    group = SSD_GROUP_CHUNKS if n_chunks % SSD_GROUP_CHUNKS == 0 else 1
    b_cols = [slice(SSD_WIDTH + SSD_STATE * g, SSD_WIDTH + SSD_STATE * (g + 1)) for g in range(SSD_GROUPS)]
    c_cols = [slice(SSD_WIDTH + SSD_STATE * (SSD_GROUPS + g), SSD_WIDTH + SSD_STATE * (SSD_GROUPS + g + 1))
              for g in range(SSD_GROUPS)]
    g_lanes = [slice(gw * g, gw * (g + 1)) for g in range(SSD_GROUPS)]

    def group_body(gi, carry):
        cs = range(group)
        rows = [pl.ds(pl.multiple_of((gi * group + ci) * CHUNK, CHUNK), CHUNK) for ci in cs]
        cum = [_dot_exact_lhs(tri, dta_scr[rows[ci], :]) for ci in cs]
        xdt = [xdt_scr[rows[ci], :] for ci in cs]
        bm = [[xc_scr[rows[ci], b_cols[g]] for g in range(SSD_GROUPS)] for ci in cs]
        cm = [[xc_scr[rows[ci], c_cols[g]] for g in range(SSD_GROUPS)] for ci in cs]
        tot = [cum[ci][CHUNK - 1:CHUNK, :] for ci in cs]
        xw = [xdt[ci] * jnp.exp(tot[ci] - cum[ci]) for ci in cs]
        cb2 = [[_dot_nt(cm[ci][g], jnp.concatenate([bm[ci][g], bm[ci][g]], axis=0))
                for g in range(SSD_GROUPS)] for ci in cs]
        inc = [[_dot_tn(bm[ci][g], xw[ci][:, g_lanes[g]]) for g in range(SSD_GROUPS)] for ci in cs]
        lhs, rhs = [], []
        for ci in cs:
            for p in range(N_PAIRS):
                pl_ = slice(PAIR * p, PAIR * (p + 1))
                colp = cum[ci][:, pl_]
                rowp = jnp.sum(jnp.where(diag2, colp, 0.0), axis=0, keepdims=True)
                dec = jnp.exp(jnp.where(causal2, colp - rowp, -jnp.inf))
                lhs.append(cb2[ci][p // 2] * dec)
                rhs.append(_stack_heads(xdt[ci][:, pl_], in_first))
        y_diag = [_dot(lhs[i], rhs[i]) for i in range(len(lhs))]
        for ci in cs:
            c = gi * group + ci
            for p in range(N_PAIRS):
                y_scr[rows[ci], PAIR * p:PAIR * (p + 1)] = y_diag[ci * N_PAIRS + p]
            for g in range(SSD_GROUPS):
                inc_scr[c, :, g_lanes[g]] = inc[ci][g]
            dec_scr[c] = jnp.broadcast_to(jnp.exp(tot[ci]), (8, SSD_WIDTH))
            dta_scr[rows[ci], :] = jnp.exp(cum[ci])
        return carry

    lax.fori_loop(0, n_chunks // group, group_body, 0)

    state = state_scr[...]
    for c in range(n_chunks):
        pre_scr[c] = state.astype(BF16)
        state = state * dec_scr[c, 0:1, :] + inc_scr[c]
    state_scr[...] = state

    def off_body(gi, carry):
        cs = range(group)
        rows = [pl.ds(pl.multiple_of((gi * group + ci) * CHUNK, CHUNK), CHUNK) for ci in cs]
        y_off = [[jnp.dot(xc_scr[rows[ci], c_cols[g]].astype(BF16), pre_scr[gi * group + ci, :, g_lanes[g]],
                          preferred_element_type=F32) for g in range(SSD_GROUPS)] for ci in cs]
        for ci in cs:
            for g in range(SSD_GROUPS):
                y_scr[rows[ci], g_lanes[g]] = (y_scr[rows[ci], g_lanes[g]]
                                               + y_off[ci][g] * dta_scr[rows[ci], g_lanes[g]])
        return carry

    lax.fori_loop(0, n_chunks // group, off_body, 0)

    y = y_scr[...] + dskip_ref[...] * xc_scr[:, 0:SSD_WIDTH]
    o_ref[...] = _rms(y * _silu(z_ref[...].astype(F32)), nw_ref[...]).astype(o_ref.dtype)


def _head_expand(vec, width):
    return jnp.repeat(vec.astype(F32), width).reshape(1, -1)


def _ssd_call(proj, conv_w, conv_b, dt_bias, a_log, d_skip, ssd_norm_w, batch, seq):
    ts = min(seq, 512)
    nt = seq // ts
    dtb = jnp.zeros((1, LANES), F32).at[0, :SSD_HEADS].set(dt_bias)
    expand = (jnp.arange(LANES)[:, None] == (jnp.arange(SSD_WIDTH) // SSD_HEAD_DIM)[None, :]).astype(BF16)
    row = lambda w: pl.BlockSpec((1, w), lambda b, i: (0, 0))
    return pl.pallas_call(
        _ssd_kernel,
        out_shape=jax.ShapeDtypeStruct((batch * seq, SSD_WIDTH), BF16),
        grid=(batch, nt),
        in_specs=[pl.BlockSpec((ts, SSD_XBC), lambda b, i: (b * nt + i, COL_XBC // SSD_XBC)),
                  pl.BlockSpec((ts, PROJ_BLOCK), lambda b, i: (b * nt + i, COL_Z // PROJ_BLOCK)),
                  pl.BlockSpec((ts, PROJ_BLOCK), lambda b, i: (b * nt + i, COL_MISC // PROJ_BLOCK)),
                  pl.BlockSpec((SSD_CONV, SSD_XBC), lambda b, i: (0, 0)),
                  row(SSD_XBC), row(LANES), row(SSD_WIDTH), row(SSD_WIDTH), row(SSD_WIDTH),
                  pl.BlockSpec((LANES, SSD_WIDTH), lambda b, i: (0, 0))],
        out_specs=pl.BlockSpec((ts, SSD_WIDTH), lambda b, i: (b * nt + i, 0)),
        scratch_shapes=[pltpu.VMEM((ts + 8, SSD_XBC), F32),
                        pltpu.VMEM((SSD_STATE, SSD_WIDTH), F32),
                        pltpu.VMEM((ts, SSD_XBC), F32),
                        pltpu.VMEM((ts, SSD_WIDTH), F32),
                        pltpu.VMEM((ts, SSD_WIDTH), F32),
                        pltpu.VMEM((ts, SSD_WIDTH), F32),
                        pltpu.VMEM((ts // CHUNK, SSD_STATE, SSD_WIDTH), F32),
                        pltpu.VMEM((ts // CHUNK, 8, SSD_WIDTH), F32),
                        pltpu.VMEM((ts // CHUNK, SSD_STATE, SSD_WIDTH), BF16)],
        compiler_params=_params(2), name="ssd",
    )(proj, proj, proj, conv_w, conv_b.reshape(1, -1), dtb, _head_expand(a_log, SSD_HEAD_DIM),
      _head_expand(d_skip, SSD_HEAD_DIM), ssd_norm_w.reshape(1, -1), expand)


def _pair_sum(x, in_first):
    s0 = jnp.sum(jnp.where(in_first, x, 0.0), axis=-1, keepdims=True)
    s1 = jnp.sum(jnp.where(in_first, 0.0, x), axis=-1, keepdims=True)
    return jnp.where(in_first, s0, s1)


def _stack_heads(x, in_first):
    return jnp.concatenate([jnp.where(in_first, x, 0.0), jnp.where(in_first, 0.0, x)], axis=0)


def _rwkv_kernel(r_ref, k_ref, v_ref, misc_ref, g_ref, mur_ref, muk_ref, muv_ref, muwa_ref, w0_ref,
                 wlb_ref, a0_ref, alb_ref, kk_ref, ka_ref, rk_ref, lnw_ref, lnb_ref, o_ref,
                 prev_scr, state_scr, r_scr, lw_scr, k_scr, v_scr, a_scr, b_scr, y_scr, rg_scr, yh_scr):
    ts = r_ref.shape[0]

    @pl.when(pl.program_id(1) == 0)
    def _():
        prev_scr[...] = jnp.zeros(prev_scr.shape, F32)
        state_scr[...] = jnp.zeros(state_scr.shape, F32)

    first_row = _iota((8, 1), 0) == 0

    def shift(x, mu, slot):
        w = x.shape[1]
        rolled = pltpu.roll(x, 1, 0)
        head = jnp.where(first_row, prev_scr[slot:slot + 1, 0:w], rolled[0:8, :])
        prev = jnp.concatenate([head, rolled[8:, :]], axis=0)
        prev_scr[slot:slot + 1, 0:w] = x[ts - 1:ts, :]
        return x + (prev - x) * mu

    r = shift(r_ref[...].astype(F32), mur_ref[...], 0)
    k = shift(k_ref[...].astype(F32), muk_ref[...], 1)
    v = shift(v_ref[...].astype(F32), muv_ref[...], 2)
    wa = shift(misc_ref[:, 0:LANES].astype(F32), muwa_ref[...], 3)

    lw_scr[...] = -DECAY_SCALE * _sigmoid(w0_ref[...] + _dot(jnp.tanh(wa), wlb_ref[...]))
    a = _sigmoid(a0_ref[...] + _dot(wa, alb_ref[...]))

    in_first_t = _iota((ts, PAIR), 1) < RWKV_HEAD_DIM
    kk = k * kk_ref[...]
    k2 = k * (1.0 + (a - 1.0) * ka_ref[...])
    rkr = r * k2 * rk_ref[...]
    for p in range(N_PAIRS):
        sl = slice(PAIR * p, PAIR * (p + 1))
        kkp = kk[:, sl]
        kkn = kkp * lax.rsqrt(jnp.maximum(_pair_sum(kkp * kkp, in_first_t), 1e-24))
        a_scr[:, sl] = -kkn
        b_scr[:, sl] = kkn * a[:, sl]
        y_scr[:, sl] = _pair_sum(rkr[:, sl], in_first_t)
    bonus = y_scr[...] * v
    r_scr[...] = r
    k_scr[...] = k2
    v_scr[...] = v

    tri = _tri_incl_bf16()
    in_first = _iota((CHUNK, PAIR), 1) < RWKV_HEAD_DIM
    rr = _iota((PAIR, PAIR), 0)
    cc = _iota((PAIR, PAIR), 1)
    same = lax.shift_right_logical(rr, CHUNK_SHIFT) == lax.shift_right_logical(cc, CHUNK_SHIFT)
    strict = same & (rr > cc)
    incl = same & (rr >= cc)
    eye_b = rr == cc
    eye = eye_b.astype(F32)
    zeros_pp = jnp.zeros((PAIR, PAIR), BF16)
    n_chunks = ts // CHUNK
    group = RWKV_GROUP_CHUNKS if n_chunks % RWKV_GROUP_CHUNKS == 0 else 1

    def group_body(gi, carry):
        atm, rtm, vm, bkh, bke, wend = [], [], [], [], [], []
        for ci in range(group):
            r0 = pl.multiple_of((gi * group + ci) * CHUNK, CHUNK)
            rows = pl.ds(r0, CHUNK)
            lw = lw_scr[rows, :]
            cum = _dot_exact_lhs(tri, lw)
            tot = cum[CHUNK - 1:CHUNK, :]
            e_neg = jnp.exp(-cum)
            e_end = jnp.exp(tot - cum)
            w_end = jnp.exp(tot)
            rc, kc, vc, ac, bc = (r_scr[rows, :], k_scr[rows, :], v_scr[rows, :], a_scr[rows, :],
                                  b_scr[rows, :])
            a_t = ac * jnp.exp(cum - lw)
            r_t = rc * jnp.exp(cum)
            b_h = bc * e_neg
            k_h = kc * e_neg
            b_e = bc * e_end
            k_e = kc * e_end
            for p in range(N_PAIRS):
                sl = slice(PAIR * p, PAIR * (p + 1))
                st = lambda x: _stack_heads(x[:, sl], in_first)
                atm.append(st(a_t))
                rtm.append(st(r_t))
                vm.append(st(vc).astype(BF16))
                bkh.append(jnp.concatenate([st(b_h), st(k_h)], axis=0).astype(BF16))
                bke.append(jnp.concatenate([st(b_e), st(k_e)], axis=0).astype(BF16))
                wend.append(w_end[:, sl])
        ids = range(len(atm))
        amat = [_dot_nt(jnp.concatenate([atm[i], rtm[i]], axis=0), bkh[i]) for i in ids]
        n_ab = [jnp.where(strict, amat[i][0:PAIR, 0:PAIR], 0.0) for i in ids]
        a_ak = [jnp.where(strict, amat[i][0:PAIR, PAIR:], 0.0) for i in ids]
        a_r = [jnp.concatenate([jnp.where(incl, amat[i][PAIR:, 0:PAIR], 0.0),
                                jnp.where(incl, amat[i][PAIR:, PAIR:], 0.0)], axis=1).astype(BF16)
               for i in ids]
        akv = [_dot(a_ak[i], vm[i]) for i in ids]
        npow = [_dot(n_ab[i], n_ab[i]) for i in ids]
        tinv = [eye + n_ab[i] for i in ids]
        for _ in range(CHUNK_SHIFT - 2):
            both = [_dot(npow[i], jnp.concatenate([npow[i], tinv[i]], axis=1)) for i in ids]
            npow = [both[i][:, 0:PAIR] for i in ids]
            tinv = [tinv[i] + both[i][:, PAIR:] for i in ids]
        last = [_dot(npow[i], tinv[i]) for i in ids]
        tinv = [tinv[i] + last[i] for i in ids]
        au = [_dot(tinv[i], jnp.concatenate([atm[i], akv[i]], axis=1)) for i in ids]
        rhs = [jnp.concatenate([au[i].astype(BF16), jnp.concatenate([zeros_pp, vm[i]], axis=1)], axis=0)
               for i in ids]
        ry = [jnp.dot(a_r[i], rhs[i], preferred_element_type=F32) for i in ids]
        gh = [_dot_tn(bke[i], rhs[i]) for i in ids]
        for i in ids:
            c = gi * group + i // N_PAIRS
            p = i % N_PAIRS
            gmat = gh[i][:, 0:PAIR] + jnp.where(eye_b, wend[i], 0.0)
            rg_scr[c, p] = jnp.concatenate([rtm[i] + ry[i][:, 0:PAIR], gmat], axis=0).astype(BF16)
            yh_scr[c, p] = jnp.concatenate([ry[i][:, PAIR:], gh[i][:, PAIR:]], axis=0)
        return carry

    lax.fori_loop(0, n_chunks // group, group_body, 0)

    def sweep_body(c, carry):
        r0 = c * CHUNK
        s2 = []
        for p in range(N_PAIRS):
            s_hi, s_lo = _split(state_scr[p])
            s2.append(jnp.concatenate([s_hi, s_lo], axis=1))
        out = [jnp.dot(rg_scr[c, p], s2[p], preferred_element_type=F32) for p in range(N_PAIRS)]
        for p in range(N_PAIRS):
            tot = yh_scr[c, p] + out[p][:, 0:PAIR] + out[p][:, PAIR:]
            state_scr[p] = tot[PAIR:, :]
            y_scr[pl.ds(r0, CHUNK), PAIR * p:PAIR * (p + 1)] = tot[0:CHUNK, :] + tot[CHUNK:PAIR, :]
        return carry

    for c in range(n_chunks):
        sweep_body(c, 0)

    g = g_ref[...].astype(F32)
    for p in range(N_PAIRS):
        sl = slice(PAIR * p, PAIR * (p + 1))
        y = y_scr[:, sl]
        mean = _pair_sum(y, in_first_t) * (1.0 / RWKV_HEAD_DIM)
        d = y - mean
        var = _pair_sum(d * d, in_first_t) * (1.0 / RWKV_HEAD_DIM)
        yn = d * lax.rsqrt(var + GN_EPS) * lnw_ref[:, sl] + lnb_ref[:, sl]
        o_ref[:, sl] = ((yn + bonus[:, sl]) * _silu(g[:, sl])).astype(o_ref.dtype)


def _rwkv_call(proj, mu_rkv, mu_w, mu_a, w0, w_lora_b, a0, a_lora_b, k_k, k_a, r_k, lnx_w, lnx_b,
               batch, seq):
    ts = min(seq, 512)
    nt = seq // ts
    w = RWKV_WIDTH
    muwa = jnp.concatenate([mu_w, mu_a]).reshape(1, LANES)
    zl = jnp.zeros((DECAY_LORA, w), F32)
    wlb = jnp.concatenate([w_lora_b, zl], axis=0).astype(BF16)
    alb = jnp.concatenate([zl, a_lora_b], axis=0).astype(BF16)
    blk = lambda col: pl.BlockSpec((ts, PROJ_BLOCK), lambda b, i: (b * nt + i, col // PROJ_BLOCK))
    row = lambda width: pl.BlockSpec((1, width), lambda b, i: (0, 0))
    lora = pl.BlockSpec((LANES, w), lambda b, i: (0, 0))
    tile = pltpu.VMEM((ts, w), F32)
    return pl.pallas_call(
        _rwkv_kernel,
        out_shape=jax.ShapeDtypeStruct((batch * seq, w), BF16),
        grid=(batch, nt),
        in_specs=[blk(COL_R), blk(COL_K), blk(COL_V), blk(COL_MISC), blk(COL_GRWKV),
                  row(w), row(w), row(w), row(LANES), row(w), lora, row(w), lora,
                  row(w), row(w), row(w), row(w), row(w)],
        out_specs=pl.BlockSpec((ts, w), lambda b, i: (b * nt + i, 0)),
        scratch_shapes=[pltpu.VMEM((8, w), F32),
                        pltpu.VMEM((N_PAIRS, PAIR, PAIR), F32),
                        tile, tile, tile, tile, tile, tile, tile,
                        pltpu.VMEM((ts // CHUNK, N_PAIRS, 2 * PAIR, PAIR), BF16),
                        pltpu.VMEM((ts // CHUNK, N_PAIRS, 2 * PAIR, PAIR), F32)],
        compiler_params=_params(2), name="rwkv7",
    )(proj, proj, proj, proj, proj,
      mu_rkv[0:1], mu_rkv[1:2], mu_rkv[2:3], muwa, w0.reshape(1, w), wlb, a0.reshape(1, w), alb,
      k_k.reshape(1, w), k_a.reshape(1, w), r_k.reshape(1, w), lnx_w.reshape(1, w), lnx_b.reshape(1, w))


def _outproj_kernel(ym_ref, ys_ref, yr_ref, x_ref, gate_ref, w_ref, fnw_ref, o_ref, *, final):
    acc = (_dot(ym_ref[...], w_ref[0:MLA_WIDTH, :])
           + _dot(ys_ref[...], w_ref[MLA_WIDTH:MLA_WIDTH + SSD_WIDTH, :])
           + _dot(yr_ref[...], w_ref[MLA_WIDTH + SSD_WIDTH:, :]))
    xn = x_ref[...] + gate_ref[0] * acc
    o_ref[...] = _rms(xn, fnw_ref[...]) if final else xn


def _outproj_call(y_mla, y_ssd, y_rwkv, x2, gate, w_out, final_norm_w, seq, final):
    t, d = x2.shape
    tm = min(seq, 1024)
    per_batch = seq // tm
    yspec = lambda wd: pl.BlockSpec((tm, wd), lambda i: (i, 0))
    return pl.pallas_call(
        functools.partial(_outproj_kernel, final=final),
        out_shape=jax.ShapeDtypeStruct((t, d), F32),
        grid=(t // tm,),
        in_specs=[yspec(MLA_WIDTH), yspec(SSD_WIDTH), yspec(RWKV_WIDTH), yspec(d),
                  pl.BlockSpec((1, 1, d), lambda i: (i // per_batch, 0, 0)),
                  pl.BlockSpec((D_MIX, d), lambda i: (0, 0)),
                  pl.BlockSpec((1, d), lambda i: (0, 0))],
        out_specs=yspec(d),
        compiler_params=_params(1), name="out_proj",
    )(y_mla, y_ssd, y_rwkv, x2, gate, w_out.astype(BF16), final_norm_w.reshape(1, d))


def kernel(x, c, positions, ada_w, ada_b, norm_w, w_in, q_norm_w, w_uq, kv_norm_w, w_ukv, conv_w, conv_b, dt_bias, a_log, d_skip, ssd_norm_w, mu_rkv, mu_w, mu_a, w0, w_lora_b, a0, a_lora_b, k_k, k_a, r_k, lnx_w, lnx_b, w_out, final_norm_w):
    batch, seq, d = x.shape
    depth = ada_w.shape[0]
    assert d == D_MODEL and batch <= 8 and seq % CHUNK == 0
    t = batch * seq
    x2 = x.reshape(t, d)
    mod = _mod_call(jnp.pad(c, ((0, 8 - batch), (0, 0))), ada_w, ada_b)
    rope_tab = _rope_call(positions.reshape(t, 1))
    wp_all = _pack_w_in(w_in)
    for l in range(depth):
        shift = mod[l, :batch, 0:d].reshape(batch, 1, d)
        scale = mod[l, :batch, d:2 * d].reshape(batch, 1, d)
        gate = mod[l, :batch, 2 * d:].reshape(batch, 1, d)
        proj = _inproj_call(x2, scale, shift, norm_w[l], wp_all, l, seq)
        q, k, v = _mla_prep_call(proj, rope_tab, q_norm_w[l], w_uq[l], kv_norm_w[l], w_ukv[l], batch, seq)
        y_mla = _attn_call(q, k, v, proj, batch, seq)
        y_ssd = _ssd_call(proj, conv_w[l], conv_b[l], dt_bias[l], a_log[l], d_skip[l], ssd_norm_w[l],
                          batch, seq)
        y_rwkv = _rwkv_call(proj, mu_rkv[l], mu_w[l], mu_a[l], w0[l], w_lora_b[l], a0[l], a_lora_b[l],
                            k_k[l], k_a[l], r_k[l], lnx_w[l], lnx_b[l], batch, seq)
        x2 = _outproj_call(y_mla, y_ssd, y_rwkv, x2, gate, w_out[l], final_norm_w, seq,
                           final=(l == depth - 1))
    return x2.reshape(batch, seq, d)
```

```python
import functools

import jax
import jax.numpy as jnp
import numpy as np
from jax import lax
from jax.experimental import pallas as pl
from jax.experimental.pallas import tpu as pltpu

F32 = jnp.float32
BF16 = jnp.bfloat16

D_MODEL = 1024
CHUNK = 64
CHUNK_SHIFT = 6
NORM_EPS = 1e-6

MLA_HEADS = 4
MLA_Q_RANK = 256
MLA_KV_RANK = 128
MLA_NOPE = 128
MLA_ROPE = 64
MLA_V = 128
MLA_WIDTH = MLA_HEADS * MLA_V
MLA_QK_PAD = 256
MLA_VT_ROWS = 144
ATTN_Q_SUB = 256
ATTN_HEADS_PER_ITER = 4
LOG2_E = 1.4426950408889634
ROPE_THETA = 10000.0

SSD_HEADS = 8
SSD_HEAD_DIM = 64
SSD_WIDTH = SSD_HEADS * SSD_HEAD_DIM
SSD_GROUPS = 2
SSD_STATE = 128
SSD_CONV = 4
SSD_XBC = SSD_WIDTH + 2 * SSD_GROUPS * SSD_STATE

RWKV_HEADS = 8
RWKV_HEAD_DIM = 64
RWKV_WIDTH = RWKV_HEADS * RWKV_HEAD_DIM
DECAY_LORA = 64
ICLR_LORA = 64
DECAY_SCALE = 0.606531
GN_EPS = 64e-5

D_MIX = MLA_WIDTH + SSD_WIDTH + RWKV_WIDTH

LANES = 128
PAIR = 2 * RWKV_HEAD_DIM
N_PAIRS = RWKV_HEADS // 2
RWKV_GROUP_CHUNKS = 8
SSD_GROUP_CHUNKS = 8

COL_XBC = 0
COL_GMLA = 1024
COL_Z = 1536
COL_R = 2048
COL_K = 2560
COL_V = 3072
COL_GRWKV = 3584
COL_LAT = 4096
COL_MISC = 4608
PROJ_COLS = 5120
PROJ_BLOCK = 512

VMEM_LIMIT_BYTES = 56 * 1024 * 1024


def _params(n_axes):
    return pltpu.CompilerParams(dimension_semantics=("arbitrary",) * n_axes,
                                vmem_limit_bytes=VMEM_LIMIT_BYTES)


def _dot(a, b):
    return jnp.dot(a.astype(BF16), b.astype(BF16), preferred_element_type=F32)


def _dot_nt(a, b):
    return lax.dot_general(a.astype(BF16), b.astype(BF16), (((1,), (1,)), ((), ())),
                           preferred_element_type=F32)


def _dot_tn(a, b):
    return lax.dot_general(a.astype(BF16), b.astype(BF16), (((0,), (0,)), ((), ())),
                           preferred_element_type=F32)


def _split(x):
    hi = x.astype(BF16)
    lo = (x - hi.astype(F32)).astype(BF16)
    return hi, lo


def _dot_exact_rhs(a, b_bf16):
    hi, lo = _split(a)
    return (jnp.dot(hi, b_bf16, preferred_element_type=F32)
            + jnp.dot(lo, b_bf16, preferred_element_type=F32))


def _dot_exact_lhs(a_bf16, b):
    hi, lo = _split(b)
    return (jnp.dot(a_bf16, hi, preferred_element_type=F32)
            + jnp.dot(a_bf16, lo, preferred_element_type=F32))


def _sigmoid(x):
    return 0.5 * jnp.tanh(0.5 * x) + 0.5


def _silu(x):
    h = 0.5 * x
    return h + h * jnp.tanh(h)


def _iota(shape, dim):
    return lax.broadcasted_iota(jnp.int32, shape, dim)


def _mod_kernel(c_ref, w_ref, b_ref, o_ref):
    c = c_ref[...]
    ch, cl = _split(_silu(c))
    wh, wl = _split(w_ref[0])
    acc = (jnp.dot(ch, wh, preferred_element_type=F32)
           + jnp.dot(cl, wh, preferred_element_type=F32)
           + jnp.dot(ch, wl, preferred_element_type=F32))
    o_ref[0] = acc + b_ref[0]


def _mod_call(c8, ada_w, ada_b):
    depth, d, d3 = ada_w.shape
    tn = 512
    return pl.pallas_call(
        _mod_kernel,
        out_shape=jax.ShapeDtypeStruct((depth, 8, d3), F32),
        grid=(depth, d3 // tn),
        in_specs=[pl.BlockSpec((8, d), lambda l, j: (0, 0)),
                  pl.BlockSpec((1, d, tn), lambda l, j: (l, 0, j)),
                  pl.BlockSpec((1, 1, tn), lambda l, j: (l, 0, j))],
        out_specs=pl.BlockSpec((1, 8, tn), lambda l, j: (l, 0, j)),
        compiler_params=_params(2), name="adaln_mod",
    )(c8, ada_w, ada_b.reshape(depth, 1, d3))


def _rope_kernel(pos_ref, freq_ref, sign_ref, o_ref):
    ang = pos_ref[...].astype(F32) * freq_ref[...]
    o_ref[:, 0:MLA_ROPE] = jnp.cos(ang)
    o_ref[:, MLA_ROPE:2 * MLA_ROPE] = jnp.sin(ang) * sign_ref[...]


def _rope_call(pos_col):
    t = pos_col.shape[0]
    tr = min(t, 1024)
    half = MLA_ROPE // 2
    inv_freq = ROPE_THETA ** (-jnp.arange(half, dtype=F32) / half)
    freq = jnp.concatenate([inv_freq, inv_freq]).reshape(1, MLA_ROPE)
    sign = jnp.concatenate([-jnp.ones((half,), F32), jnp.ones((half,), F32)]).reshape(1, MLA_ROPE)
    return pl.pallas_call(
        _rope_kernel,
        out_shape=jax.ShapeDtypeStruct((t, 2 * MLA_ROPE), F32),
        grid=(t // tr,),
        in_specs=[pl.BlockSpec((tr, 1), lambda i: (i, 0)),
                  pl.BlockSpec((1, MLA_ROPE), lambda i: (0, 0)),
                  pl.BlockSpec((1, MLA_ROPE), lambda i: (0, 0))],
        out_specs=pl.BlockSpec((tr, 2 * MLA_ROPE), lambda i: (i, 0)),
        compiler_params=_params(1), name="rope_table",
    )(pos_col, freq, sign)


def _inproj_kernel(x_ref, scale_ref, shift_ref, nw_ref, w_ref, o_ref):
    x = x_ref[...]
    ms = jnp.mean(x * x, axis=-1, keepdims=True)
    y = x * lax.rsqrt(ms + NORM_EPS) * nw_ref[...]
    h = (y * (1.0 + scale_ref[0]) + shift_ref[0]).astype(BF16)
    o_ref[...] = jnp.dot(h, w_ref[0], preferred_element_type=F32).astype(o_ref.dtype)


def _inproj_call(x2, scale, shift, norm_w, wp_all, layer, seq):
    t, d = x2.shape
    tm = min(seq, 512)
    per_batch = seq // tm
    return pl.pallas_call(
        _inproj_kernel,
        out_shape=jax.ShapeDtypeStruct((t, PROJ_COLS), BF16),
        grid=(t // tm,),
        in_specs=[pl.BlockSpec((tm, d), lambda i: (i, 0)),
                  pl.BlockSpec((1, 1, d), lambda i: (i // per_batch, 0, 0)),
                  pl.BlockSpec((1, 1, d), lambda i: (i // per_batch, 0, 0)),
                  pl.BlockSpec((1, d), lambda i: (0, 0)),
                  pl.BlockSpec((1, d, PROJ_COLS), lambda i: (layer, 0, 0))],
        out_specs=pl.BlockSpec((tm, PROJ_COLS), lambda i: (i, 0)),
        compiler_params=_params(1), name="in_proj",
    )(x2, scale, shift, norm_w.reshape(1, d), wp_all)


def _pack_w_in(w):
    w = w.astype(BF16)
    o = np.cumsum((0, MLA_Q_RANK, MLA_KV_RANK, MLA_ROPE, MLA_WIDTH, SSD_WIDTH, SSD_XBC, SSD_HEADS,
                   RWKV_WIDTH, RWKV_WIDTH, RWKV_WIDTH, DECAY_LORA, ICLR_LORA, RWKV_WIDTH))
    q_lat, kv_lat, k_pe, g_mla, z, xbc, dt, r, k, v, w_lo, a_lo, g_rwkv = (
        w[..., o[i]:o[i + 1]] for i in range(13))
    half = MLA_ROPE // 2
    misc_pad = jnp.zeros(w.shape[:-1] + (PROJ_BLOCK - DECAY_LORA - ICLR_LORA - SSD_HEADS,), BF16)
    return jnp.concatenate([xbc, g_mla, z, r, k, v, g_rwkv, q_lat, kv_lat, k_pe, k_pe[..., half:],
                            k_pe[..., :half], w_lo, a_lo, dt, misc_pad], axis=-1)


def _rms(x, w):
    ms = jnp.mean(x * x, axis=-1, keepdims=True)
    return x * lax.rsqrt(ms + NORM_EPS) * w


def _mla_prep_kernel(lat_ref, rope_ref, qnw_ref, wuq_ref, kvnw_ref, wuk_ref, wuvt_ref, q_ref, k_ref, v_ref):
    lat = lat_ref[...].astype(F32)
    cs = rope_ref[...]
    q = _dot(_rms(lat[:, 0:MLA_Q_RANK], qnw_ref[...]), wuq_ref[...])
    kvn = _rms(lat[:, MLA_Q_RANK:MLA_Q_RANK + MLA_KV_RANK], kvnw_ref[...]).astype(BF16)
    k_nope = jnp.dot(kvn, wuk_ref[...], preferred_element_type=F32)
    v_t = _dot_nt(wuvt_ref[...], kvn)
    kp = lat[:, MLA_Q_RANK + MLA_KV_RANK:] * cs
    k_pe = (kp[:, 0:MLA_ROPE] + kp[:, MLA_ROPE:]).astype(BF16)
    rows = lat.shape[0]
    zpad = jnp.zeros((rows, MLA_QK_PAD - MLA_NOPE - MLA_ROPE), BF16)
    scale = (MLA_NOPE + MLA_ROPE) ** -0.5 * LOG2_E
    ones_row = (_iota((MLA_VT_ROWS - MLA_V, rows), 0) == 0).astype(BF16)
    for h in range(MLA_HEADS):
        qh = q[:, 256 * h:256 * (h + 1)]
        qp = qh[:, MLA_NOPE:] * cs
        q_pe = qp[:, 0:MLA_ROPE] + qp[:, MLA_ROPE:]
        q_ref[0, h] = jnp.concatenate([(qh[:, 0:MLA_NOPE] * scale).astype(BF16),
                                       (q_pe * scale).astype(BF16), zpad], axis=1)
        k_ref[0, h] = jnp.concatenate([k_nope[:, MLA_NOPE * h:MLA_NOPE * (h + 1)].astype(BF16), k_pe, zpad],
                                      axis=1)
        v_ref[0, h, 0:MLA_V, :] = v_t[MLA_V * h:MLA_V * (h + 1), :].astype(BF16)
        v_ref[0, h, MLA_V:, :] = ones_row


def _split_w_ukv(w_ukv):
    w = w_ukv.astype(BF16).reshape(MLA_KV_RANK, MLA_HEADS, MLA_NOPE + MLA_V)
    w_k = w[:, :, :MLA_NOPE].reshape(MLA_KV_RANK, MLA_HEADS * MLA_NOPE)
    w_vt = w[:, :, MLA_NOPE:].reshape(MLA_KV_RANK, MLA_HEADS * MLA_V).T
    return w_k, w_vt


def _pack_w_uq(w_uq):
    half = MLA_ROPE // 2
    per = MLA_NOPE + MLA_ROPE
    cols = []
    for h in range(MLA_HEADS):
        wh = w_uq[:, per * h:per * (h + 1)]
        pe = wh[:, MLA_NOPE:]
        cols += [wh, pe[:, half:], pe[:, :half]]
    return jnp.concatenate(cols, axis=1).astype(BF16)


def _mla_prep_call(proj, rope_tab, q_norm_w, w_uq, kv_norm_w, w_ukv, batch, seq):
    tm = min(seq, 1024)
    per_batch = seq // tm
    lat_blk = COL_LAT // PROJ_BLOCK
    qk_shape = jax.ShapeDtypeStruct((batch, MLA_HEADS, seq, MLA_QK_PAD), BF16)
    qk_spec = pl.BlockSpec((1, MLA_HEADS, tm, MLA_QK_PAD), lambda i: (i // per_batch, 0, i % per_batch, 0))
    return pl.pallas_call(
        _mla_prep_kernel,
        out_shape=(qk_shape, qk_shape, jax.ShapeDtypeStruct((batch, MLA_HEADS, MLA_VT_ROWS, seq), BF16)),
        grid=(batch * per_batch,),
        in_specs=[pl.BlockSpec((tm, PROJ_BLOCK), lambda i: (i, lat_blk)),
                  pl.BlockSpec((tm, 2 * MLA_ROPE), lambda i: (i, 0)),
                  pl.BlockSpec((1, MLA_Q_RANK), lambda i: (0, 0)),
                  pl.BlockSpec((MLA_Q_RANK, 256 * MLA_HEADS), lambda i: (0, 0)),
                  pl.BlockSpec((1, MLA_KV_RANK), lambda i: (0, 0)),
                  pl.BlockSpec((MLA_KV_RANK, MLA_NOPE * MLA_HEADS), lambda i: (0, 0)),
                  pl.BlockSpec((MLA_V * MLA_HEADS, MLA_KV_RANK), lambda i: (0, 0))],
        out_specs=(qk_spec, qk_spec,
                   pl.BlockSpec((1, MLA_HEADS, MLA_VT_ROWS, tm), lambda i: (i // per_batch, 0, 0, i % per_batch))),
        compiler_params=_params(1), name="mla_prep",
    )(proj, rope_tab, q_norm_w.reshape(1, -1), _pack_w_uq(w_uq), kv_norm_w.reshape(1, -1),
      *_split_w_ukv(w_ukv))


def _attn_kernel(qi_tab, ki_tab, q_ref, k_ref, vt_ref, g_ref, o_ref, m_scr, acc_scr):
    p = pl.program_id(1)
    qi = qi_tab[p]
    ki = ki_tab[p]
    tq = q_ref.shape[2]
    sub = min(ATTN_Q_SUB, tq)

    @pl.when(ki == 0)
    def _():
        m_scr[...] = jnp.full(m_scr.shape, -jnp.inf, F32)
        acc_scr[...] = jnp.zeros(acc_scr.shape, F32)

    def step(diagonal):
        def head_body(hi, carry):
            chains = [(hi * ATTN_HEADS_PER_ITER + hh, j) for hh in range(ATTN_HEADS_PER_ITER)
                      for j in range(tq // sub)]
            ids = range(len(chains))
            head = [h for h, _ in chains]
            cols = [slice(sub * j, sub * (j + 1)) for _, j in chains]
            n_keys = [sub * (j + 1) if diagonal else tq for _, j in chains]
            s = [lax.dot_general(k_ref[0, head[i], 0:n_keys[i], :], q_ref[0, head[i], cols[i], :],
                                 (((1,), (1,)), ((), ())), preferred_element_type=F32) for i in ids]
            if diagonal:
                for i in ids:
                    key = lax.shift_right_logical(_iota(s[i].shape, 0), CHUNK_SHIFT)
                    qry = lax.shift_right_logical(_iota(s[i].shape, 1) + sub * chains[i][1], CHUNK_SHIFT)
                    s[i] = jnp.where(key <= qry, s[i], -jnp.inf)
            m_prev = [m_scr[head[i], 0:1, cols[i]] for i in ids]
            m_new = [jnp.maximum(m_prev[i], jnp.max(s[i], axis=0, keepdims=True)) for i in ids]
            pm = [jnp.exp2(s[i] - m_new[i]).astype(BF16) for i in ids]
            pv = [jnp.dot(vt_ref[0, head[i], :, 0:n_keys[i]], pm[i], preferred_element_type=F32)
                  for i in ids]
            for i in ids:
                acc_scr[head[i], :, cols[i]] = (jnp.exp2(m_prev[i] - m_new[i]) * acc_scr[head[i], :, cols[i]]
                                                + pv[i])
                m_scr[head[i], 0:1, cols[i]] = m_new[i]
            return carry

        lax.fori_loop(0, MLA_HEADS // ATTN_HEADS_PER_ITER, head_body, 0)

    @pl.when(ki < qi)
    def _():
        step(False)

    @pl.when(ki == qi)
    def _():
        step(True)
        g = g_ref[...].astype(F32)
        for h in range(MLA_HEADS):
            acc = acc_scr[h]
            o = (acc[0:MLA_V, :] / acc[MLA_V:MLA_V + 1, :]).T
            o_ref[:, MLA_V * h:MLA_V * (h + 1)] = (o * _silu(g[:, MLA_V * h:MLA_V * (h + 1)])).astype(o_ref.dtype)


def _attn_call(q, k, vt, proj, batch, seq):
    tq = min(seq, 1024)
    nq = seq // tq
    pairs = [(a, b) for a in range(nq) for b in range(a + 1)]
    qi_tab = jnp.asarray([a for a, _ in pairs], jnp.int32)
    ki_tab = jnp.asarray([b for _, b in pairs], jnp.int32)
    grid_spec = pltpu.PrefetchScalarGridSpec(
        num_scalar_prefetch=2,
        grid=(batch, len(pairs)),
        in_specs=[pl.BlockSpec((1, MLA_HEADS, tq, MLA_QK_PAD), lambda b, p, qt, kt: (b, 0, qt[p], 0)),
                  pl.BlockSpec((1, MLA_HEADS, tq, MLA_QK_PAD), lambda b, p, qt, kt: (b, 0, kt[p], 0)),
                  pl.BlockSpec((1, MLA_HEADS, MLA_VT_ROWS, tq), lambda b, p, qt, kt: (b, 0, 0, kt[p])),
                  pl.BlockSpec((tq, MLA_WIDTH), lambda b, p, qt, kt: (b * nq + qt[p], COL_GMLA // MLA_WIDTH))],
        out_specs=pl.BlockSpec((tq, MLA_WIDTH), lambda b, p, qt, kt: (b * nq + qt[p], 0)),
        scratch_shapes=[pltpu.VMEM((MLA_HEADS, 8, tq), F32),
                        pltpu.VMEM((MLA_HEADS, MLA_VT_ROWS, tq), F32)],
    )
    return pl.pallas_call(
        _attn_kernel,
        out_shape=jax.ShapeDtypeStruct((batch * seq, MLA_WIDTH), BF16),
        grid_spec=grid_spec,
        compiler_params=_params(2), name="mla_attn",
    )(qi_tab, ki_tab, q, k, vt, proj)


def _tri_incl_bf16():
    return (_iota((CHUNK, CHUNK), 0) >= _iota((CHUNK, CHUNK), 1)).astype(BF16)


SSD_HALO = 8
N_SSD_IN = 10
N_RWKV_IN = 18


def _ssd_init(xpad_scr, state_scr):
    xpad_scr[0:SSD_HALO, :] = jnp.zeros((SSD_HALO, SSD_XBC), F32)
    state_scr[...] = jnp.zeros(state_scr.shape, F32)


def _rwkv_init(prev_scr, state_scr):
    prev_scr[...] = jnp.zeros(prev_scr.shape, F32)
    state_scr[...] = jnp.zeros(state_scr.shape, F32)


def _ssd_kernel(*refs):
    @pl.when(pl.program_id(1) == 0)
    def _():
        _ssd_init(refs[N_SSD_IN + 1], refs[N_SSD_IN + 2])

    _ssd_body(*refs)


def _rwkv_kernel(*refs):
    @pl.when(pl.program_id(1) == 0)
    def _():
        _rwkv_init(refs[N_RWKV_IN + 1], refs[N_RWKV_IN + 2])

    _rwkv_body(*refs)


def _ssd_body(xbc_ref, z_ref, misc_ref, convw_ref, convb_ref, dtb_ref, alog_ref, dskip_ref, nw_ref,
              expand_ref, o_ref, xpad_scr, state_scr, xc_scr, xdt_scr, dta_scr, y_scr, inc_scr, dec_scr,
              pre_scr):
    ts = xbc_ref.shape[0]
    halo = SSD_HALO
    xpad_scr[halo:halo + ts, :] = xbc_ref[...].astype(F32)
    conv = convb_ref[...]
    for i in range(SSD_CONV):
        start = halo - (SSD_CONV - 1) + i
        conv = conv + convw_ref[i:i + 1, :] * xpad_scr[start:start + ts, :]
    xpad_scr[0:halo, :] = xpad_scr[ts:ts + halo, :]
    xc_scr[...] = _silu(conv)

    dt = jax.nn.softplus(misc_ref[:, LANES:2 * LANES].astype(F32) + dtb_ref[...])
    dt_x = _dot_exact_rhs(dt, expand_ref[...])
    xdt_scr[...] = xc_scr[:, 0:SSD_WIDTH] * dt_x
    dta_scr[...] = dt_x * (-jnp.exp(alog_ref[...]))

    tri = _tri_incl_bf16()
    lane = _iota((CHUNK, PAIR), 1)
    row = _iota((CHUNK, PAIR), 0)
    in_first = lane < SSD_HEAD_DIM
    src = jnp.where(in_first, lane, lane - SSD_HEAD_DIM)
    diag2 = row == src
    causal2 = row >= src
    gw = SSD_WIDTH // SSD_GROUPS

    n_chunks = ts // CHUNK
    group = SSD_GROUP_CHUNKS if n_chunks % SSD_GROUP_CHUNKS == 0 else 1
    b_cols = [slice(SSD_WIDTH + SSD_STATE * g, SSD_WIDTH + SSD_STATE * (g + 1)) for g in range(SSD_GROUPS)]
    c_cols = [slice(SSD_WIDTH + SSD_STATE * (SSD_GROUPS + g), SSD_WIDTH + SSD_STATE * (SSD_GROUPS + g + 1))
              for g in range(SSD_GROUPS)]
    g_lanes = [slice(gw * g, gw * (g + 1)) for g in range(SSD_GROUPS)]

    def group_body(gi, carry):
        cs = range(group)
        rows = [pl.ds(pl.multiple_of((gi * group + ci) * CHUNK, CHUNK), CHUNK) for ci in cs]
        cum = [_dot_exact_lhs(tri, dta_scr[rows[ci], :]) for ci in cs]
        xdt = [xdt_scr[rows[ci], :] for ci in cs]
        bm = [[xc_scr[rows[ci], b_cols[g]] for g in range(SSD_GROUPS)] for ci in cs]
        cm = [[xc_scr[rows[ci], c_cols[g]] for g in range(SSD_GROUPS)] for ci in cs]
        tot = [cum[ci][CHUNK - 1:CHUNK, :] for ci in cs]
        xw = [xdt[ci] * jnp.exp(tot[ci] - cum[ci]) for ci in cs]
        cb2 = [[_dot_nt(cm[ci][g], jnp.concatenate([bm[ci][g], bm[ci][g]], axis=0))
                for g in range(SSD_GROUPS)] for ci in cs]
        inc = [[_dot_tn(bm[ci][g], xw[ci][:, g_lanes[g]]) for g in range(SSD_GROUPS)] for ci in cs]
        lhs, rhs = [], []
        for ci in cs:
            for p in range(N_PAIRS):
                pl_ = slice(PAIR * p, PAIR * (p + 1))
                colp = cum[ci][:, pl_]
                rowp = jnp.sum(jnp.where(diag2, colp, 0.0), axis=0, keepdims=True)
                dec = jnp.exp(jnp.where(causal2, colp - rowp, -jnp.inf))
                lhs.append(cb2[ci][p // 2] * dec)
                rhs.append(_stack_heads(xdt[ci][:, pl_], in_first))
        y_diag = [_dot(lhs[i], rhs[i]) for i in range(len(lhs))]
        for ci in cs:
            c = gi * group + ci
            for p in range(N_PAIRS):
                y_scr[rows[ci], PAIR * p:PAIR * (p + 1)] = y_diag[ci * N_PAIRS + p]
            for g in range(SSD_GROUPS):
                inc_scr[c, :, g_lanes[g]] = inc[ci][g]
            dec_scr[c] = jnp.broadcast_to(jnp.exp(tot[ci]), (8, SSD_WIDTH))
            dta_scr[rows[ci], :] = jnp.exp(cum[ci])
        return carry

    lax.fori_loop(0, n_chunks // group, group_body, 0)

    state = state_scr[...]
    for c in range(n_chunks):
        pre_scr[c] = state.astype(BF16)
        state = state * dec_scr[c, 0:1, :] + inc_scr[c]
    state_scr[...] = state

    def off_body(gi, carry):
        cs = range(group)
        rows = [pl.ds(pl.multiple_of((gi * group + ci) * CHUNK, CHUNK), CHUNK) for ci in cs]
        y_off = [[jnp.dot(xc_scr[rows[ci], c_cols[g]].astype(BF16), pre_scr[gi * group + ci, :, g_lanes[g]],
                          preferred_element_type=F32) for g in range(SSD_GROUPS)] for ci in cs]
        for ci in cs:
            for g in range(SSD_GROUPS):
                y_scr[rows[ci], g_lanes[g]] = (y_scr[rows[ci], g_lanes[g]]
                                               + y_off[ci][g] * dta_scr[rows[ci], g_lanes[g]])
        return carry

    lax.fori_loop(0, n_chunks // group, off_body, 0)

    y = y_scr[...] + dskip_ref[...] * xc_scr[:, 0:SSD_WIDTH]
    o_ref[...] = _rms(y * _silu(z_ref[...].astype(F32)), nw_ref[...]).astype(o_ref.dtype)


def _head_expand(vec, width):
    return jnp.repeat(vec.astype(F32), width).reshape(1, -1)


def _ssd_call(proj, conv_w, conv_b, dt_bias, a_log, d_skip, ssd_norm_w, batch, seq):
    ts = min(seq, 512)
    nt = seq // ts
    dtb = jnp.zeros((1, LANES), F32).at[0, :SSD_HEADS].set(dt_bias)
    expand = (jnp.arange(LANES)[:, None] == (jnp.arange(SSD_WIDTH) // SSD_HEAD_DIM)[None, :]).astype(BF16)
    row = lambda w: pl.BlockSpec((1, w), lambda b, i: (0, 0))
    return pl.pallas_call(
        _ssd_kernel,
        out_shape=jax.ShapeDtypeStruct((batch * seq, SSD_WIDTH), BF16),
        grid=(batch, nt),
        in_specs=[pl.BlockSpec((ts, SSD_XBC), lambda b, i: (b * nt + i, COL_XBC // SSD_XBC)),
                  pl.BlockSpec((ts, PROJ_BLOCK), lambda b, i: (b * nt + i, COL_Z // PROJ_BLOCK)),
                  pl.BlockSpec((ts, PROJ_BLOCK), lambda b, i: (b * nt + i, COL_MISC // PROJ_BLOCK)),
                  pl.BlockSpec((SSD_CONV, SSD_XBC), lambda b, i: (0, 0)),
                  row(SSD_XBC), row(LANES), row(SSD_WIDTH), row(SSD_WIDTH), row(SSD_WIDTH),
                  pl.BlockSpec((LANES, SSD_WIDTH), lambda b, i: (0, 0))],
        out_specs=pl.BlockSpec((ts, SSD_WIDTH), lambda b, i: (b * nt + i, 0)),
        scratch_shapes=[pltpu.VMEM((ts + SSD_HALO, SSD_XBC), F32),
                        pltpu.VMEM((SSD_STATE, SSD_WIDTH), F32),
                        pltpu.VMEM((ts, SSD_XBC), F32),
                        pltpu.VMEM((ts, SSD_WIDTH), F32),
                        pltpu.VMEM((ts, SSD_WIDTH), F32),
                        pltpu.VMEM((ts, SSD_WIDTH), F32),
                        pltpu.VMEM((ts // CHUNK, SSD_STATE, SSD_WIDTH), F32),
                        pltpu.VMEM((ts // CHUNK, 8, SSD_WIDTH), F32),
                        pltpu.VMEM((ts // CHUNK, SSD_STATE, SSD_WIDTH), BF16)],
        compiler_params=_params(2), name="ssd",
    )(proj, proj, proj, conv_w, conv_b.reshape(1, -1), dtb, _head_expand(a_log, SSD_HEAD_DIM),
      _head_expand(d_skip, SSD_HEAD_DIM), ssd_norm_w.reshape(1, -1), expand)


def _pair_sum(x, in_first):
    s0 = jnp.sum(jnp.where(in_first, x, 0.0), axis=-1, keepdims=True)
    s1 = jnp.sum(jnp.where(in_first, 0.0, x), axis=-1, keepdims=True)
    return jnp.where(in_first, s0, s1)


def _stack_heads(x, in_first):
    return jnp.concatenate([jnp.where(in_first, x, 0.0), jnp.where(in_first, 0.0, x)], axis=0)


def _rwkv_body(r_ref, k_ref, v_ref, misc_ref, g_ref, mur_ref, muk_ref, muv_ref, muwa_ref, w0_ref,
               wlb_ref, a0_ref, alb_ref, kk_ref, ka_ref, rk_ref, lnw_ref, lnb_ref, o_ref,
               prev_scr, state_scr, r_scr, lw_scr, k_scr, v_scr, a_scr, b_scr, y_scr, rg_scr, yh_scr):
    ts = r_ref.shape[0]
    first_row = _iota((8, 1), 0) == 0

    def shift(x, mu, slot):
        w = x.shape[1]
        rolled = pltpu.roll(x, 1, 0)
        head = jnp.where(first_row, prev_scr[slot:slot + 1, 0:w], rolled[0:8, :])
        prev = jnp.concatenate([head, rolled[8:, :]], axis=0)
        prev_scr[slot:slot + 1, 0:w] = x[ts - 1:ts, :]
        return x + (prev - x) * mu

    r = shift(r_ref[...].astype(F32), mur_ref[...], 0)
    k = shift(k_ref[...].astype(F32), muk_ref[...], 1)
    v = shift(v_ref[...].astype(F32), muv_ref[...], 2)
    wa = shift(misc_ref[:, 0:LANES].astype(F32), muwa_ref[...], 3)

    lw_scr[...] = -DECAY_SCALE * _sigmoid(w0_ref[...] + _dot(jnp.tanh(wa), wlb_ref[...]))
    a = _sigmoid(a0_ref[...] + _dot(wa, alb_ref[...]))

    in_first_t = _iota((ts, PAIR), 1) < RWKV_HEAD_DIM
    kk = k * kk_ref[...]
    k2 = k * (1.0 + (a - 1.0) * ka_ref[...])
    rkr = r * k2 * rk_ref[...]
    for p in range(N_PAIRS):
        sl = slice(PAIR * p, PAIR * (p + 1))
        kkp = kk[:, sl]
        kkn = kkp * lax.rsqrt(jnp.maximum(_pair_sum(kkp * kkp, in_first_t), 1e-24))
        a_scr[:, sl] = -kkn
        b_scr[:, sl] = kkn * a[:, sl]
        y_scr[:, sl] = _pair_sum(rkr[:, sl], in_first_t)
    bonus = y_scr[...] * v
    r_scr[...] = r
    k_scr[...] = k2
    v_scr[...] = v

    tri = _tri_incl_bf16()
    in_first = _iota((CHUNK, PAIR), 1) < RWKV_HEAD_DIM
    rr = _iota((PAIR, PAIR), 0)
    cc = _iota((PAIR, PAIR), 1)
    same = lax.shift_right_logical(rr, CHUNK_SHIFT) == lax.shift_right_logical(cc, CHUNK_SHIFT)
    strict = same & (rr > cc)
    incl = same & (rr >= cc)
    eye_b = rr == cc
    eye = eye_b.astype(F32)
    zeros_pp = jnp.zeros((PAIR, PAIR), BF16)
    n_chunks = ts // CHUNK
    group = RWKV_GROUP_CHUNKS if n_chunks % RWKV_GROUP_CHUNKS == 0 else 1

    def group_body(gi, carry):
        atm, rtm, vm, bkh, bke, wend = [], [], [], [], [], []
        for ci in range(group):
            r0 = pl.multiple_of((gi * group + ci) * CHUNK, CHUNK)
            rows = pl.ds(r0, CHUNK)
            lw = lw_scr[rows, :]
            cum = _dot_exact_lhs(tri, lw)
            tot = cum[CHUNK - 1:CHUNK, :]
            e_neg = jnp.exp(-cum)
            e_end = jnp.exp(tot - cum)
            w_end = jnp.exp(tot)
            rc, kc, vc, ac, bc = (r_scr[rows, :], k_scr[rows, :], v_scr[rows, :], a_scr[rows, :],
                                  b_scr[rows, :])
            a_t = ac * jnp.exp(cum - lw)
            r_t = rc * jnp.exp(cum)
            b_h = bc * e_neg
            k_h = kc * e_neg
            b_e = bc * e_end
            k_e = kc * e_end
            for p in range(N_PAIRS):
                sl = slice(PAIR * p, PAIR * (p + 1))
                st = lambda x: _stack_heads(x[:, sl], in_first)
                atm.append(st(a_t))
                rtm.append(st(r_t))
                vm.append(st(vc).astype(BF16))
                bkh.append(jnp.concatenate([st(b_h), st(k_h)], axis=0).astype(BF16))
                bke.append(jnp.concatenate([st(b_e), st(k_e)], axis=0).astype(BF16))
                wend.append(w_end[:, sl])
        ids = range(len(atm))
        amat = [_dot_nt(jnp.concatenate([atm[i], rtm[i]], axis=0), bkh[i]) for i in ids]
        n_ab = [jnp.where(strict, amat[i][0:PAIR, 0:PAIR], 0.0) for i in ids]
        a_ak = [jnp.where(strict, amat[i][0:PAIR, PAIR:], 0.0) for i in ids]
        a_r = [jnp.concatenate([jnp.where(incl, amat[i][PAIR:, 0:PAIR], 0.0),
                                jnp.where(incl, amat[i][PAIR:, PAIR:], 0.0)], axis=1).astype(BF16)
               for i in ids]
        akv = [_dot(a_ak[i], vm[i]) for i in ids]
        npow = [_dot(n_ab[i], n_ab[i]) for i in ids]
        tinv = [eye + n_ab[i] for i in ids]
        for _ in range(CHUNK_SHIFT - 2):
            both = [_dot(npow[i], jnp.concatenate([npow[i], tinv[i]], axis=1)) for i in ids]
            npow = [both[i][:, 0:PAIR] for i in ids]
            tinv = [tinv[i] + both[i][:, PAIR:] for i in ids]
        last = [_dot(npow[i], tinv[i]) for i in ids]
        tinv = [tinv[i] + last[i] for i in ids]
        au = [_dot(tinv[i], jnp.concatenate([atm[i], akv[i]], axis=1)) for i in ids]
        rhs = [jnp.concatenate([au[i].astype(BF16), jnp.concatenate([zeros_pp, vm[i]], axis=1)], axis=0)
               for i in ids]
        ry = [jnp.dot(a_r[i], rhs[i], preferred_element_type=F32) for i in ids]
        gh = [_dot_tn(bke[i], rhs[i]) for i in ids]
        for i in ids:
            c = gi * group + i // N_PAIRS
            p = i % N_PAIRS
            gmat = gh[i][:, 0:PAIR] + jnp.where(eye_b, wend[i], 0.0)
            rg_scr[c, p] = jnp.concatenate([rtm[i] + ry[i][:, 0:PAIR], gmat], axis=0).astype(BF16)
            yh_scr[c, p] = jnp.concatenate([ry[i][:, PAIR:], gh[i][:, PAIR:]], axis=0)
        return carry

    lax.fori_loop(0, n_chunks // group, group_body, 0)

    def sweep_body(c, carry):
        r0 = c * CHUNK
        s2 = []
        for p in range(N_PAIRS):
            s_hi, s_lo = _split(state_scr[p])
            s2.append(jnp.concatenate([s_hi, s_lo], axis=1))
        out = [jnp.dot(rg_scr[c, p], s2[p], preferred_element_type=F32) for p in range(N_PAIRS)]
        for p in range(N_PAIRS):
            tot = yh_scr[c, p] + out[p][:, 0:PAIR] + out[p][:, PAIR:]
            state_scr[p] = tot[PAIR:, :]
            y_scr[pl.ds(r0, CHUNK), PAIR * p:PAIR * (p + 1)] = tot[0:CHUNK, :] + tot[CHUNK:PAIR, :]
        return carry

    for c in range(n_chunks):
        sweep_body(c, 0)

    g = g_ref[...].astype(F32)
    for p in range(N_PAIRS):
        sl = slice(PAIR * p, PAIR * (p + 1))
        y = y_scr[:, sl]
        mean = _pair_sum(y, in_first_t) * (1.0 / RWKV_HEAD_DIM)
        d = y - mean
        var = _pair_sum(d * d, in_first_t) * (1.0 / RWKV_HEAD_DIM)
        yn = d * lax.rsqrt(var + GN_EPS) * lnw_ref[:, sl] + lnb_ref[:, sl]
        o_ref[:, sl] = ((yn + bonus[:, sl]) * _silu(g[:, sl])).astype(o_ref.dtype)


def _rwkv_call(proj, mu_rkv, mu_w, mu_a, w0, w_lora_b, a0, a_lora_b, k_k, k_a, r_k, lnx_w, lnx_b,
               batch, seq):
    ts = min(seq, 512)
    nt = seq // ts
    w = RWKV_WIDTH
    muwa = jnp.concatenate([mu_w, mu_a]).reshape(1, LANES)
    zl = jnp.zeros((DECAY_LORA, w), F32)
    wlb = jnp.concatenate([w_lora_b, zl], axis=0).astype(BF16)
    alb = jnp.concatenate([zl, a_lora_b], axis=0).astype(BF16)
    blk = lambda col: pl.BlockSpec((ts, PROJ_BLOCK), lambda b, i: (b * nt + i, col // PROJ_BLOCK))
    row = lambda width: pl.BlockSpec((1, width), lambda b, i: (0, 0))
    lora = pl.BlockSpec((LANES, w), lambda b, i: (0, 0))
    tile = pltpu.VMEM((ts, w), F32)
    return pl.pallas_call(
        _rwkv_kernel,
        out_shape=jax.ShapeDtypeStruct((batch * seq, w), BF16),
        grid=(batch, nt),
        in_specs=[blk(COL_R), blk(COL_K), blk(COL_V), blk(COL_MISC), blk(COL_GRWKV),
                  row(w), row(w), row(w), row(LANES), row(w), lora, row(w), lora,
                  row(w), row(w), row(w), row(w), row(w)],
        out_specs=pl.BlockSpec((ts, w), lambda b, i: (b * nt + i, 0)),
        scratch_shapes=[pltpu.VMEM((8, w), F32),
                        pltpu.VMEM((N_PAIRS, PAIR, PAIR), F32),
                        tile, tile, tile, tile, tile, tile, tile,
                        pltpu.VMEM((ts // CHUNK, N_PAIRS, 2 * PAIR, PAIR), BF16),
                        pltpu.VMEM((ts // CHUNK, N_PAIRS, 2 * PAIR, PAIR), F32)],
        compiler_params=_params(2), name="rwkv7",
    )(proj, proj, proj, proj, proj,
      mu_rkv[0:1], mu_rkv[1:2], mu_rkv[2:3], muwa, w0.reshape(1, w), wlb, a0.reshape(1, w), alb,
      k_k.reshape(1, w), k_a.reshape(1, w), r_k.reshape(1, w), lnx_w.reshape(1, w), lnx_b.reshape(1, w))


def _outproj_kernel(ym_ref, ys_ref, yr_ref, x_ref, gate_ref, w_ref, fnw_ref, o_ref, *, final):
    acc = (_dot(ym_ref[...], w_ref[0:MLA_WIDTH, :])
           + _dot(ys_ref[...], w_ref[MLA_WIDTH:MLA_WIDTH + SSD_WIDTH, :])
           + _dot(yr_ref[...], w_ref[MLA_WIDTH + SSD_WIDTH:, :]))
    xn = x_ref[...] + gate_ref[0] * acc
    o_ref[...] = _rms(xn, fnw_ref[...]) if final else xn


def _outproj_call(y_mla, y_ssd, y_rwkv, x2, gate, w_out, final_norm_w, seq, final):
    t, d = x2.shape
    tm = min(seq, 1024)
    per_batch = seq // tm
    yspec = lambda wd: pl.BlockSpec((tm, wd), lambda i: (i, 0))
    return pl.pallas_call(
        functools.partial(_outproj_kernel, final=final),
        out_shape=jax.ShapeDtypeStruct((t, d), F32),
        grid=(t // tm,),
        in_specs=[yspec(MLA_WIDTH), yspec(SSD_WIDTH), yspec(RWKV_WIDTH), yspec(d),
                  pl.BlockSpec((1, 1, d), lambda i: (i // per_batch, 0, 0)),
                  pl.BlockSpec((D_MIX, d), lambda i: (0, 0)),
                  pl.BlockSpec((1, d), lambda i: (0, 0))],
        out_specs=yspec(d),
        compiler_params=_params(1), name="out_proj",
    )(y_mla, y_ssd, y_rwkv, x2, gate, w_out.astype(BF16), final_norm_w.reshape(1, d))


def kernel(x, c, positions, ada_w, ada_b, norm_w, w_in, q_norm_w, w_uq, kv_norm_w, w_ukv, conv_w, conv_b, dt_bias, a_log, d_skip, ssd_norm_w, mu_rkv, mu_w, mu_a, w0, w_lora_b, a0, a_lora_b, k_k, k_a, r_k, lnx_w, lnx_b, w_out, final_norm_w):
    batch, seq, d = x.shape
    depth = ada_w.shape[0]
    assert d == D_MODEL and batch <= 8 and seq % CHUNK == 0
    t = batch * seq
    x2 = x.reshape(t, d)
    mod = _mod_call(jnp.pad(c, ((0, 8 - batch), (0, 0))), ada_w, ada_b)
    rope_tab = _rope_call(positions.reshape(t, 1))
    wp_all = _pack_w_in(w_in)
    for l in range(depth):
        shift = mod[l, :batch, 0:d].reshape(batch, 1, d)
        scale = mod[l, :batch, d:2 * d].reshape(batch, 1, d)
        gate = mod[l, :batch, 2 * d:].reshape(batch, 1, d)
        proj = _inproj_call(x2, scale, shift, norm_w[l], wp_all, l, seq)
        q, k, v = _mla_prep_call(proj, rope_tab, q_norm_w[l], w_uq[l], kv_norm_w[l], w_ukv[l], batch, seq)
        y_mla = _attn_call(q, k, v, proj, batch, seq)
        y_ssd = _ssd_call(proj, conv_w[l], conv_b[l], dt_bias[l], a_log[l], d_skip[l], ssd_norm_w[l],
                          batch, seq)
        y_rwkv = _rwkv_call(proj, mu_rkv[l], mu_w[l], mu_a[l], w0[l], w_lora_b[l], a0[l], a_lora_b[l],
                            k_k[l], k_a[l], r_k[l], lnx_w[l], lnx_b[l], batch, seq)
        x2 = _outproj_call(y_mla, y_ssd, y_rwkv, x2, gate, w_out[l], final_norm_w, seq,
                           final=(l == depth - 1))
    return x2.reshape(batch, seq, d)
```

```python
import functools

import jax
import jax.numpy as jnp
import numpy as np
from jax import lax
from jax.experimental import pallas as pl
from jax.experimental.pallas import tpu as pltpu

F32 = jnp.float32
BF16 = jnp.bfloat16

D_MODEL = 1024
CHUNK = 64
CHUNK_SHIFT = 6
NORM_EPS = 1e-6

MLA_HEADS = 4
MLA_Q_RANK = 256
MLA_KV_RANK = 128
MLA_NOPE = 128
MLA_ROPE = 64
MLA_V = 128
MLA_WIDTH = MLA_HEADS * MLA_V
MLA_QK_PAD = 256
MLA_VT_ROWS = 144
ATTN_Q_SUB = 256
ATTN_HEADS_PER_ITER = 4
LOG2_E = 1.4426950408889634
ROPE_THETA = 10000.0

SSD_HEADS = 8
SSD_HEAD_DIM = 64
SSD_WIDTH = SSD_HEADS * SSD_HEAD_DIM
SSD_GROUPS = 2
SSD_STATE = 128
SSD_CONV = 4
SSD_XBC = SSD_WIDTH + 2 * SSD_GROUPS * SSD_STATE

RWKV_HEADS = 8
RWKV_HEAD_DIM = 64
RWKV_WIDTH = RWKV_HEADS * RWKV_HEAD_DIM
DECAY_LORA = 64
ICLR_LORA = 64
DECAY_SCALE = 0.606531
GN_EPS = 64e-5

D_MIX = MLA_WIDTH + SSD_WIDTH + RWKV_WIDTH

LANES = 128
PAIR = 2 * RWKV_HEAD_DIM
N_PAIRS = RWKV_HEADS // 2
RWKV_GROUP_CHUNKS = 8
SSD_GROUP_CHUNKS = 8

COL_XBC = 0
COL_GMLA = 1024
COL_Z = 1536
COL_R = 2048
COL_K = 2560
COL_V = 3072
COL_GRWKV = 3584
COL_LAT = 4096
COL_MISC = 4608
MISC_BLOCK = 256
PROJ_COLS = COL_MISC + MISC_BLOCK
PROJ_BLOCK = 512

VMEM_LIMIT_BYTES = 56 * 1024 * 1024


def _params(n_axes):
    return pltpu.CompilerParams(dimension_semantics=("arbitrary",) * n_axes,
                                vmem_limit_bytes=VMEM_LIMIT_BYTES)


def _dot(a, b):
    return jnp.dot(a.astype(BF16), b.astype(BF16), preferred_element_type=F32)


def _dot_nt(a, b):
    return lax.dot_general(a.astype(BF16), b.astype(BF16), (((1,), (1,)), ((), ())),
                           preferred_element_type=F32)


def _dot_tn(a, b):
    return lax.dot_general(a.astype(BF16), b.astype(BF16), (((0,), (0,)), ((), ())),
                           preferred_element_type=F32)


def _split(x):
    hi = x.astype(BF16)
    lo = (x - hi.astype(F32)).astype(BF16)
    return hi, lo


def _dot_exact_rhs(a, b_bf16):
    hi, lo = _split(a)
    return (jnp.dot(hi, b_bf16, preferred_element_type=F32)
            + jnp.dot(lo, b_bf16, preferred_element_type=F32))


def _dot_exact_lhs(a_bf16, b):
    hi, lo = _split(b)
    return (jnp.dot(a_bf16, hi, preferred_element_type=F32)
            + jnp.dot(a_bf16, lo, preferred_element_type=F32))


def _sigmoid(x):
    return 0.5 * jnp.tanh(0.5 * x) + 0.5


def _silu(x):
    h = 0.5 * x
    return h + h * jnp.tanh(h)


def _iota(shape, dim):
    return lax.broadcasted_iota(jnp.int32, shape, dim)


def _mod_kernel(c_ref, w_ref, b_ref, o_ref):
    c = c_ref[...]
    ch, cl = _split(_silu(c))
    wh, wl = _split(w_ref[0])
    acc = (jnp.dot(ch, wh, preferred_element_type=F32)
           + jnp.dot(cl, wh, preferred_element_type=F32)
           + jnp.dot(ch, wl, preferred_element_type=F32))
    o_ref[0] = acc + b_ref[0]


def _mod_call(c8, ada_w, ada_b):
    depth, d, d3 = ada_w.shape
    tn = 512
    return pl.pallas_call(
        _mod_kernel,
        out_shape=jax.ShapeDtypeStruct((depth, 8, d3), F32),
        grid=(depth, d3 // tn),
        in_specs=[pl.BlockSpec((8, d), lambda l, j: (0, 0)),
                  pl.BlockSpec((1, d, tn), lambda l, j: (l, 0, j)),
                  pl.BlockSpec((1, 1, tn), lambda l, j: (l, 0, j))],
        out_specs=pl.BlockSpec((1, 8, tn), lambda l, j: (l, 0, j)),
        compiler_params=_params(2), name="adaln_mod",
    )(c8, ada_w, ada_b.reshape(depth, 1, d3))


def _rope_kernel(pos_ref, freq_ref, sign_ref, o_ref):
    ang = pos_ref[...].astype(F32) * freq_ref[...]
    o_ref[:, 0:MLA_ROPE] = jnp.cos(ang)
    o_ref[:, MLA_ROPE:2 * MLA_ROPE] = jnp.sin(ang) * sign_ref[...]


def _rope_call(pos_col):
    t = pos_col.shape[0]
    tr = min(t, 1024)
    half = MLA_ROPE // 2
    inv_freq = ROPE_THETA ** (-jnp.arange(half, dtype=F32) / half)
    freq = jnp.concatenate([inv_freq, inv_freq]).reshape(1, MLA_ROPE)
    sign = jnp.concatenate([-jnp.ones((half,), F32), jnp.ones((half,), F32)]).reshape(1, MLA_ROPE)
    return pl.pallas_call(
        _rope_kernel,
        out_shape=jax.ShapeDtypeStruct((t, 2 * MLA_ROPE), F32),
        grid=(t // tr,),
        in_specs=[pl.BlockSpec((tr, 1), lambda i: (i, 0)),
                  pl.BlockSpec((1, MLA_ROPE), lambda i: (0, 0)),
                  pl.BlockSpec((1, MLA_ROPE), lambda i: (0, 0))],
        out_specs=pl.BlockSpec((tr, 2 * MLA_ROPE), lambda i: (i, 0)),
        compiler_params=_params(1), name="rope_table",
    )(pos_col, freq, sign)


def _inproj_kernel(x_ref, scale_ref, shift_ref, nw_ref, w_ref, o_ref):
    x = x_ref[...]
    ms = jnp.mean(x * x, axis=-1, keepdims=True)
    y = x * lax.rsqrt(ms + NORM_EPS) * nw_ref[...]
    h = (y * (1.0 + scale_ref[0]) + shift_ref[0]).astype(BF16)
    o_ref[...] = jnp.dot(h, w_ref[...], preferred_element_type=F32).astype(o_ref.dtype)


def _inproj_call(x2, scale, shift, norm_w, wp, seq):
    t, d = x2.shape
    tm = min(seq, 512)
    per_batch = seq // tm
    return pl.pallas_call(
        _inproj_kernel,
        out_shape=jax.ShapeDtypeStruct((t, PROJ_COLS), BF16),
        grid=(t // tm,),
        in_specs=[pl.BlockSpec((tm, d), lambda i: (i, 0)),
                  pl.BlockSpec((1, 1, d), lambda i: (i // per_batch, 0, 0)),
                  pl.BlockSpec((1, 1, d), lambda i: (i // per_batch, 0, 0)),
                  pl.BlockSpec((1, d), lambda i: (0, 0)),
                  pl.BlockSpec((d, PROJ_COLS), lambda i: (0, 0))],
        out_specs=pl.BlockSpec((tm, PROJ_COLS), lambda i: (i, 0)),
        compiler_params=_params(1), name="in_proj",
    )(x2, scale, shift, norm_w.reshape(1, d), wp)


def _pack_w_in(w):
    o = np.cumsum((0, MLA_Q_RANK, MLA_KV_RANK, MLA_ROPE, MLA_WIDTH, SSD_WIDTH, SSD_XBC, SSD_HEADS,
                   RWKV_WIDTH, RWKV_WIDTH, RWKV_WIDTH, DECAY_LORA, ICLR_LORA, RWKV_WIDTH))
    q_lat, kv_lat, k_pe, g_mla, z, xbc, dt, r, k, v, w_lo, a_lo, g_rwkv = (
        w[..., o[i]:o[i + 1]] for i in range(13))
    half = MLA_ROPE // 2
    misc_pad = jnp.zeros(w.shape[:-1] + (MISC_BLOCK - DECAY_LORA - ICLR_LORA - SSD_HEADS,), w.dtype)
    return jnp.concatenate([xbc, g_mla, z, r, k, v, g_rwkv, q_lat, kv_lat, k_pe, k_pe[..., half:],
                            k_pe[..., :half], w_lo, a_lo, dt, misc_pad], axis=-1).astype(BF16)


def _rms(x, w):
    ms = jnp.mean(x * x, axis=-1, keepdims=True)
    return x * lax.rsqrt(ms + NORM_EPS) * w


def _mla_prep_kernel(lat_ref, rope_ref, qnw_ref, wuq_ref, kvnw_ref, wuk_ref, wuvt_ref, q_ref, k_ref, v_ref):
    lat = lat_ref[...].astype(F32)
    cs = rope_ref[...]
    q = _dot(_rms(lat[:, 0:MLA_Q_RANK], qnw_ref[...]), wuq_ref[...])
    kvn = _rms(lat[:, MLA_Q_RANK:MLA_Q_RANK + MLA_KV_RANK], kvnw_ref[...]).astype(BF16)
    k_nope = jnp.dot(kvn, wuk_ref[...], preferred_element_type=F32)
    v_t = _dot_nt(wuvt_ref[...], kvn)
    kp = lat[:, MLA_Q_RANK + MLA_KV_RANK:] * cs
    k_pe = (kp[:, 0:MLA_ROPE] + kp[:, MLA_ROPE:]).astype(BF16)
    rows = lat.shape[0]
    zpad = jnp.zeros((rows, MLA_QK_PAD - MLA_NOPE - MLA_ROPE), BF16)
    scale = (MLA_NOPE + MLA_ROPE) ** -0.5 * LOG2_E
    ones_row = (_iota((MLA_VT_ROWS - MLA_V, rows), 0) == 0).astype(BF16)
    for h in range(MLA_HEADS):
        qh = q[:, 256 * h:256 * (h + 1)]
        qp = qh[:, MLA_NOPE:] * cs
        q_pe = qp[:, 0:MLA_ROPE] + qp[:, MLA_ROPE:]
        q_ref[0, h] = jnp.concatenate([(qh[:, 0:MLA_NOPE] * scale).astype(BF16),
                                       (q_pe * scale).astype(BF16), zpad], axis=1)
        k_ref[0, h] = jnp.concatenate([k_nope[:, MLA_NOPE * h:MLA_NOPE * (h + 1)].astype(BF16), k_pe, zpad],
                                      axis=1)
        v_ref[0, h, 0:MLA_V, :] = v_t[MLA_V * h:MLA_V * (h + 1), :].astype(BF16)
        v_ref[0, h, MLA_V:, :] = ones_row


def _split_w_ukv(w_ukv):
    w = w_ukv.astype(BF16).reshape(MLA_KV_RANK, MLA_HEADS, MLA_NOPE + MLA_V)
    w_k = w[:, :, :MLA_NOPE].reshape(MLA_KV_RANK, MLA_HEADS * MLA_NOPE)
    w_vt = w[:, :, MLA_NOPE:].reshape(MLA_KV_RANK, MLA_HEADS * MLA_V).T
    return w_k, w_vt


def _pack_w_uq(w_uq):
    half = MLA_ROPE // 2
    per = MLA_NOPE + MLA_ROPE
    cols = []
    for h in range(MLA_HEADS):
        wh = w_uq[:, per * h:per * (h + 1)]
        pe = wh[:, MLA_NOPE:]
        cols += [wh, pe[:, half:], pe[:, :half]]
    return jnp.concatenate(cols, axis=1).astype(BF16)


def _mla_prep_call(proj, rope_tab, q_norm_w, w_uq, kv_norm_w, w_ukv, batch, seq):
    tm = min(seq, 1024)
    per_batch = seq // tm
    lat_blk = COL_LAT // PROJ_BLOCK
    qk_shape = jax.ShapeDtypeStruct((batch, MLA_HEADS, seq, MLA_QK_PAD), BF16)
    qk_spec = pl.BlockSpec((1, MLA_HEADS, tm, MLA_QK_PAD), lambda i: (i // per_batch, 0, i % per_batch, 0))
    return pl.pallas_call(
        _mla_prep_kernel,
        out_shape=(qk_shape, qk_shape, jax.ShapeDtypeStruct((batch, MLA_HEADS, MLA_VT_ROWS, seq), BF16)),
        grid=(batch * per_batch,),
        in_specs=[pl.BlockSpec((tm, PROJ_BLOCK), lambda i: (i, lat_blk)),
                  pl.BlockSpec((tm, 2 * MLA_ROPE), lambda i: (i, 0)),
                  pl.BlockSpec((1, MLA_Q_RANK), lambda i: (0, 0)),
                  pl.BlockSpec((MLA_Q_RANK, 256 * MLA_HEADS), lambda i: (0, 0)),
                  pl.BlockSpec((1, MLA_KV_RANK), lambda i: (0, 0)),
                  pl.BlockSpec((MLA_KV_RANK, MLA_NOPE * MLA_HEADS), lambda i: (0, 0)),
                  pl.BlockSpec((MLA_V * MLA_HEADS, MLA_KV_RANK), lambda i: (0, 0))],
        out_specs=(qk_spec, qk_spec,
                   pl.BlockSpec((1, MLA_HEADS, MLA_VT_ROWS, tm), lambda i: (i // per_batch, 0, 0, i % per_batch))),
        compiler_params=_params(1), name="mla_prep",
    )(proj, rope_tab, q_norm_w.reshape(1, -1), _pack_w_uq(w_uq), kv_norm_w.reshape(1, -1),
      *_split_w_ukv(w_ukv))


def _attn_kernel(qi_tab, ki_tab, q_ref, k_ref, vt_ref, g_ref, o_ref, m_scr, acc_scr):
    p = pl.program_id(1)
    qi = qi_tab[p]
    ki = ki_tab[p]
    tq = q_ref.shape[2]
    sub = min(ATTN_Q_SUB, tq)

    @pl.when(ki == 0)
    def _():
        m_scr[...] = jnp.full(m_scr.shape, -jnp.inf, F32)
        acc_scr[...] = jnp.zeros(acc_scr.shape, F32)

    def step(diagonal):
        def head_body(hi, carry):
            chains = [(hi * ATTN_HEADS_PER_ITER + hh, j) for hh in range(ATTN_HEADS_PER_ITER)
                      for j in range(tq // sub)]
            ids = range(len(chains))
            head = [h for h, _ in chains]
            cols = [slice(sub * j, sub * (j + 1)) for _, j in chains]
            n_keys = [sub * (j + 1) if diagonal else tq for _, j in chains]
            s = [lax.dot_general(k_ref[0, head[i], 0:n_keys[i], :], q_ref[0, head[i], cols[i], :],
                                 (((1,), (1,)), ((), ())), preferred_element_type=F32) for i in ids]
            if diagonal:
                for i in ids:
                    key = lax.shift_right_logical(_iota(s[i].shape, 0), CHUNK_SHIFT)
                    qry = lax.shift_right_logical(_iota(s[i].shape, 1) + sub * chains[i][1], CHUNK_SHIFT)
                    s[i] = jnp.where(key <= qry, s[i], -jnp.inf)
            m_prev = [m_scr[head[i], 0:1, cols[i]] for i in ids]
            m_new = [jnp.maximum(m_prev[i], jnp.max(s[i], axis=0, keepdims=True)) for i in ids]
            pm = [jnp.exp2(s[i] - m_new[i]).astype(BF16) for i in ids]
            pv = [jnp.dot(vt_ref[0, head[i], :, 0:n_keys[i]], pm[i], preferred_element_type=F32)
                  for i in ids]
            for i in ids:
                acc_scr[head[i], :, cols[i]] = (jnp.exp2(m_prev[i] - m_new[i]) * acc_scr[head[i], :, cols[i]]
                                                + pv[i])
                m_scr[head[i], 0:1, cols[i]] = m_new[i]
            return carry

        lax.fori_loop(0, MLA_HEADS // ATTN_HEADS_PER_ITER, head_body, 0)

    @pl.when(ki < qi)
    def _():
        step(False)

    @pl.when(ki == qi)
    def _():
        step(True)
        g = g_ref[...].astype(F32)
        for h in range(MLA_HEADS):
            acc = acc_scr[h]
            o = (acc[0:MLA_V, :] / acc[MLA_V:MLA_V + 1, :]).T
            o_ref[:, MLA_V * h:MLA_V * (h + 1)] = (o * _silu(g[:, MLA_V * h:MLA_V * (h + 1)])).astype(o_ref.dtype)


def _attn_call(q, k, vt, proj, batch, seq):
    tq = min(seq, 1024)
    nq = seq // tq
    pairs = [(a, b) for a in range(nq) for b in range(a + 1)]
    qi_tab = jnp.asarray([a for a, _ in pairs], jnp.int32)
    ki_tab = jnp.asarray([b for _, b in pairs], jnp.int32)
    grid_spec = pltpu.PrefetchScalarGridSpec(
        num_scalar_prefetch=2,
        grid=(batch, len(pairs)),
        in_specs=[pl.BlockSpec((1, MLA_HEADS, tq, MLA_QK_PAD), lambda b, p, qt, kt: (b, 0, qt[p], 0)),
                  pl.BlockSpec((1, MLA_HEADS, tq, MLA_QK_PAD), lambda b, p, qt, kt: (b, 0, kt[p], 0)),
                  pl.BlockSpec((1, MLA_HEADS, MLA_VT_ROWS, tq), lambda b, p, qt, kt: (b, 0, 0, kt[p])),
                  pl.BlockSpec((tq, MLA_WIDTH), lambda b, p, qt, kt: (b * nq + qt[p], COL_GMLA // MLA_WIDTH))],
        out_specs=pl.BlockSpec((tq, MLA_WIDTH), lambda b, p, qt, kt: (b * nq + qt[p], 0)),
        scratch_shapes=[pltpu.VMEM((MLA_HEADS, 8, tq), F32),
                        pltpu.VMEM((MLA_HEADS, MLA_VT_ROWS, tq), F32)],
    )
    return pl.pallas_call(
        _attn_kernel,
        out_shape=jax.ShapeDtypeStruct((batch * seq, MLA_WIDTH), BF16),
        grid_spec=grid_spec,
        compiler_params=_params(2), name="mla_attn",
    )(qi_tab, ki_tab, q, k, vt, proj)


def _tri_incl_bf16():
    return (_iota((CHUNK, CHUNK), 0) >= _iota((CHUNK, CHUNK), 1)).astype(BF16)


SSD_HALO = 8
N_SSD_IN = 10
N_RWKV_IN = 18


def _ssd_init(xpad_scr, state_scr):
    xpad_scr[0:SSD_HALO, :] = jnp.zeros((SSD_HALO, SSD_XBC), F32)
    state_scr[...] = jnp.zeros(state_scr.shape, F32)


def _rwkv_init(prev_scr, state_scr):
    prev_scr[...] = jnp.zeros(prev_scr.shape, F32)
    state_scr[...] = jnp.zeros(state_scr.shape, F32)


def _ssd_kernel(*refs):
    @pl.when(pl.program_id(1) == 0)
    def _():
        _ssd_init(refs[N_SSD_IN + 1], refs[N_SSD_IN + 2])

    _ssd_body(*refs)


def _rwkv_kernel(*refs):
    @pl.when(pl.program_id(1) == 0)
    def _():
        _rwkv_init(refs[N_RWKV_IN + 1], refs[N_RWKV_IN + 2])

    _rwkv_body(*refs)


def _ssd_body(xbc_ref, z_ref, misc_ref, convw_ref, convb_ref, dtb_ref, alog_ref, dskip_ref, nw_ref,
              expand_ref, o_ref, xpad_scr, state_scr, xc_scr, xdt_scr, dta_scr, y_scr, inc_scr, dec_scr,
              pre_scr):
    ts = xbc_ref.shape[0]
    halo = SSD_HALO
    xpad_scr[halo:halo + ts, :] = xbc_ref[...].astype(F32)
    conv = convb_ref[...]
    for i in range(SSD_CONV):
        start = halo - (SSD_CONV - 1) + i
        conv = conv + convw_ref[i:i + 1, :] * xpad_scr[start:start + ts, :]
    xpad_scr[0:halo, :] = xpad_scr[ts:ts + halo, :]
    xc_scr[...] = _silu(conv)

    dt = jax.nn.softplus(misc_ref[:, LANES:2 * LANES].astype(F32) + dtb_ref[...])
    dt_x = _dot_exact_rhs(dt, expand_ref[...])
    xdt_scr[...] = xc_scr[:, 0:SSD_WIDTH] * dt_x
    dta_scr[...] = dt_x * (-jnp.exp(alog_ref[...]))

    tri = _tri_incl_bf16()
    lane = _iota((CHUNK, PAIR), 1)
    row = _iota((CHUNK, PAIR), 0)
    in_first = lane < SSD_HEAD_DIM
    src = jnp.where(in_first, lane, lane - SSD_HEAD_DIM)
    diag2 = row == src
    causal2 = row >= src
    gw = SSD_WIDTH // SSD_GROUPS

    n_chunks = ts // CHUNK
    group = SSD_GROUP_CHUNKS if n_chunks % SSD_GROUP_CHUNKS == 0 else 1
    b_cols = [slice(SSD_WIDTH + SSD_STATE * g, SSD_WIDTH + SSD_STATE * (g + 1)) for g in range(SSD_GROUPS)]
    c_cols = [slice(SSD_WIDTH + SSD_STATE * (SSD_GROUPS + g), SSD_WIDTH + SSD_STATE * (SSD_GROUPS + g + 1))
              for g in range(SSD_GROUPS)]
    g_lanes = [slice(gw * g, gw * (g + 1)) for g in range(SSD_GROUPS)]

    def group_body(gi, carry):
        cs = range(group)
        rows = [pl.ds(pl.multiple_of((gi * group + ci) * CHUNK, CHUNK), CHUNK) for ci in cs]
        cum = [_dot_exact_lhs(tri, dta_scr[rows[ci], :]) for ci in cs]
        xdt = [xdt_scr[rows[ci], :] for ci in cs]
        bm = [[xc_scr[rows[ci], b_cols[g]] for g in range(SSD_GROUPS)] for ci in cs]
        cm = [[xc_scr[rows[ci], c_cols[g]] for g in range(SSD_GROUPS)] for ci in cs]
        tot = [cum[ci][CHUNK - 1:CHUNK, :] for ci in cs]
        xw = [xdt[ci] * jnp.exp(tot[ci] - cum[ci]) for ci in cs]
        cb2 = [[_dot_nt(cm[ci][g], jnp.concatenate([bm[ci][g], bm[ci][g]], axis=0))
                for g in range(SSD_GROUPS)] for ci in cs]
        inc = [[_dot_tn(bm[ci][g], xw[ci][:, g_lanes[g]]) for g in range(SSD_GROUPS)] for ci in cs]
        lhs, rhs = [], []
        for ci in cs:
            for p in range(N_PAIRS):
                pl_ = slice(PAIR * p, PAIR * (p + 1))
                colp = cum[ci][:, pl_]
                rowp = jnp.sum(jnp.where(diag2, colp, 0.0), axis=0, keepdims=True)
                dec = jnp.exp(jnp.where(causal2, colp - rowp, -jnp.inf))
                lhs.append(cb2[ci][p // 2] * dec)
                rhs.append(_stack_heads(xdt[ci][:, pl_], in_first))
        y_diag = [_dot(lhs[i], rhs[i]) for i in range(len(lhs))]
        for ci in cs:
            c = gi * group + ci
            for p in range(N_PAIRS):
                y_scr[rows[ci], PAIR * p:PAIR * (p + 1)] = y_diag[ci * N_PAIRS + p]
            for g in range(SSD_GROUPS):
                inc_scr[c, :, g_lanes[g]] = inc[ci][g]
            dec_scr[c] = jnp.broadcast_to(jnp.exp(tot[ci]), (8, SSD_WIDTH))
            dta_scr[rows[ci], :] = jnp.exp(cum[ci])
        return carry

    lax.fori_loop(0, n_chunks // group, group_body, 0)

    state = state_scr[...]
    for c in range(n_chunks):
        pre_scr[c] = state.astype(BF16)
        state = state * dec_scr[c, 0:1, :] + inc_scr[c]
    state_scr[...] = state

    def off_body(gi, carry):
        cs = range(group)
        rows = [pl.ds(pl.multiple_of((gi * group + ci) * CHUNK, CHUNK), CHUNK) for ci in cs]
        y_off = [[jnp.dot(xc_scr[rows[ci], c_cols[g]].astype(BF16), pre_scr[gi * group + ci, :, g_lanes[g]],
                          preferred_element_type=F32) for g in range(SSD_GROUPS)] for ci in cs]
        for ci in cs:
            for g in range(SSD_GROUPS):
                y_scr[rows[ci], g_lanes[g]] = (y_scr[rows[ci], g_lanes[g]]
                                               + y_off[ci][g] * dta_scr[rows[ci], g_lanes[g]])
        return carry

    lax.fori_loop(0, n_chunks // group, off_body, 0)

    y = y_scr[...] + dskip_ref[...] * xc_scr[:, 0:SSD_WIDTH]
    o_ref[...] = _rms(y * _silu(z_ref[...].astype(F32)), nw_ref[...]).astype(o_ref.dtype)


def _head_expand(vec, width):
    return jnp.repeat(vec.astype(F32), width).reshape(1, -1)


def _ssd_call(proj, conv_w, conv_b, dt_bias, a_log, d_skip, ssd_norm_w, batch, seq):
    ts = min(seq, 512)
    nt = seq // ts
    dtb = jnp.zeros((1, LANES), F32).at[0, :SSD_HEADS].set(dt_bias)
    expand = (jnp.arange(LANES)[:, None] == (jnp.arange(SSD_WIDTH) // SSD_HEAD_DIM)[None, :]).astype(BF16)
    row = lambda w: pl.BlockSpec((1, w), lambda b, i: (0, 0))
    return pl.pallas_call(
        _ssd_kernel,
        out_shape=jax.ShapeDtypeStruct((batch * seq, SSD_WIDTH), BF16),
        grid=(batch, nt),
        in_specs=[pl.BlockSpec((ts, SSD_XBC), lambda b, i: (b * nt + i, COL_XBC // SSD_XBC)),
                  pl.BlockSpec((ts, PROJ_BLOCK), lambda b, i: (b * nt + i, COL_Z // PROJ_BLOCK)),
                  pl.BlockSpec((ts, MISC_BLOCK), lambda b, i: (b * nt + i, COL_MISC // MISC_BLOCK)),
                  pl.BlockSpec((SSD_CONV, SSD_XBC), lambda b, i: (0, 0)),
                  row(SSD_XBC), row(LANES), row(SSD_WIDTH), row(SSD_WIDTH), row(SSD_WIDTH),
                  pl.BlockSpec((LANES, SSD_WIDTH), lambda b, i: (0, 0))],
        out_specs=pl.BlockSpec((ts, SSD_WIDTH), lambda b, i: (b * nt + i, 0)),
        scratch_shapes=[pltpu.VMEM((ts + SSD_HALO, SSD_XBC), F32),
                        pltpu.VMEM((SSD_STATE, SSD_WIDTH), F32),
                        pltpu.VMEM((ts, SSD_XBC), F32),
                        pltpu.VMEM((ts, SSD_WIDTH), F32),
                        pltpu.VMEM((ts, SSD_WIDTH), F32),
                        pltpu.VMEM((ts, SSD_WIDTH), F32),
                        pltpu.VMEM((ts // CHUNK, SSD_STATE, SSD_WIDTH), F32),
                        pltpu.VMEM((ts // CHUNK, 8, SSD_WIDTH), F32),
                        pltpu.VMEM((ts // CHUNK, SSD_STATE, SSD_WIDTH), BF16)],
        compiler_params=_params(2), name="ssd",
    )(proj, proj, proj, conv_w, conv_b.reshape(1, -1), dtb, _head_expand(a_log, SSD_HEAD_DIM),
      _head_expand(d_skip, SSD_HEAD_DIM), ssd_norm_w.reshape(1, -1), expand)


def _pair_sum(x, in_first):
    s0 = jnp.sum(jnp.where(in_first, x, 0.0), axis=-1, keepdims=True)
    s1 = jnp.sum(jnp.where(in_first, 0.0, x), axis=-1, keepdims=True)
    return jnp.where(in_first, s0, s1)


def _stack_heads(x, in_first):
    return jnp.concatenate([jnp.where(in_first, x, 0.0), jnp.where(in_first, 0.0, x)], axis=0)


def _rwkv_body(r_ref, k_ref, v_ref, misc_ref, g_ref, mur_ref, muk_ref, muv_ref, muwa_ref, w0_ref,
               wlb_ref, a0_ref, alb_ref, kk_ref, ka_ref, rk_ref, lnw_ref, lnb_ref, o_ref,
               prev_scr, state_scr, r_scr, lw_scr, k_scr, v_scr, a_scr, b_scr, y_scr, rg_scr, yh_scr):
    ts = r_ref.shape[0]
    first_row = _iota((8, 1), 0) == 0

    def shift(x, mu, slot):
        w = x.shape[1]
        rolled = pltpu.roll(x, 1, 0)
        head = jnp.where(first_row, prev_scr[slot:slot + 1, 0:w], rolled[0:8, :])
        prev = jnp.concatenate([head, rolled[8:, :]], axis=0)
        prev_scr[slot:slot + 1, 0:w] = x[ts - 1:ts, :]
        return x + (prev - x) * mu

    r = shift(r_ref[...].astype(F32), mur_ref[...], 0)
    k = shift(k_ref[...].astype(F32), muk_ref[...], 1)
    v = shift(v_ref[...].astype(F32), muv_ref[...], 2)
    wa = shift(misc_ref[:, 0:LANES].astype(F32), muwa_ref[...], 3)

    lw_scr[...] = -DECAY_SCALE * _sigmoid(w0_ref[...] + _dot(jnp.tanh(wa), wlb_ref[...]))
    a = _sigmoid(a0_ref[...] + _dot(wa, alb_ref[...]))

    in_first_t = _iota((ts, PAIR), 1) < RWKV_HEAD_DIM
    kk = k * kk_ref[...]
    k2 = k * (1.0 + (a - 1.0) * ka_ref[...])
    rkr = r * k2 * rk_ref[...]
    for p in range(N_PAIRS):
        sl = slice(PAIR * p, PAIR * (p + 1))
        kkp = kk[:, sl]
        kkn = kkp * lax.rsqrt(jnp.maximum(_pair_sum(kkp * kkp, in_first_t), 1e-24))
        a_scr[:, sl] = -kkn
        b_scr[:, sl] = kkn * a[:, sl]
        y_scr[:, sl] = _pair_sum(rkr[:, sl], in_first_t)
    bonus = y_scr[...] * v
    r_scr[...] = r
    k_scr[...] = k2
    v_scr[...] = v

    tri = _tri_incl_bf16()
    in_first = _iota((CHUNK, PAIR), 1) < RWKV_HEAD_DIM
    rr = _iota((PAIR, PAIR), 0)
    cc = _iota((PAIR, PAIR), 1)
    same = lax.shift_right_logical(rr, CHUNK_SHIFT) == lax.shift_right_logical(cc, CHUNK_SHIFT)
    strict = same & (rr > cc)
    incl = same & (rr >= cc)
    eye_b = rr == cc
    eye = eye_b.astype(F32)
    zeros_pp = jnp.zeros((PAIR, PAIR), BF16)
    n_chunks = ts // CHUNK
    group = RWKV_GROUP_CHUNKS if n_chunks % RWKV_GROUP_CHUNKS == 0 else 1

    def group_body(gi, carry):
        atm, rtm, vm, bkh, bke, wend = [], [], [], [], [], []
        for ci in range(group):
            r0 = pl.multiple_of((gi * group + ci) * CHUNK, CHUNK)
            rows = pl.ds(r0, CHUNK)
            lw = lw_scr[rows, :]
            cum = _dot_exact_lhs(tri, lw)
            tot = cum[CHUNK - 1:CHUNK, :]
            e_neg = jnp.exp(-cum)
            e_end = jnp.exp(tot - cum)
            w_end = jnp.exp(tot)
            rc, kc, vc, ac, bc = (r_scr[rows, :], k_scr[rows, :], v_scr[rows, :], a_scr[rows, :],
                                  b_scr[rows, :])
            a_t = ac * jnp.exp(cum - lw)
            r_t = rc * jnp.exp(cum)
            b_h = bc * e_neg
            k_h = kc * e_neg
            b_e = bc * e_end
            k_e = kc * e_end
            for p in range(N_PAIRS):
                sl = slice(PAIR * p, PAIR * (p + 1))
                st = lambda x: _stack_heads(x[:, sl], in_first)
                atm.append(st(a_t))
                rtm.append(st(r_t))
                vm.append(st(vc).astype(BF16))
                bkh.append(jnp.concatenate([st(b_h), st(k_h)], axis=0).astype(BF16))
                bke.append(jnp.concatenate([st(b_e), st(k_e)], axis=0).astype(BF16))
                wend.append(w_end[:, sl])
        ids = range(len(atm))
        amat = [_dot_nt(jnp.concatenate([atm[i], rtm[i]], axis=0), bkh[i]) for i in ids]
        n_ab = [jnp.where(strict, amat[i][0:PAIR, 0:PAIR], 0.0) for i in ids]
        a_ak = [jnp.where(strict, amat[i][0:PAIR, PAIR:], 0.0) for i in ids]
        a_r = [jnp.concatenate([jnp.where(incl, amat[i][PAIR:, 0:PAIR], 0.0),
                                jnp.where(incl, amat[i][PAIR:, PAIR:], 0.0)], axis=1).astype(BF16)
               for i in ids]
        akv = [_dot(a_ak[i], vm[i]) for i in ids]
        npow = [_dot(n_ab[i], n_ab[i]) for i in ids]
        tinv = [eye + n_ab[i] for i in ids]
        for _ in range(CHUNK_SHIFT - 2):
            both = [_dot(npow[i], jnp.concatenate([npow[i], tinv[i]], axis=1)) for i in ids]
            npow = [both[i][:, 0:PAIR] for i in ids]
            tinv = [tinv[i] + both[i][:, PAIR:] for i in ids]
        last = [_dot(npow[i], tinv[i]) for i in ids]
        tinv = [tinv[i] + last[i] for i in ids]
        au = [_dot(tinv[i], jnp.concatenate([atm[i], akv[i]], axis=1)) for i in ids]
        rhs = [jnp.concatenate([au[i].astype(BF16), jnp.concatenate([zeros_pp, vm[i]], axis=1)], axis=0)
               for i in ids]
        ry = [jnp.dot(a_r[i], rhs[i], preferred_element_type=F32) for i in ids]
        gh = [_dot_tn(bke[i], rhs[i]) for i in ids]
        for i in ids:
            c = gi * group + i // N_PAIRS
            p = i % N_PAIRS
            gmat = gh[i][:, 0:PAIR] + jnp.where(eye_b, wend[i], 0.0)
            rg_scr[c, p] = jnp.concatenate([rtm[i] + ry[i][:, 0:PAIR], gmat], axis=0).astype(BF16)
            yh_scr[c, p] = jnp.concatenate([ry[i][:, PAIR:], gh[i][:, PAIR:]], axis=0)
        return carry

    lax.fori_loop(0, n_chunks // group, group_body, 0)

    def sweep_body(c, carry):
        r0 = c * CHUNK
        s2 = []
        for p in range(N_PAIRS):
            s_hi, s_lo = _split(state_scr[p])
            s2.append(jnp.concatenate([s_hi, s_lo], axis=1))
        out = [jnp.dot(rg_scr[c, p], s2[p], preferred_element_type=F32) for p in range(N_PAIRS)]
        for p in range(N_PAIRS):
            tot = yh_scr[c, p] + out[p][:, 0:PAIR] + out[p][:, PAIR:]
            state_scr[p] = tot[PAIR:, :]
            y_scr[pl.ds(r0, CHUNK), PAIR * p:PAIR * (p + 1)] = tot[0:CHUNK, :] + tot[CHUNK:PAIR, :]
        return carry

    for c in range(n_chunks):
        sweep_body(c, 0)

    g = g_ref[...].astype(F32)
    for p in range(N_PAIRS):
        sl = slice(PAIR * p, PAIR * (p + 1))
        y = y_scr[:, sl]
        mean = _pair_sum(y, in_first_t) * (1.0 / RWKV_HEAD_DIM)
        d = y - mean
        var = _pair_sum(d * d, in_first_t) * (1.0 / RWKV_HEAD_DIM)
        yn = d * lax.rsqrt(var + GN_EPS) * lnw_ref[:, sl] + lnb_ref[:, sl]
        o_ref[:, sl] = ((yn + bonus[:, sl]) * _silu(g[:, sl])).astype(o_ref.dtype)


def _rwkv_call(proj, mu_rkv, mu_w, mu_a, w0, w_lora_b, a0, a_lora_b, k_k, k_a, r_k, lnx_w, lnx_b,
               batch, seq):
    ts = min(seq, 512)
    nt = seq // ts
    w = RWKV_WIDTH
    muwa = jnp.concatenate([mu_w, mu_a]).reshape(1, LANES)
    zl = jnp.zeros((DECAY_LORA, w), F32)
    wlb = jnp.concatenate([w_lora_b, zl], axis=0).astype(BF16)
    alb = jnp.concatenate([zl, a_lora_b], axis=0).astype(BF16)
    blk = lambda col: pl.BlockSpec((ts, PROJ_BLOCK), lambda b, i: (b * nt + i, col // PROJ_BLOCK))
    row = lambda width: pl.BlockSpec((1, width), lambda b, i: (0, 0))
    lora = pl.BlockSpec((LANES, w), lambda b, i: (0, 0))
    tile = pltpu.VMEM((ts, w), F32)
    return pl.pallas_call(
        _rwkv_kernel,
        out_shape=jax.ShapeDtypeStruct((batch * seq, w), BF16),
        grid=(batch, nt),
        in_specs=[blk(COL_R), blk(COL_K), blk(COL_V),
                  pl.BlockSpec((ts, MISC_BLOCK), lambda b, i: (b * nt + i, COL_MISC // MISC_BLOCK)),
                  blk(COL_GRWKV),
                  row(w), row(w), row(w), row(LANES), row(w), lora, row(w), lora,
                  row(w), row(w), row(w), row(w), row(w)],
        out_specs=pl.BlockSpec((ts, w), lambda b, i: (b * nt + i, 0)),
        scratch_shapes=[pltpu.VMEM((8, w), F32),
                        pltpu.VMEM((N_PAIRS, PAIR, PAIR), F32),
                        tile, tile, tile, tile, tile, tile, tile,
                        pltpu.VMEM((ts // CHUNK, N_PAIRS, 2 * PAIR, PAIR), BF16),
                        pltpu.VMEM((ts // CHUNK, N_PAIRS, 2 * PAIR, PAIR), F32)],
        compiler_params=_params(2), name="rwkv7",
    )(proj, proj, proj, proj, proj,
      mu_rkv[0:1], mu_rkv[1:2], mu_rkv[2:3], muwa, w0.reshape(1, w), wlb, a0.reshape(1, w), alb,
      k_k.reshape(1, w), k_a.reshape(1, w), r_k.reshape(1, w), lnx_w.reshape(1, w), lnx_b.reshape(1, w))


def _outproj_kernel(ym_ref, ys_ref, yr_ref, x_ref, gate_ref, w_ref, fnw_ref, o_ref, *, final):
    acc = (_dot(ym_ref[...], w_ref[0:MLA_WIDTH, :])
           + _dot(ys_ref[...], w_ref[MLA_WIDTH:MLA_WIDTH + SSD_WIDTH, :])
           + _dot(yr_ref[...], w_ref[MLA_WIDTH + SSD_WIDTH:, :]))
    xn = x_ref[...] + gate_ref[0] * acc
    o_ref[...] = _rms(xn, fnw_ref[...]) if final else xn


def _outproj_call(y_mla, y_ssd, y_rwkv, x2, gate, w_out, final_norm_w, seq, final):
    t, d = x2.shape
    tm = min(seq, 1024)
    per_batch = seq // tm
    yspec = lambda wd: pl.BlockSpec((tm, wd), lambda i: (i, 0))
    return pl.pallas_call(
        functools.partial(_outproj_kernel, final=final),
        out_shape=jax.ShapeDtypeStruct((t, d), F32),
        grid=(t // tm,),
        in_specs=[yspec(MLA_WIDTH), yspec(SSD_WIDTH), yspec(RWKV_WIDTH), yspec(d),
                  pl.BlockSpec((1, 1, d), lambda i: (i // per_batch, 0, 0)),
                  pl.BlockSpec((D_MIX, d), lambda i: (0, 0)),
                  pl.BlockSpec((1, d), lambda i: (0, 0))],
        out_specs=yspec(d),
        compiler_params=_params(1), name="out_proj",
    )(y_mla, y_ssd, y_rwkv, x2, gate, w_out.astype(BF16), final_norm_w.reshape(1, d))


def kernel(x, c, positions, ada_w, ada_b, norm_w, w_in, q_norm_w, w_uq, kv_norm_w, w_ukv, conv_w, conv_b, dt_bias, a_log, d_skip, ssd_norm_w, mu_rkv, mu_w, mu_a, w0, w_lora_b, a0, a_lora_b, k_k, k_a, r_k, lnx_w, lnx_b, w_out, final_norm_w):
    batch, seq, d = x.shape
    depth = ada_w.shape[0]
    assert d == D_MODEL and batch <= 8 and seq % CHUNK == 0
    t = batch * seq
    x2 = x.reshape(t, d)
    mod = _mod_call(jnp.pad(c, ((0, 8 - batch), (0, 0))), ada_w, ada_b)
    rope_tab = _rope_call(positions.reshape(t, 1))
    for l in range(depth):
        shift = mod[l, :batch, 0:d].reshape(batch, 1, d)
        scale = mod[l, :batch, d:2 * d].reshape(batch, 1, d)
        gate = mod[l, :batch, 2 * d:].reshape(batch, 1, d)
        proj = _inproj_call(x2, scale, shift, norm_w[l], _pack_w_in(w_in[l]), seq)
        q, k, v = _mla_prep_call(proj, rope_tab, q_norm_w[l], w_uq[l], kv_norm_w[l], w_ukv[l], batch, seq)
        y_mla = _attn_call(q, k, v, proj, batch, seq)
        y_ssd = _ssd_call(proj, conv_w[l], conv_b[l], dt_bias[l], a_log[l], d_skip[l], ssd_norm_w[l],
                          batch, seq)
        y_rwkv = _rwkv_call(proj, mu_rkv[l], mu_w[l], mu_a[l], w0[l], w_lora_b[l], a0[l], a_lora_b[l],
                            k_k[l], k_a[l], r_k[l], lnx_w[l], lnx_b[l], batch, seq)
        x2 = _outproj_call(y_mla, y_ssd, y_rwkv, x2, gate, w_out[l], final_norm_w, seq,
                           final=(l == depth - 1))
    return x2.reshape(batch, seq, d)
```

```python
import functools

import jax
import jax.numpy as jnp
import numpy as np
from jax import lax
from jax.experimental import pallas as pl
from jax.experimental.pallas import tpu as pltpu

F32 = jnp.float32
BF16 = jnp.bfloat16

D_MODEL = 1024
CHUNK = 64
CHUNK_SHIFT = 6
NORM_EPS = 1e-6

MLA_HEADS = 4
MLA_Q_RANK = 256
MLA_KV_RANK = 128
MLA_NOPE = 128
MLA_ROPE = 64
MLA_V = 128
MLA_WIDTH = MLA_HEADS * MLA_V
MLA_QK_PAD = 256
MLA_VT_ROWS = 144
ATTN_Q_SUB = 256
ATTN_HEADS_PER_ITER = 2
LOG2_E = 1.4426950408889634
ROPE_THETA = 10000.0

SSD_HEADS = 8
SSD_HEAD_DIM = 64
SSD_WIDTH = SSD_HEADS * SSD_HEAD_DIM
SSD_GROUPS = 2
SSD_STATE = 128
SSD_CONV = 4
SSD_XBC = SSD_WIDTH + 2 * SSD_GROUPS * SSD_STATE

RWKV_HEADS = 8
RWKV_HEAD_DIM = 64
RWKV_WIDTH = RWKV_HEADS * RWKV_HEAD_DIM
DECAY_LORA = 64
ICLR_LORA = 64
DECAY_SCALE = 0.606531
GN_EPS = 64e-5

D_MIX = MLA_WIDTH + SSD_WIDTH + RWKV_WIDTH

LANES = 128
PAIR = 2 * RWKV_HEAD_DIM
N_PAIRS = RWKV_HEADS // 2
RWKV_GROUP_CHUNKS = 4
SSD_GROUP_CHUNKS = 8

COL_XBC = 0
COL_GMLA = 1024
COL_Z = 1536
COL_R = 2048
COL_K = 2560
COL_V = 3072
COL_GRWKV = 3584
COL_LAT = 4096
COL_MISC = 4608
MISC_BLOCK = 256
PROJ_COLS = COL_MISC + MISC_BLOCK
PROJ_BLOCK = 512

VMEM_LIMIT_BYTES = 56 * 1024 * 1024


def _params(n_axes):
    return pltpu.CompilerParams(dimension_semantics=("arbitrary",) * n_axes,
                                vmem_limit_bytes=VMEM_LIMIT_BYTES)


def _dot(a, b):
    return jnp.dot(a.astype(BF16), b.astype(BF16), preferred_element_type=F32)


def _dot_nt(a, b):
    return lax.dot_general(a.astype(BF16), b.astype(BF16), (((1,), (1,)), ((), ())),
                           preferred_element_type=F32)


def _dot_tn(a, b):
    return lax.dot_general(a.astype(BF16), b.astype(BF16), (((0,), (0,)), ((), ())),
                           preferred_element_type=F32)


def _split(x):
    hi = x.astype(BF16)
    lo = (x - hi.astype(F32)).astype(BF16)
    return hi, lo


def _dot_exact_rhs(a, b_bf16):
    hi, lo = _split(a)
    return (jnp.dot(hi, b_bf16, preferred_element_type=F32)
            + jnp.dot(lo, b_bf16, preferred_element_type=F32))


def _dot_exact_lhs(a_bf16, b):
    hi, lo = _split(b)
    return (jnp.dot(a_bf16, hi, preferred_element_type=F32)
            + jnp.dot(a_bf16, lo, preferred_element_type=F32))


def _sigmoid(x):
    return 0.5 * jnp.tanh(0.5 * x) + 0.5


def _silu(x):
    h = 0.5 * x
    return h + h * jnp.tanh(h)


def _iota(shape, dim):
    return lax.broadcasted_iota(jnp.int32, shape, dim)


def _mod_kernel(c_ref, w_ref, b_ref, o_ref):
    c = c_ref[...]
    ch, cl = _split(_silu(c))
    wh, wl = _split(w_ref[0])
    acc = (jnp.dot(ch, wh, preferred_element_type=F32)
           + jnp.dot(cl, wh, preferred_element_type=F32)
           + jnp.dot(ch, wl, preferred_element_type=F32))
    o_ref[0] = acc + b_ref[0]


def _mod_call(c8, ada_w, ada_b):
    depth, d, d3 = ada_w.shape
    tn = 512
    return pl.pallas_call(
        _mod_kernel,
        out_shape=jax.ShapeDtypeStruct((depth, 8, d3), F32),
        grid=(depth, d3 // tn),
        in_specs=[pl.BlockSpec((8, d), lambda l, j: (0, 0)),
                  pl.BlockSpec((1, d, tn), lambda l, j: (l, 0, j)),
                  pl.BlockSpec((1, 1, tn), lambda l, j: (l, 0, j))],
        out_specs=pl.BlockSpec((1, 8, tn), lambda l, j: (l, 0, j)),
        compiler_params=_params(2), name="adaln_mod",
    )(c8, ada_w, ada_b.reshape(depth, 1, d3))


def _rope_kernel(pos_ref, freq_ref, sign_ref, o_ref):
    ang = pos_ref[...].astype(F32) * freq_ref[...]
    o_ref[:, 0:MLA_ROPE] = jnp.cos(ang)
    o_ref[:, MLA_ROPE:2 * MLA_ROPE] = jnp.sin(ang) * sign_ref[...]


def _rope_call(pos_col):
    t = pos_col.shape[0]
    tr = min(t, 1024)
    half = MLA_ROPE // 2
    inv_freq = ROPE_THETA ** (-jnp.arange(half, dtype=F32) / half)
    freq = jnp.concatenate([inv_freq, inv_freq]).reshape(1, MLA_ROPE)
    sign = jnp.concatenate([-jnp.ones((half,), F32), jnp.ones((half,), F32)]).reshape(1, MLA_ROPE)
    return pl.pallas_call(
        _rope_kernel,
        out_shape=jax.ShapeDtypeStruct((t, 2 * MLA_ROPE), F32),
        grid=(t // tr,),
        in_specs=[pl.BlockSpec((tr, 1), lambda i: (i, 0)),
                  pl.BlockSpec((1, MLA_ROPE), lambda i: (0, 0)),
                  pl.BlockSpec((1, MLA_ROPE), lambda i: (0, 0))],
        out_specs=pl.BlockSpec((tr, 2 * MLA_ROPE), lambda i: (i, 0)),
        compiler_params=_params(1), name="rope_table",
    )(pos_col, freq, sign)


def _inproj_kernel(x_ref, scale_ref, shift_ref, nw_ref, w_ref, o_ref):
    x = x_ref[...]
    ms = jnp.mean(x * x, axis=-1, keepdims=True)
    y = x * lax.rsqrt(ms + NORM_EPS) * nw_ref[...]
    h = (y * (1.0 + scale_ref[0]) + shift_ref[0]).astype(BF16)
    o_ref[...] = jnp.dot(h, w_ref[...], preferred_element_type=F32).astype(o_ref.dtype)


def _inproj_call(x2, scale, shift, norm_w, wp, seq):
    t, d = x2.shape
    tm = min(seq, 512)
    per_batch = seq // tm
    return pl.pallas_call(
        _inproj_kernel,
        out_shape=jax.ShapeDtypeStruct((t, PROJ_COLS), BF16),
        grid=(t // tm,),
        in_specs=[pl.BlockSpec((tm, d), lambda i: (i, 0)),
                  pl.BlockSpec((1, 1, d), lambda i: (i // per_batch, 0, 0)),
                  pl.BlockSpec((1, 1, d), lambda i: (i // per_batch, 0, 0)),
                  pl.BlockSpec((1, d), lambda i: (0, 0)),
                  pl.BlockSpec((d, PROJ_COLS), lambda i: (0, 0))],
        out_specs=pl.BlockSpec((tm, PROJ_COLS), lambda i: (i, 0)),
        compiler_params=_params(1), name="in_proj",
    )(x2, scale, shift, norm_w.reshape(1, d), wp)


def _pack_w_in(w):
    o = np.cumsum((0, MLA_Q_RANK, MLA_KV_RANK, MLA_ROPE, MLA_WIDTH, SSD_WIDTH, SSD_XBC, SSD_HEADS,
                   RWKV_WIDTH, RWKV_WIDTH, RWKV_WIDTH, DECAY_LORA, ICLR_LORA, RWKV_WIDTH))
    q_lat, kv_lat, k_pe, g_mla, z, xbc, dt, r, k, v, w_lo, a_lo, g_rwkv = (
        w[..., o[i]:o[i + 1]] for i in range(13))
    half = MLA_ROPE // 2
    misc_pad = jnp.zeros(w.shape[:-1] + (MISC_BLOCK - DECAY_LORA - ICLR_LORA - SSD_HEADS,), w.dtype)
    return jnp.concatenate([xbc, g_mla, z, r, k, v, g_rwkv, q_lat, kv_lat, k_pe, k_pe[..., half:],
                            k_pe[..., :half], w_lo, a_lo, dt, misc_pad], axis=-1).astype(BF16)


def _rms(x, w):
    ms = jnp.mean(x * x, axis=-1, keepdims=True)
    return x * lax.rsqrt(ms + NORM_EPS) * w


def _mla_prep_kernel(lat_ref, rope_ref, qnw_ref, wuq_ref, kvnw_ref, wuk_ref, wuvt_ref, q_ref, k_ref, v_ref):
    lat = lat_ref[...].astype(F32)
    cs = rope_ref[...]
    q = _dot(_rms(lat[:, 0:MLA_Q_RANK], qnw_ref[...]), wuq_ref[...])
    kvn = _rms(lat[:, MLA_Q_RANK:MLA_Q_RANK + MLA_KV_RANK], kvnw_ref[...]).astype(BF16)
    k_nope = jnp.dot(kvn, wuk_ref[...], preferred_element_type=F32)
    v_t = _dot_nt(wuvt_ref[...], kvn)
    kp = lat[:, MLA_Q_RANK + MLA_KV_RANK:] * cs
    k_pe = (kp[:, 0:MLA_ROPE] + kp[:, MLA_ROPE:]).astype(BF16)
    rows = lat.shape[0]
    zpad = jnp.zeros((rows, MLA_QK_PAD - MLA_NOPE - MLA_ROPE), BF16)
    scale = (MLA_NOPE + MLA_ROPE) ** -0.5 * LOG2_E
    ones_row = (_iota((MLA_VT_ROWS - MLA_V, rows), 0) == 0).astype(BF16)
    for h in range(MLA_HEADS):
        qh = q[:, 256 * h:256 * (h + 1)]
        qp = qh[:, MLA_NOPE:] * cs
        q_pe = qp[:, 0:MLA_ROPE] + qp[:, MLA_ROPE:]
        q_ref[0, h] = jnp.concatenate([(qh[:, 0:MLA_NOPE] * scale).astype(BF16),
                                       (q_pe * scale).astype(BF16), zpad], axis=1)
        k_ref[0, h] = jnp.concatenate([k_nope[:, MLA_NOPE * h:MLA_NOPE * (h + 1)].astype(BF16), k_pe, zpad],
                                      axis=1)
        v_ref[0, h, 0:MLA_V, :] = v_t[MLA_V * h:MLA_V * (h + 1), :].astype(BF16)
        v_ref[0, h, MLA_V:, :] = ones_row


def _split_w_ukv(w_ukv):
    w = w_ukv.astype(BF16).reshape(MLA_KV_RANK, MLA_HEADS, MLA_NOPE + MLA_V)
    w_k = w[:, :, :MLA_NOPE].reshape(MLA_KV_RANK, MLA_HEADS * MLA_NOPE)
    w_vt = w[:, :, MLA_NOPE:].reshape(MLA_KV_RANK, MLA_HEADS * MLA_V).T
    return w_k, w_vt


def _pack_w_uq(w_uq):
    half = MLA_ROPE // 2
    per = MLA_NOPE + MLA_ROPE
    cols = []
    for h in range(MLA_HEADS):
        wh = w_uq[:, per * h:per * (h + 1)]
        pe = wh[:, MLA_NOPE:]
        cols += [wh, pe[:, half:], pe[:, :half]]
    return jnp.concatenate(cols, axis=1).astype(BF16)


def _mla_prep_call(proj, rope_tab, q_norm_w, w_uq, kv_norm_w, w_ukv, batch, seq):
    tm = min(seq, 1024)
    per_batch = seq // tm
    lat_blk = COL_LAT // PROJ_BLOCK
    qk_shape = jax.ShapeDtypeStruct((batch, MLA_HEADS, seq, MLA_QK_PAD), BF16)
    qk_spec = pl.BlockSpec((1, MLA_HEADS, tm, MLA_QK_PAD), lambda i: (i // per_batch, 0, i % per_batch, 0))
    return pl.pallas_call(
        _mla_prep_kernel,
        out_shape=(qk_shape, qk_shape, jax.ShapeDtypeStruct((batch, MLA_HEADS, MLA_VT_ROWS, seq), BF16)),
        grid=(batch * per_batch,),
        in_specs=[pl.BlockSpec((tm, PROJ_BLOCK), lambda i: (i, lat_blk)),
                  pl.BlockSpec((tm, 2 * MLA_ROPE), lambda i: (i, 0)),
                  pl.BlockSpec((1, MLA_Q_RANK), lambda i: (0, 0)),
                  pl.BlockSpec((MLA_Q_RANK, 256 * MLA_HEADS), lambda i: (0, 0)),
                  pl.BlockSpec((1, MLA_KV_RANK), lambda i: (0, 0)),
                  pl.BlockSpec((MLA_KV_RANK, MLA_NOPE * MLA_HEADS), lambda i: (0, 0)),
                  pl.BlockSpec((MLA_V * MLA_HEADS, MLA_KV_RANK), lambda i: (0, 0))],
        out_specs=(qk_spec, qk_spec,
                   pl.BlockSpec((1, MLA_HEADS, MLA_VT_ROWS, tm), lambda i: (i // per_batch, 0, 0, i % per_batch))),
        compiler_params=_params(1), name="mla_prep",
    )(proj, rope_tab, q_norm_w.reshape(1, -1), _pack_w_uq(w_uq), kv_norm_w.reshape(1, -1),
      *_split_w_ukv(w_ukv))


def _attn_kernel(qi_tab, ki_tab, q_ref, k_ref, vt_ref, g_ref, o_ref, m_scr, acc_scr):
    p = pl.program_id(1)
    qi = qi_tab[p]
    ki = ki_tab[p]
    tq = q_ref.shape[2]
    sub = min(ATTN_Q_SUB, tq)

    @pl.when(ki == 0)
    def _():
        m_scr[...] = jnp.full(m_scr.shape, -jnp.inf, F32)
        acc_scr[...] = jnp.zeros(acc_scr.shape, F32)

    def step(diagonal):
        def head_body(hi, carry):
            chains = [(hi * ATTN_HEADS_PER_ITER + hh, j) for hh in range(ATTN_HEADS_PER_ITER)
                      for j in range(tq // sub)]
            ids = range(len(chains))
            head = [h for h, _ in chains]
            cols = [slice(sub * j, sub * (j + 1)) for _, j in chains]
            n_keys = [sub * (j + 1) if diagonal else tq for _, j in chains]
            s = [lax.dot_general(k_ref[0, head[i], 0:n_keys[i], :], q_ref[0, head[i], cols[i], :],
                                 (((1,), (1,)), ((), ())), preferred_element_type=F32) for i in ids]
            if diagonal:
                for i in ids:
                    key = lax.shift_right_logical(_iota(s[i].shape, 0), CHUNK_SHIFT)
                    qry = lax.shift_right_logical(_iota(s[i].shape, 1) + sub * chains[i][1], CHUNK_SHIFT)
                    s[i] = jnp.where(key <= qry, s[i], -jnp.inf)
            m_prev = [m_scr[head[i], 0:1, cols[i]] for i in ids]
            m_new = [jnp.maximum(m_prev[i], jnp.max(s[i], axis=0, keepdims=True)) for i in ids]
            pm = [jnp.exp2(s[i] - m_new[i]).astype(BF16) for i in ids]
            pv = [jnp.dot(vt_ref[0, head[i], :, 0:n_keys[i]], pm[i], preferred_element_type=F32)
                  for i in ids]
            for i in ids:
                acc_scr[head[i], :, cols[i]] = (jnp.exp2(m_prev[i] - m_new[i]) * acc_scr[head[i], :, cols[i]]
                                                + pv[i])
                m_scr[head[i], 0:1, cols[i]] = m_new[i]
            return carry

        lax.fori_loop(0, MLA_HEADS // ATTN_HEADS_PER_ITER, head_body, 0)

    @pl.when(ki < qi)
    def _():
        step(False)

    @pl.when(ki == qi)
    def _():
        step(True)
        g = g_ref[...].astype(F32)
        for h in range(MLA_HEADS):
            acc = acc_scr[h]
            o = (acc[0:MLA_V, :] / acc[MLA_V:MLA_V + 1, :]).T
            o_ref[:, MLA_V * h:MLA_V * (h + 1)] = (o * _silu(g[:, MLA_V * h:MLA_V * (h + 1)])).astype(o_ref.dtype)


def _attn_call(q, k, vt, proj, batch, seq):
    tq = min(seq, 1024)
    nq = seq // tq
    pairs = [(a, b) for a in range(nq) for b in range(a + 1)]
    qi_tab = jnp.asarray([a for a, _ in pairs], jnp.int32)
    ki_tab = jnp.asarray([b for _, b in pairs], jnp.int32)
    grid_spec = pltpu.PrefetchScalarGridSpec(
        num_scalar_prefetch=2,
        grid=(batch, len(pairs)),
        in_specs=[pl.BlockSpec((1, MLA_HEADS, tq, MLA_QK_PAD), lambda b, p, qt, kt: (b, 0, qt[p], 0)),
                  pl.BlockSpec((1, MLA_HEADS, tq, MLA_QK_PAD), lambda b, p, qt, kt: (b, 0, kt[p], 0)),
                  pl.BlockSpec((1, MLA_HEADS, MLA_VT_ROWS, tq), lambda b, p, qt, kt: (b, 0, 0, kt[p])),
                  pl.BlockSpec((tq, MLA_WIDTH), lambda b, p, qt, kt: (b * nq + qt[p], COL_GMLA // MLA_WIDTH))],
        out_specs=pl.BlockSpec((tq, MLA_WIDTH), lambda b, p, qt, kt: (b * nq + qt[p], 0)),
        scratch_shapes=[pltpu.VMEM((MLA_HEADS, 8, tq), F32),
                        pltpu.VMEM((MLA_HEADS, MLA_VT_ROWS, tq), F32)],
    )
    return pl.pallas_call(
        _attn_kernel,
        out_shape=jax.ShapeDtypeStruct((batch * seq, MLA_WIDTH), BF16),
        grid_spec=grid_spec,
        compiler_params=_params(2), name="mla_attn",
    )(qi_tab, ki_tab, q, k, vt, proj)


def _tri_incl_bf16():
    return (_iota((CHUNK, CHUNK), 0) >= _iota((CHUNK, CHUNK), 1)).astype(BF16)


SSD_HALO = 8
N_SSD_IN = 10
N_RWKV_IN = 18


def _ssd_init(xpad_scr, state_scr):
    xpad_scr[0:SSD_HALO, :] = jnp.zeros((SSD_HALO, SSD_XBC), F32)
    state_scr[...] = jnp.zeros(state_scr.shape, F32)


def _rwkv_init(prev_scr, state_scr):
    prev_scr[...] = jnp.zeros(prev_scr.shape, F32)
    state_scr[...] = jnp.zeros(state_scr.shape, F32)


def _ssd_kernel(*refs):
    @pl.when(pl.program_id(1) == 0)
    def _():
        _ssd_init(refs[N_SSD_IN + 1], refs[N_SSD_IN + 2])

    _ssd_body(*refs)


def _rwkv_kernel(*refs):
    @pl.when(pl.program_id(1) == 0)
    def _():
        _rwkv_init(refs[N_RWKV_IN + 1], refs[N_RWKV_IN + 2])

    _rwkv_body(*refs)


def _ssd_body(xbc_ref, z_ref, misc_ref, convw_ref, convb_ref, dtb_ref, alog_ref, dskip_ref, nw_ref,
              expand_ref, o_ref, xpad_scr, state_scr, xc_scr, xdt_scr, dta_scr, y_scr, inc_scr, dec_scr,
              pre_scr):
    ts = xbc_ref.shape[0]
    halo = SSD_HALO
    xpad_scr[halo:halo + ts, :] = xbc_ref[...].astype(F32)
    conv = convb_ref[...]
    for i in range(SSD_CONV):
        start = halo - (SSD_CONV - 1) + i
        conv = conv + convw_ref[i:i + 1, :] * xpad_scr[start:start + ts, :]
    xpad_scr[0:halo, :] = xpad_scr[ts:ts + halo, :]
    xc_scr[...] = _silu(conv)

    dt = jax.nn.softplus(misc_ref[:, LANES:2 * LANES].astype(F32) + dtb_ref[...])
    dt_x = _dot_exact_rhs(dt, expand_ref[...])
    xdt_scr[...] = xc_scr[:, 0:SSD_WIDTH] * dt_x
    dta_scr[...] = dt_x * (-jnp.exp(alog_ref[...]))

    tri = _tri_incl_bf16()
    lane = _iota((CHUNK, PAIR), 1)
    row = _iota((CHUNK, PAIR), 0)
    in_first = lane < SSD_HEAD_DIM
    src = jnp.where(in_first, lane, lane - SSD_HEAD_DIM)
    diag2 = row == src
    causal2 = row >= src
    gw = SSD_WIDTH // SSD_GROUPS

    n_chunks = ts // CHUNK
    group = SSD_GROUP_CHUNKS if n_chunks % SSD_GROUP_CHUNKS == 0 else 1
    b_cols = [slice(SSD_WIDTH + SSD_STATE * g, SSD_WIDTH + SSD_STATE * (g + 1)) for g in range(SSD_GROUPS)]
    c_cols = [slice(SSD_WIDTH + SSD_STATE * (SSD_GROUPS + g), SSD_WIDTH + SSD_STATE * (SSD_GROUPS + g + 1))
              for g in range(SSD_GROUPS)]
    g_lanes = [slice(gw * g, gw * (g + 1)) for g in range(SSD_GROUPS)]

    def group_body(gi, carry):
        cs = range(group)
        rows = [pl.ds(pl.multiple_of((gi * group + ci) * CHUNK, CHUNK), CHUNK) for ci in cs]
        cum = [_dot_exact_lhs(tri, dta_scr[rows[ci], :]) for ci in cs]
        xdt = [xdt_scr[rows[ci], :] for ci in cs]
        bm = [[xc_scr[rows[ci], b_cols[g]] for g in range(SSD_GROUPS)] for ci in cs]
        cm = [[xc_scr[rows[ci], c_cols[g]] for g in range(SSD_GROUPS)] for ci in cs]
        tot = [cum[ci][CHUNK - 1:CHUNK, :] for ci in cs]
        xw = [xdt[ci] * jnp.exp(tot[ci] - cum[ci]) for ci in cs]
        cb2 = [[_dot_nt(cm[ci][g], jnp.concatenate([bm[ci][g], bm[ci][g]], axis=0))
                for g in range(SSD_GROUPS)] for ci in cs]
        inc = [[_dot_tn(bm[ci][g], xw[ci][:, g_lanes[g]]) for g in range(SSD_GROUPS)] for ci in cs]
        lhs, rhs = [], []
        for ci in cs:
            for p in range(N_PAIRS):
                pl_ = slice(PAIR * p, PAIR * (p + 1))
                colp = cum[ci][:, pl_]
                rowp = jnp.sum(jnp.where(diag2, colp, 0.0), axis=0, keepdims=True)
                dec = jnp.exp(jnp.where(causal2, colp - rowp, -jnp.inf))
                lhs.append(cb2[ci][p // 2] * dec)
                rhs.append(_stack_heads(xdt[ci][:, pl_], in_first))
        y_diag = [_dot(lhs[i], rhs[i]) for i in range(len(lhs))]
        for ci in cs:
            c = gi * group + ci
            for p in range(N_PAIRS):
                y_scr[rows[ci], PAIR * p:PAIR * (p + 1)] = y_diag[ci * N_PAIRS + p]
            for g in range(SSD_GROUPS):
                inc_scr[c, :, g_lanes[g]] = inc[ci][g]
            dec_scr[c] = jnp.broadcast_to(jnp.exp(tot[ci]), (8, SSD_WIDTH))
            dta_scr[rows[ci], :] = jnp.exp(cum[ci])
        return carry

    lax.fori_loop(0, n_chunks // group, group_body, 0)

    state = state_scr[...]
    for c in range(n_chunks):
        pre_scr[c] = state.astype(BF16)
        state = state * dec_scr[c, 0:1, :] + inc_scr[c]
    state_scr[...] = state

    def off_body(gi, carry):
        cs = range(group)
        rows = [pl.ds(pl.multiple_of((gi * group + ci) * CHUNK, CHUNK), CHUNK) for ci in cs]
        y_off = [[jnp.dot(xc_scr[rows[ci], c_cols[g]].astype(BF16), pre_scr[gi * group + ci, :, g_lanes[g]],
                          preferred_element_type=F32) for g in range(SSD_GROUPS)] for ci in cs]
        for ci in cs:
            for g in range(SSD_GROUPS):
                y_scr[rows[ci], g_lanes[g]] = (y_scr[rows[ci], g_lanes[g]]
                                               + y_off[ci][g] * dta_scr[rows[ci], g_lanes[g]])
        return carry

    lax.fori_loop(0, n_chunks // group, off_body, 0)

    y = y_scr[...] + dskip_ref[...] * xc_scr[:, 0:SSD_WIDTH]
    o_ref[...] = _rms(y * _silu(z_ref[...].astype(F32)), nw_ref[...]).astype(o_ref.dtype)


def _head_expand(vec, width):
    return jnp.repeat(vec.astype(F32), width).reshape(1, -1)


def _ssd_call(proj, conv_w, conv_b, dt_bias, a_log, d_skip, ssd_norm_w, batch, seq):
    ts = min(seq, 512)
    nt = seq // ts
    dtb = jnp.zeros((1, LANES), F32).at[0, :SSD_HEADS].set(dt_bias)
    expand = (jnp.arange(LANES)[:, None] == (jnp.arange(SSD_WIDTH) // SSD_HEAD_DIM)[None, :]).astype(BF16)
    row = lambda w: pl.BlockSpec((1, w), lambda b, i: (0, 0))
    return pl.pallas_call(
        _ssd_kernel,
        out_shape=jax.ShapeDtypeStruct((batch * seq, SSD_WIDTH), BF16),
        grid=(batch, nt),
        in_specs=[pl.BlockSpec((ts, SSD_XBC), lambda b, i: (b * nt + i, COL_XBC // SSD_XBC)),
                  pl.BlockSpec((ts, PROJ_BLOCK), lambda b, i: (b * nt + i, COL_Z // PROJ_BLOCK)),
                  pl.BlockSpec((ts, MISC_BLOCK), lambda b, i: (b * nt + i, COL_MISC // MISC_BLOCK)),
                  pl.BlockSpec((SSD_CONV, SSD_XBC), lambda b, i: (0, 0)),
                  row(SSD_XBC), row(LANES), row(SSD_WIDTH), row(SSD_WIDTH), row(SSD_WIDTH),
                  pl.BlockSpec((LANES, SSD_WIDTH), lambda b, i: (0, 0))],
        out_specs=pl.BlockSpec((ts, SSD_WIDTH), lambda b, i: (b * nt + i, 0)),
        scratch_shapes=[pltpu.VMEM((ts + SSD_HALO, SSD_XBC), F32),
                        pltpu.VMEM((SSD_STATE, SSD_WIDTH), F32),
                        pltpu.VMEM((ts, SSD_XBC), F32),
                        pltpu.VMEM((ts, SSD_WIDTH), F32),
                        pltpu.VMEM((ts, SSD_WIDTH), F32),
                        pltpu.VMEM((ts, SSD_WIDTH), F32),
                        pltpu.VMEM((ts // CHUNK, SSD_STATE, SSD_WIDTH), F32),
                        pltpu.VMEM((ts // CHUNK, 8, SSD_WIDTH), F32),
                        pltpu.VMEM((ts // CHUNK, SSD_STATE, SSD_WIDTH), BF16)],
        compiler_params=_params(2), name="ssd",
    )(proj, proj, proj, conv_w, conv_b.reshape(1, -1), dtb, _head_expand(a_log, SSD_HEAD_DIM),
      _head_expand(d_skip, SSD_HEAD_DIM), ssd_norm_w.reshape(1, -1), expand)


def _pair_sum(x, in_first):
    s0 = jnp.sum(jnp.where(in_first, x, 0.0), axis=-1, keepdims=True)
    s1 = jnp.sum(jnp.where(in_first, 0.0, x), axis=-1, keepdims=True)
    return jnp.where(in_first, s0, s1)


def _stack_heads(x, in_first):
    return jnp.concatenate([jnp.where(in_first, x, 0.0), jnp.where(in_first, 0.0, x)], axis=0)


def _rwkv_body(r_ref, k_ref, v_ref, misc_ref, g_ref, mur_ref, muk_ref, muv_ref, muwa_ref, w0_ref,
               wlb_ref, a0_ref, alb_ref, kk_ref, ka_ref, rk_ref, lnw_ref, lnb_ref, o_ref,
               prev_scr, state_scr, r_scr, lw_scr, k_scr, v_scr, a_scr, b_scr, y_scr, rg_scr, yh_scr):
    ts = r_ref.shape[0]
    first_row = _iota((8, 1), 0) == 0

    def shift(x, mu, slot):
        w = x.shape[1]
        rolled = pltpu.roll(x, 1, 0)
        head = jnp.where(first_row, prev_scr[slot:slot + 1, 0:w], rolled[0:8, :])
        prev = jnp.concatenate([head, rolled[8:, :]], axis=0)
        prev_scr[slot:slot + 1, 0:w] = x[ts - 1:ts, :]
        return x + (prev - x) * mu

    r = shift(r_ref[...].astype(F32), mur_ref[...], 0)
    k = shift(k_ref[...].astype(F32), muk_ref[...], 1)
    v = shift(v_ref[...].astype(F32), muv_ref[...], 2)
    wa = shift(misc_ref[:, 0:LANES].astype(F32), muwa_ref[...], 3)

    lw_scr[...] = -DECAY_SCALE * _sigmoid(w0_ref[...] + _dot(jnp.tanh(wa), wlb_ref[...]))
    a = _sigmoid(a0_ref[...] + _dot(wa, alb_ref[...]))

    in_first_t = _iota((ts, PAIR), 1) < RWKV_HEAD_DIM
    kk = k * kk_ref[...]
    k2 = k * (1.0 + (a - 1.0) * ka_ref[...])
    rkr = r * k2 * rk_ref[...]
    for p in range(N_PAIRS):
        sl = slice(PAIR * p, PAIR * (p + 1))
        kkp = kk[:, sl]
        kkn = kkp * lax.rsqrt(jnp.maximum(_pair_sum(kkp * kkp, in_first_t), 1e-24))
        a_scr[:, sl] = -kkn
        b_scr[:, sl] = kkn * a[:, sl]
        y_scr[:, sl] = _pair_sum(rkr[:, sl], in_first_t)
    bonus = y_scr[...] * v
    r_scr[...] = r
    k_scr[...] = k2
    v_scr[...] = v

    tri = _tri_incl_bf16()
    in_first = _iota((CHUNK, PAIR), 1) < RWKV_HEAD_DIM
    rr = _iota((PAIR, PAIR), 0)
    cc = _iota((PAIR, PAIR), 1)
    same = lax.shift_right_logical(rr, CHUNK_SHIFT) == lax.shift_right_logical(cc, CHUNK_SHIFT)
    strict = same & (rr > cc)
    incl = same & (rr >= cc)
    eye_b = rr == cc
    eye = eye_b.astype(F32)
    zeros_pp = jnp.zeros((PAIR, PAIR), BF16)
    n_chunks = ts // CHUNK
    group = RWKV_GROUP_CHUNKS if n_chunks % RWKV_GROUP_CHUNKS == 0 else 1

    def group_body(gi, carry):
        atm, rtm, vm, bkh, bke, wend = [], [], [], [], [], []
        for ci in range(group):
            rows = pl.ds((gi * group + ci) * CHUNK, CHUNK)
            lw = lw_scr[rows, :]
            cum = _dot_exact_lhs(tri, lw)
            tot = cum[CHUNK - 1:CHUNK, :]
            e_neg = jnp.exp(-cum)
            e_end = jnp.exp(tot - cum)
            w_end = jnp.exp(tot)
            rc, kc, vc, ac, bc = (r_scr[rows, :], k_scr[rows, :], v_scr[rows, :], a_scr[rows, :],
                                  b_scr[rows, :])
            a_t = ac * jnp.exp(cum - lw)
            r_t = rc * jnp.exp(cum)
            b_h = bc * e_neg
            k_h = kc * e_neg
            b_e = bc * e_end
            k_e = kc * e_end
            for p in range(N_PAIRS):
                sl = slice(PAIR * p, PAIR * (p + 1))
                st = lambda x: _stack_heads(x[:, sl], in_first)
                atm.append(st(a_t))
                rtm.append(st(r_t))
                vm.append(st(vc).astype(BF16))
                bkh.append(jnp.concatenate([st(b_h), st(k_h)], axis=0).astype(BF16))
                bke.append(jnp.concatenate([st(b_e), st(k_e)], axis=0).astype(BF16))
                wend.append(w_end[:, sl])
        ids = range(len(atm))
        amat = [_dot_nt(jnp.concatenate([atm[i], rtm[i]], axis=0), bkh[i]) for i in ids]
        n_ab = [jnp.where(strict, amat[i][0:PAIR, 0:PAIR], 0.0) for i in ids]
        a_ak = [jnp.where(strict, amat[i][0:PAIR, PAIR:], 0.0) for i in ids]
        a_r = [jnp.concatenate([jnp.where(incl, amat[i][PAIR:, 0:PAIR], 0.0),
                                jnp.where(incl, amat[i][PAIR:, PAIR:], 0.0)], axis=1).astype(BF16)
               for i in ids]
        akv = [_dot(a_ak[i], vm[i]) for i in ids]
        npow = [_dot(n_ab[i], n_ab[i]) for i in ids]
        tinv = [eye + n_ab[i] for i in ids]
        for _ in range(CHUNK_SHIFT - 2):
            both = [_dot(npow[i], jnp.concatenate([npow[i], tinv[i]], axis=1)) for i in ids]
            npow = [both[i][:, 0:PAIR] for i in ids]
            tinv = [tinv[i] + both[i][:, PAIR:] for i in ids]
        last = [_dot(npow[i], tinv[i]) for i in ids]
        tinv = [tinv[i] + last[i] for i in ids]
        au = [_dot(tinv[i], jnp.concatenate([atm[i], akv[i]], axis=1)) for i in ids]
        rhs = [jnp.concatenate([au[i].astype(BF16), jnp.concatenate([zeros_pp, vm[i]], axis=1)], axis=0)
               for i in ids]
        ry = [jnp.dot(a_r[i], rhs[i], preferred_element_type=F32) for i in ids]
        gh = [_dot_tn(bke[i], rhs[i]) for i in ids]
        for i in ids:
            c = gi * group + i // N_PAIRS
            p = i % N_PAIRS
            gmat = gh[i][:, 0:PAIR] + jnp.where(eye_b, wend[i], 0.0)
            rg_scr[c, p] = jnp.concatenate([rtm[i] + ry[i][:, 0:PAIR], gmat], axis=0).astype(BF16)
            yh_scr[c, p] = jnp.concatenate([ry[i][:, PAIR:], gh[i][:, PAIR:]], axis=0)
        return carry

    for gi in range(n_chunks // group):
        group_body(gi, 0)

    def sweep_body(c, carry):
        r0 = c * CHUNK
        s2 = []
        for p in range(N_PAIRS):
            s_hi, s_lo = _split(state_scr[p])
            s2.append(jnp.concatenate([s_hi, s_lo], axis=1))
        out = [jnp.dot(rg_scr[c, p], s2[p], preferred_element_type=F32) for p in range(N_PAIRS)]
        for p in range(N_PAIRS):
            tot = yh_scr[c, p] + out[p][:, 0:PAIR] + out[p][:, PAIR:]
            state_scr[p] = tot[PAIR:, :]
            y_scr[pl.ds(r0, CHUNK), PAIR * p:PAIR * (p + 1)] = tot[0:CHUNK, :] + tot[CHUNK:PAIR, :]
        return carry

    for c in range(n_chunks):
        sweep_body(c, 0)

    g = g_ref[...].astype(F32)
    for p in range(N_PAIRS):
        sl = slice(PAIR * p, PAIR * (p + 1))
        y = y_scr[:, sl]
        mean = _pair_sum(y, in_first_t) * (1.0 / RWKV_HEAD_DIM)
        d = y - mean
        var = _pair_sum(d * d, in_first_t) * (1.0 / RWKV_HEAD_DIM)
        yn = d * lax.rsqrt(var + GN_EPS) * lnw_ref[:, sl] + lnb_ref[:, sl]
        o_ref[:, sl] = ((yn + bonus[:, sl]) * _silu(g[:, sl])).astype(o_ref.dtype)


def _rwkv_call(proj, mu_rkv, mu_w, mu_a, w0, w_lora_b, a0, a_lora_b, k_k, k_a, r_k, lnx_w, lnx_b,
               batch, seq):
    ts = min(seq, 512)
    nt = seq // ts
    w = RWKV_WIDTH
    muwa = jnp.concatenate([mu_w, mu_a]).reshape(1, LANES)
    zl = jnp.zeros((DECAY_LORA, w), F32)
    wlb = jnp.concatenate([w_lora_b, zl], axis=0).astype(BF16)
    alb = jnp.concatenate([zl, a_lora_b], axis=0).astype(BF16)
    blk = lambda col: pl.BlockSpec((ts, PROJ_BLOCK), lambda b, i: (b * nt + i, col // PROJ_BLOCK))
    row = lambda width: pl.BlockSpec((1, width), lambda b, i: (0, 0))
    lora = pl.BlockSpec((LANES, w), lambda b, i: (0, 0))
    tile = pltpu.VMEM((ts, w), F32)
    return pl.pallas_call(
        _rwkv_kernel,
        out_shape=jax.ShapeDtypeStruct((batch * seq, w), BF16),
        grid=(batch, nt),
        in_specs=[blk(COL_R), blk(COL_K), blk(COL_V),
                  pl.BlockSpec((ts, MISC_BLOCK), lambda b, i: (b * nt + i, COL_MISC // MISC_BLOCK)),
                  blk(COL_GRWKV),
                  row(w), row(w), row(w), row(LANES), row(w), lora, row(w), lora,
                  row(w), row(w), row(w), row(w), row(w)],
        out_specs=pl.BlockSpec((ts, w), lambda b, i: (b * nt + i, 0)),
        scratch_shapes=[pltpu.VMEM((8, w), F32),
                        pltpu.VMEM((N_PAIRS, PAIR, PAIR), F32),
                        tile, tile, tile, tile, tile, tile, tile,
                        pltpu.VMEM((ts // CHUNK, N_PAIRS, 2 * PAIR, PAIR), BF16),
                        pltpu.VMEM((ts // CHUNK, N_PAIRS, 2 * PAIR, PAIR), F32)],
        compiler_params=_params(2), name="rwkv7",
    )(proj, proj, proj, proj, proj,
      mu_rkv[0:1], mu_rkv[1:2], mu_rkv[2:3], muwa, w0.reshape(1, w), wlb, a0.reshape(1, w), alb,
      k_k.reshape(1, w), k_a.reshape(1, w), r_k.reshape(1, w), lnx_w.reshape(1, w), lnx_b.reshape(1, w))


def _outproj_kernel(ym_ref, ys_ref, yr_ref, x_ref, gate_ref, w_ref, fnw_ref, o_ref, *, final):
    acc = (_dot(ym_ref[...], w_ref[0:MLA_WIDTH, :])
           + _dot(ys_ref[...], w_ref[MLA_WIDTH:MLA_WIDTH + SSD_WIDTH, :])
           + _dot(yr_ref[...], w_ref[MLA_WIDTH + SSD_WIDTH:, :]))
    xn = x_ref[...] + gate_ref[0] * acc
    o_ref[...] = _rms(xn, fnw_ref[...]) if final else xn


def _outproj_call(y_mla, y_ssd, y_rwkv, x2, gate, w_out, final_norm_w, seq, final):
    t, d = x2.shape
    tm = min(seq, 1024)
    per_batch = seq // tm
    yspec = lambda wd: pl.BlockSpec((tm, wd), lambda i: (i, 0))
    return pl.pallas_call(
        functools.partial(_outproj_kernel, final=final),
        out_shape=jax.ShapeDtypeStruct((t, d), F32),
        grid=(t // tm,),
        in_specs=[yspec(MLA_WIDTH), yspec(SSD_WIDTH), yspec(RWKV_WIDTH), yspec(d),
                  pl.BlockSpec((1, 1, d), lambda i: (i // per_batch, 0, 0)),
                  pl.BlockSpec((D_MIX, d), lambda i: (0, 0)),
                  pl.BlockSpec((1, d), lambda i: (0, 0))],
        out_specs=yspec(d),
        compiler_params=_params(1), name="out_proj",
    )(y_mla, y_ssd, y_rwkv, x2, gate, w_out.astype(BF16), final_norm_w.reshape(1, d))


def kernel(x, c, positions, ada_w, ada_b, norm_w, w_in, q_norm_w, w_uq, kv_norm_w, w_ukv, conv_w, conv_b, dt_bias, a_log, d_skip, ssd_norm_w, mu_rkv, mu_w, mu_a, w0, w_lora_b, a0, a_lora_b, k_k, k_a, r_k, lnx_w, lnx_b, w_out, final_norm_w):
    batch, seq, d = x.shape
    depth = ada_w.shape[0]
    assert d == D_MODEL and batch <= 8 and seq % CHUNK == 0
    t = batch * seq
    x2 = x.reshape(t, d)
    mod = _mod_call(jnp.pad(c, ((0, 8 - batch), (0, 0))), ada_w, ada_b)
    rope_tab = _rope_call(positions.reshape(t, 1))
    for l in range(depth):
        shift = mod[l, :batch, 0:d].reshape(batch, 1, d)
        scale = mod[l, :batch, d:2 * d].reshape(batch, 1, d)
        gate = mod[l, :batch, 2 * d:].reshape(batch, 1, d)
        proj = _inproj_call(x2, scale, shift, norm_w[l], _pack_w_in(w_in[l]), seq)
        q, k, v = _mla_prep_call(proj, rope_tab, q_norm_w[l], w_uq[l], kv_norm_w[l], w_ukv[l], batch, seq)
        y_mla = _attn_call(q, k, v, proj, batch, seq)
        y_ssd = _ssd_call(proj, conv_w[l], conv_b[l], dt_bias[l], a_log[l], d_skip[l], ssd_norm_w[l],
                          batch, seq)
        y_rwkv = _rwkv_call(proj, mu_rkv[l], mu_w[l], mu_a[l], w0[l], w_lora_b[l], a0[l], a_lora_b[l],
                            k_k[l], k_a[l], r_k[l], lnx_w[l], lnx_b[l], batch, seq)
        x2 = _outproj_call(y_mla, y_ssd, y_rwkv, x2, gate, w_out[l], final_norm_w, seq,
                           final=(l == depth - 1))
    return x2.reshape(batch, seq, d)
```

```python
import functools

import jax
import jax.numpy as jnp
import numpy as np
from jax import lax
from jax.experimental import pallas as pl
from jax.experimental.pallas import tpu as pltpu

F32 = jnp.float32
BF16 = jnp.bfloat16

D_MODEL = 1024
CHUNK = 64
CHUNK_SHIFT = 6
NORM_EPS = 1e-6

MLA_HEADS = 4
MLA_Q_RANK = 256
MLA_KV_RANK = 128
MLA_NOPE = 128
MLA_ROPE = 64
MLA_V = 128
MLA_WIDTH = MLA_HEADS * MLA_V
MLA_QK_PAD = 256
MLA_VT_ROWS = 144
ATTN_Q_SUB = 256
ATTN_HEADS_PER_ITER = 4
LOG2_E = 1.4426950408889634
ROPE_THETA = 10000.0

SSD_HEADS = 8
SSD_HEAD_DIM = 64
SSD_WIDTH = SSD_HEADS * SSD_HEAD_DIM
SSD_GROUPS = 2
SSD_STATE = 128
SSD_CONV = 4
SSD_XBC = SSD_WIDTH + 2 * SSD_GROUPS * SSD_STATE

RWKV_HEADS = 8
RWKV_HEAD_DIM = 64
RWKV_WIDTH = RWKV_HEADS * RWKV_HEAD_DIM
DECAY_LORA = 64
ICLR_LORA = 64
DECAY_SCALE = 0.606531
GN_EPS = 64e-5

D_MIX = MLA_WIDTH + SSD_WIDTH + RWKV_WIDTH

LANES = 128
PAIR = 2 * RWKV_HEAD_DIM
N_PAIRS = RWKV_HEADS // 2
RWKV_GROUP_CHUNKS = 4
SSD_GROUP_CHUNKS = 8

COL_XBC = 0
COL_GMLA = 1024
COL_Z = 1536
COL_R = 2048
COL_K = 2560
COL_V = 3072
COL_GRWKV = 3584
COL_LAT = 4096
COL_MISC = 4608
MISC_BLOCK = 256
PROJ_COLS = COL_MISC + MISC_BLOCK
PROJ_BLOCK = 512

VMEM_LIMIT_BYTES = 56 * 1024 * 1024


def _params(n_axes):
    return pltpu.CompilerParams(dimension_semantics=("arbitrary",) * n_axes,
                                vmem_limit_bytes=VMEM_LIMIT_BYTES)


def _dot(a, b):
    return jnp.dot(a.astype(BF16), b.astype(BF16), preferred_element_type=F32)


def _dot_nt(a, b):
    return lax.dot_general(a.astype(BF16), b.astype(BF16), (((1,), (1,)), ((), ())),
                           preferred_element_type=F32)


def _dot_tn(a, b):
    return lax.dot_general(a.astype(BF16), b.astype(BF16), (((0,), (0,)), ((), ())),
                           preferred_element_type=F32)


def _split(x):
    hi = x.astype(BF16)
    lo = (x - hi.astype(F32)).astype(BF16)
    return hi, lo


def _dot_exact_rhs(a, b_bf16):
    hi, lo = _split(a)
    return (jnp.dot(hi, b_bf16, preferred_element_type=F32)
            + jnp.dot(lo, b_bf16, preferred_element_type=F32))


def _dot_exact_lhs(a_bf16, b):
    hi, lo = _split(b)
    return (jnp.dot(a_bf16, hi, preferred_element_type=F32)
            + jnp.dot(a_bf16, lo, preferred_element_type=F32))


def _sigmoid(x):
    return 0.5 * jnp.tanh(0.5 * x) + 0.5


def _silu(x):
    h = 0.5 * x
    return h + h * jnp.tanh(h)


def _iota(shape, dim):
    return lax.broadcasted_iota(jnp.int32, shape, dim)


def _mod_kernel(c_ref, w_ref, b_ref, o_ref):
    c = c_ref[...]
    ch, cl = _split(_silu(c))
    wh, wl = _split(w_ref[0])
    acc = (jnp.dot(ch, wh, preferred_element_type=F32)
           + jnp.dot(cl, wh, preferred_element_type=F32)
           + jnp.dot(ch, wl, preferred_element_type=F32))
    o_ref[0] = acc + b_ref[0]


def _mod_call(c8, ada_w, ada_b):
    depth, d, d3 = ada_w.shape
    tn = 512
    return pl.pallas_call(
        _mod_kernel,
        out_shape=jax.ShapeDtypeStruct((depth, 8, d3), F32),
        grid=(depth, d3 // tn),
        in_specs=[pl.BlockSpec((8, d), lambda l, j: (0, 0)),
                  pl.BlockSpec((1, d, tn), lambda l, j: (l, 0, j)),
                  pl.BlockSpec((1, 1, tn), lambda l, j: (l, 0, j))],
        out_specs=pl.BlockSpec((1, 8, tn), lambda l, j: (l, 0, j)),
        compiler_params=_params(2), name="adaln_mod",
    )(c8, ada_w, ada_b.reshape(depth, 1, d3))


def _rope_kernel(pos_ref, freq_ref, sign_ref, o_ref):
    ang = pos_ref[...].astype(F32) * freq_ref[...]
    o_ref[:, 0:MLA_ROPE] = jnp.cos(ang)
    o_ref[:, MLA_ROPE:2 * MLA_ROPE] = jnp.sin(ang) * sign_ref[...]


def _rope_call(pos_col):
    t = pos_col.shape[0]
    tr = min(t, 1024)
    half = MLA_ROPE // 2
    inv_freq = ROPE_THETA ** (-jnp.arange(half, dtype=F32) / half)
    freq = jnp.concatenate([inv_freq, inv_freq]).reshape(1, MLA_ROPE)
    sign = jnp.concatenate([-jnp.ones((half,), F32), jnp.ones((half,), F32)]).reshape(1, MLA_ROPE)
    return pl.pallas_call(
        _rope_kernel,
        out_shape=jax.ShapeDtypeStruct((t, 2 * MLA_ROPE), F32),
        grid=(t // tr,),
        in_specs=[pl.BlockSpec((tr, 1), lambda i: (i, 0)),
                  pl.BlockSpec((1, MLA_ROPE), lambda i: (0, 0)),
                  pl.BlockSpec((1, MLA_ROPE), lambda i: (0, 0))],
        out_specs=pl.BlockSpec((tr, 2 * MLA_ROPE), lambda i: (i, 0)),
        compiler_params=_params(1), name="rope_table",
    )(pos_col, freq, sign)


def _inproj_kernel(x_ref, scale_ref, shift_ref, nw_ref, w_ref, o_ref):
    x = x_ref[...]
    ms = jnp.mean(x * x, axis=-1, keepdims=True)
    y = x * lax.rsqrt(ms + NORM_EPS) * nw_ref[...]
    h = (y * (1.0 + scale_ref[0]) + shift_ref[0]).astype(BF16)
    o_ref[...] = jnp.dot(h, w_ref[...], preferred_element_type=F32).astype(o_ref.dtype)


def _inproj_call(x2, scale, shift, norm_w, wp, seq):
    t, d = x2.shape
    tm = min(seq, 512)
    per_batch = seq // tm
    return pl.pallas_call(
        _inproj_kernel,
        out_shape=jax.ShapeDtypeStruct((t, PROJ_COLS), BF16),
        grid=(t // tm,),
        in_specs=[pl.BlockSpec((tm, d), lambda i: (i, 0)),
                  pl.BlockSpec((1, 1, d), lambda i: (i // per_batch, 0, 0)),
                  pl.BlockSpec((1, 1, d), lambda i: (i // per_batch, 0, 0)),
                  pl.BlockSpec((1, d), lambda i: (0, 0)),
                  pl.BlockSpec((d, PROJ_COLS), lambda i: (0, 0))],
        out_specs=pl.BlockSpec((tm, PROJ_COLS), lambda i: (i, 0)),
        compiler_params=_params(1), name="in_proj",
    )(x2, scale, shift, norm_w.reshape(1, d), wp)


def _pack_w_in(w):
    o = np.cumsum((0, MLA_Q_RANK, MLA_KV_RANK, MLA_ROPE, MLA_WIDTH, SSD_WIDTH, SSD_XBC, SSD_HEADS,
                   RWKV_WIDTH, RWKV_WIDTH, RWKV_WIDTH, DECAY_LORA, ICLR_LORA, RWKV_WIDTH))
    q_lat, kv_lat, k_pe, g_mla, z, xbc, dt, r, k, v, w_lo, a_lo, g_rwkv = (
        w[..., o[i]:o[i + 1]] for i in range(13))
    half = MLA_ROPE // 2
    misc_pad = jnp.zeros(w.shape[:-1] + (MISC_BLOCK - DECAY_LORA - ICLR_LORA - SSD_HEADS,), w.dtype)
    return jnp.concatenate([xbc, g_mla, z, r, k, v, g_rwkv, q_lat, kv_lat, k_pe, k_pe[..., half:],
                            k_pe[..., :half], w_lo, a_lo, dt, misc_pad], axis=-1).astype(BF16)


def _rms(x, w):
    ms = jnp.mean(x * x, axis=-1, keepdims=True)
    return x * lax.rsqrt(ms + NORM_EPS) * w


def _mla_prep_kernel(lat_ref, rope_ref, qnw_ref, wuq_ref, kvnw_ref, wuk_ref, wuvt_ref, q_ref, k_ref, v_ref):
    lat = lat_ref[...].astype(F32)
    cs = rope_ref[...]
    q = _dot(_rms(lat[:, 0:MLA_Q_RANK], qnw_ref[...]), wuq_ref[...])
    kvn = _rms(lat[:, MLA_Q_RANK:MLA_Q_RANK + MLA_KV_RANK], kvnw_ref[...]).astype(BF16)
    k_nope = jnp.dot(kvn, wuk_ref[...], preferred_element_type=F32)
    v_t = _dot_nt(wuvt_ref[...], kvn)
    kp = lat[:, MLA_Q_RANK + MLA_KV_RANK:] * cs
    k_pe = (kp[:, 0:MLA_ROPE] + kp[:, MLA_ROPE:]).astype(BF16)
    rows = lat.shape[0]
    zpad = jnp.zeros((rows, MLA_QK_PAD - MLA_NOPE - MLA_ROPE), BF16)
    scale = (MLA_NOPE + MLA_ROPE) ** -0.5 * LOG2_E
    ones_row = (_iota((MLA_VT_ROWS - MLA_V, rows), 0) == 0).astype(BF16)
    for h in range(MLA_HEADS):
        qh = q[:, 256 * h:256 * (h + 1)]
        qp = qh[:, MLA_NOPE:] * cs
        q_pe = qp[:, 0:MLA_ROPE] + qp[:, MLA_ROPE:]
        q_ref[0, h] = jnp.concatenate([(qh[:, 0:MLA_NOPE] * scale).astype(BF16),
                                       (q_pe * scale).astype(BF16), zpad], axis=1)
        k_ref[0, h] = jnp.concatenate([k_nope[:, MLA_NOPE * h:MLA_NOPE * (h + 1)].astype(BF16), k_pe, zpad],
                                      axis=1)
        v_ref[0, h, 0:MLA_V, :] = v_t[MLA_V * h:MLA_V * (h + 1), :].astype(BF16)
        v_ref[0, h, MLA_V:, :] = ones_row


def _split_w_ukv(w_ukv):
    w = w_ukv.astype(BF16).reshape(MLA_KV_RANK, MLA_HEADS, MLA_NOPE + MLA_V)
    w_k = w[:, :, :MLA_NOPE].reshape(MLA_KV_RANK, MLA_HEADS * MLA_NOPE)
    w_vt = w[:, :, MLA_NOPE:].reshape(MLA_KV_RANK, MLA_HEADS * MLA_V).T
    return w_k, w_vt


def _pack_w_uq(w_uq):
    half = MLA_ROPE // 2
    per = MLA_NOPE + MLA_ROPE
    cols = []
    for h in range(MLA_HEADS):
        wh = w_uq[:, per * h:per * (h + 1)]
        pe = wh[:, MLA_NOPE:]
        cols += [wh, pe[:, half:], pe[:, :half]]
    return jnp.concatenate(cols, axis=1).astype(BF16)


def _mla_prep_call(proj, rope_tab, q_norm_w, w_uq, kv_norm_w, w_ukv, batch, seq):
    tm = min(seq, 1024)
    per_batch = seq // tm
    lat_blk = COL_LAT // PROJ_BLOCK
    qk_shape = jax.ShapeDtypeStruct((batch, MLA_HEADS, seq, MLA_QK_PAD), BF16)
    qk_spec = pl.BlockSpec((1, MLA_HEADS, tm, MLA_QK_PAD), lambda i: (i // per_batch, 0, i % per_batch, 0))
    return pl.pallas_call(
        _mla_prep_kernel,
        out_shape=(qk_shape, qk_shape, jax.ShapeDtypeStruct((batch, MLA_HEADS, MLA_VT_ROWS, seq), BF16)),
        grid=(batch * per_batch,),
        in_specs=[pl.BlockSpec((tm, PROJ_BLOCK), lambda i: (i, lat_blk)),
                  pl.BlockSpec((tm, 2 * MLA_ROPE), lambda i: (i, 0)),
                  pl.BlockSpec((1, MLA_Q_RANK), lambda i: (0, 0)),
                  pl.BlockSpec((MLA_Q_RANK, 256 * MLA_HEADS), lambda i: (0, 0)),
                  pl.BlockSpec((1, MLA_KV_RANK), lambda i: (0, 0)),
                  pl.BlockSpec((MLA_KV_RANK, MLA_NOPE * MLA_HEADS), lambda i: (0, 0)),
                  pl.BlockSpec((MLA_V * MLA_HEADS, MLA_KV_RANK), lambda i: (0, 0))],
        out_specs=(qk_spec, qk_spec,
                   pl.BlockSpec((1, MLA_HEADS, MLA_VT_ROWS, tm), lambda i: (i // per_batch, 0, 0, i % per_batch))),
        compiler_params=_params(1), name="mla_prep",
    )(proj, rope_tab, q_norm_w.reshape(1, -1), _pack_w_uq(w_uq), kv_norm_w.reshape(1, -1),
      *_split_w_ukv(w_ukv))


def _attn_kernel(qi_tab, ki_tab, q_ref, k_ref, vt_ref, g_ref, o_ref, m_scr, acc_scr):
    p = pl.program_id(1)
    qi = qi_tab[p]
    ki = ki_tab[p]
    tq = q_ref.shape[2]
    sub = min(ATTN_Q_SUB, tq)

    @pl.when(ki == 0)
    def _():
        m_scr[...] = jnp.full(m_scr.shape, -jnp.inf, F32)
        acc_scr[...] = jnp.zeros(acc_scr.shape, F32)

    def step(diagonal):
        def head_body(hi, carry):
            chains = [(hi * ATTN_HEADS_PER_ITER + hh, j) for hh in range(ATTN_HEADS_PER_ITER)
                      for j in range(tq // sub)]
            ids = range(len(chains))
            head = [h for h, _ in chains]
            cols = [slice(sub * j, sub * (j + 1)) for _, j in chains]
            n_keys = [sub * (j + 1) if diagonal else tq for _, j in chains]
            s = [lax.dot_general(k_ref[0, head[i], 0:n_keys[i], :], q_ref[0, head[i], cols[i], :],
                                 (((1,), (1,)), ((), ())), preferred_element_type=F32) for i in ids]
            if diagonal:
                for i in ids:
                    key = lax.shift_right_logical(_iota(s[i].shape, 0), CHUNK_SHIFT)
                    qry = lax.shift_right_logical(_iota(s[i].shape, 1) + sub * chains[i][1], CHUNK_SHIFT)
                    s[i] = jnp.where(key <= qry, s[i], -jnp.inf)
            m_prev = [m_scr[head[i], 0:1, cols[i]] for i in ids]
            m_new = [jnp.maximum(m_prev[i], jnp.max(s[i], axis=0, keepdims=True)) for i in ids]
            pm = [jnp.exp2(s[i] - m_new[i]).astype(BF16) for i in ids]
            pv = [jnp.dot(vt_ref[0, head[i], :, 0:n_keys[i]], pm[i], preferred_element_type=F32)
                  for i in ids]
            for i in ids:
                acc_scr[head[i], :, cols[i]] = (jnp.exp2(m_prev[i] - m_new[i]) * acc_scr[head[i], :, cols[i]]
                                                + pv[i])
                m_scr[head[i], 0:1, cols[i]] = m_new[i]
            return carry

        lax.fori_loop(0, MLA_HEADS // ATTN_HEADS_PER_ITER, head_body, 0)

    @pl.when(ki < qi)
    def _():
        step(False)

    @pl.when(ki == qi)
    def _():
        step(True)
        g = g_ref[...].astype(F32)
        for h in range(MLA_HEADS):
            acc = acc_scr[h]
            o = (acc[0:MLA_V, :] / acc[MLA_V:MLA_V + 1, :]).T
            o_ref[:, MLA_V * h:MLA_V * (h + 1)] = (o * _silu(g[:, MLA_V * h:MLA_V * (h + 1)])).astype(o_ref.dtype)


def _attn_call(q, k, vt, proj, batch, seq):
    tq = min(seq, 1024)
    nq = seq // tq
    pairs = [(a, b) for a in range(nq) for b in range(a + 1)]
    qi_tab = jnp.asarray([a for a, _ in pairs], jnp.int32)
    ki_tab = jnp.asarray([b for _, b in pairs], jnp.int32)
    grid_spec = pltpu.PrefetchScalarGridSpec(
        num_scalar_prefetch=2,
        grid=(batch, len(pairs)),
        in_specs=[pl.BlockSpec((1, MLA_HEADS, tq, MLA_QK_PAD), lambda b, p, qt, kt: (b, 0, qt[p], 0)),
                  pl.BlockSpec((1, MLA_HEADS, tq, MLA_QK_PAD), lambda b, p, qt, kt: (b, 0, kt[p], 0)),
                  pl.BlockSpec((1, MLA_HEADS, MLA_VT_ROWS, tq), lambda b, p, qt, kt: (b, 0, 0, kt[p])),
                  pl.BlockSpec((tq, MLA_WIDTH), lambda b, p, qt, kt: (b * nq + qt[p], COL_GMLA // MLA_WIDTH))],
        out_specs=pl.BlockSpec((tq, MLA_WIDTH), lambda b, p, qt, kt: (b * nq + qt[p], 0)),
        scratch_shapes=[pltpu.VMEM((MLA_HEADS, 8, tq), F32),
                        pltpu.VMEM((MLA_HEADS, MLA_VT_ROWS, tq), F32)],
    )
    return pl.pallas_call(
        _attn_kernel,
        out_shape=jax.ShapeDtypeStruct((batch * seq, MLA_WIDTH), BF16),
        grid_spec=grid_spec,
        compiler_params=_params(2), name="mla_attn",
    )(qi_tab, ki_tab, q, k, vt, proj)


def _tri_incl_bf16():
    return (_iota((CHUNK, CHUNK), 0) >= _iota((CHUNK, CHUNK), 1)).astype(BF16)


SSD_HALO = 8
N_SSD_IN = 10
N_RWKV_IN = 18


def _ssd_init(xpad_scr, state_scr):
    xpad_scr[0:SSD_HALO, :] = jnp.zeros((SSD_HALO, SSD_XBC), F32)
    state_scr[...] = jnp.zeros(state_scr.shape, F32)


def _rwkv_init(prev_scr, state_scr):
    prev_scr[...] = jnp.zeros(prev_scr.shape, F32)
    state_scr[...] = jnp.zeros(state_scr.shape, F32)


def _ssd_kernel(*refs):
    @pl.when(pl.program_id(1) == 0)
    def _():
        _ssd_init(refs[N_SSD_IN + 1], refs[N_SSD_IN + 2])

    _ssd_body(*refs)


def _rwkv_kernel(*refs):
    @pl.when(pl.program_id(1) == 0)
    def _():
        _rwkv_init(refs[N_RWKV_IN + 1], refs[N_RWKV_IN + 2])

    _rwkv_body(*refs)


def _ssd_body(xbc_ref, z_ref, misc_ref, convw_ref, convb_ref, dtb_ref, alog_ref, dskip_ref, nw_ref,
              expand_ref, o_ref, xpad_scr, state_scr, xc_scr, xdt_scr, dta_scr, y_scr, inc_scr, dec_scr,
              pre_scr):
    ts = xbc_ref.shape[0]
    halo = SSD_HALO
    xpad_scr[halo:halo + ts, :] = xbc_ref[...].astype(F32)
    conv = convb_ref[...]
    for i in range(SSD_CONV):
        start = halo - (SSD_CONV - 1) + i
        conv = conv + convw_ref[i:i + 1, :] * xpad_scr[start:start + ts, :]
    xpad_scr[0:halo, :] = xpad_scr[ts:ts + halo, :]
    xc_scr[...] = _silu(conv)

    dt = jax.nn.softplus(misc_ref[:, LANES:2 * LANES].astype(F32) + dtb_ref[...])
    dt_x = _dot_exact_rhs(dt, expand_ref[...])
    xdt_scr[...] = xc_scr[:, 0:SSD_WIDTH] * dt_x
    dta_scr[...] = dt_x * (-jnp.exp(alog_ref[...]))

    tri = _tri_incl_bf16()
    lane = _iota((CHUNK, PAIR), 1)
    row = _iota((CHUNK, PAIR), 0)
    in_first = lane < SSD_HEAD_DIM
    src = jnp.where(in_first, lane, lane - SSD_HEAD_DIM)
    diag2 = row == src
    causal2 = row >= src
    gw = SSD_WIDTH // SSD_GROUPS

    n_chunks = ts // CHUNK
    group = SSD_GROUP_CHUNKS if n_chunks % SSD_GROUP_CHUNKS == 0 else 1
    b_cols = [slice(SSD_WIDTH + SSD_STATE * g, SSD_WIDTH + SSD_STATE * (g + 1)) for g in range(SSD_GROUPS)]
    c_cols = [slice(SSD_WIDTH + SSD_STATE * (SSD_GROUPS + g), SSD_WIDTH + SSD_STATE * (SSD_GROUPS + g + 1))
              for g in range(SSD_GROUPS)]
    g_lanes = [slice(gw * g, gw * (g + 1)) for g in range(SSD_GROUPS)]

    def group_body(gi, carry):
        cs = range(group)
        rows = [pl.ds(pl.multiple_of((gi * group + ci) * CHUNK, CHUNK), CHUNK) for ci in cs]
        cum = [_dot_exact_lhs(tri, dta_scr[rows[ci], :]) for ci in cs]
        xdt = [xdt_scr[rows[ci], :] for ci in cs]
        bm = [[xc_scr[rows[ci], b_cols[g]] for g in range(SSD_GROUPS)] for ci in cs]
        cm = [[xc_scr[rows[ci], c_cols[g]] for g in range(SSD_GROUPS)] for ci in cs]
        tot = [cum[ci][CHUNK - 1:CHUNK, :] for ci in cs]
        xw = [xdt[ci] * jnp.exp(tot[ci] - cum[ci]) for ci in cs]
        cb2 = [[_dot_nt(cm[ci][g], jnp.concatenate([bm[ci][g], bm[ci][g]], axis=0))
                for g in range(SSD_GROUPS)] for ci in cs]
        inc = [[_dot_tn(bm[ci][g], xw[ci][:, g_lanes[g]]) for g in range(SSD_GROUPS)] for ci in cs]
        lhs, rhs = [], []
        for ci in cs:
            for p in range(N_PAIRS):
                pl_ = slice(PAIR * p, PAIR * (p + 1))
                colp = cum[ci][:, pl_]
                rowp = jnp.sum(jnp.where(diag2, colp, 0.0), axis=0, keepdims=True)
                dec = jnp.exp(jnp.where(causal2, colp - rowp, -jnp.inf))
                lhs.append(cb2[ci][p // 2] * dec)
                rhs.append(_stack_heads(xdt[ci][:, pl_], in_first))
        y_diag = [_dot(lhs[i], rhs[i]) for i in range(len(lhs))]
        for ci in cs:
            c = gi * group + ci
            for p in range(N_PAIRS):
                y_scr[rows[ci], PAIR * p:PAIR * (p + 1)] = y_diag[ci * N_PAIRS + p]
            for g in range(SSD_GROUPS):
                inc_scr[c, :, g_lanes[g]] = inc[ci][g]
            dec_scr[c] = jnp.broadcast_to(jnp.exp(tot[ci]), (8, SSD_WIDTH))
            dta_scr[rows[ci], :] = jnp.exp(cum[ci])
        return carry

    lax.fori_loop(0, n_chunks // group, group_body, 0)

    state = state_scr[...]
    for c in range(n_chunks):
        pre_scr[c] = state.astype(BF16)
        state = state * dec_scr[c, 0:1, :] + inc_scr[c]
    state_scr[...] = state

    def off_body(gi, carry):
        cs = range(group)
        rows = [pl.ds(pl.multiple_of((gi * group + ci) * CHUNK, CHUNK), CHUNK) for ci in cs]
        y_off = [[jnp.dot(xc_scr[rows[ci], c_cols[g]].astype(BF16), pre_scr[gi * group + ci, :, g_lanes[g]],
                          preferred_element_type=F32) for g in range(SSD_GROUPS)] for ci in cs]
        for ci in cs:
            for g in range(SSD_GROUPS):
                y_scr[rows[ci], g_lanes[g]] = (y_scr[rows[ci], g_lanes[g]]
                                               + y_off[ci][g] * dta_scr[rows[ci], g_lanes[g]])
        return carry

    lax.fori_loop(0, n_chunks // group, off_body, 0)

    y = y_scr[...] + dskip_ref[...] * xc_scr[:, 0:SSD_WIDTH]
    o_ref[...] = _rms(y * _silu(z_ref[...].astype(F32)), nw_ref[...]).astype(o_ref.dtype)


def _head_expand(vec, width):
    return jnp.repeat(vec.astype(F32), width).reshape(1, -1)


def _ssd_call(proj, conv_w, conv_b, dt_bias, a_log, d_skip, ssd_norm_w, batch, seq):
    ts = min(seq, 512)
    nt = seq // ts
    dtb = jnp.zeros((1, LANES), F32).at[0, :SSD_HEADS].set(dt_bias)
    expand = (jnp.arange(LANES)[:, None] == (jnp.arange(SSD_WIDTH) // SSD_HEAD_DIM)[None, :]).astype(BF16)
    row = lambda w: pl.BlockSpec((1, w), lambda b, i: (0, 0))
    return pl.pallas_call(
        _ssd_kernel,
        out_shape=jax.ShapeDtypeStruct((batch * seq, SSD_WIDTH), BF16),
        grid=(batch, nt),
        in_specs=[pl.BlockSpec((ts, SSD_XBC), lambda b, i: (b * nt + i, COL_XBC // SSD_XBC)),
                  pl.BlockSpec((ts, PROJ_BLOCK), lambda b, i: (b * nt + i, COL_Z // PROJ_BLOCK)),
                  pl.BlockSpec((ts, MISC_BLOCK), lambda b, i: (b * nt + i, COL_MISC // MISC_BLOCK)),
                  pl.BlockSpec((SSD_CONV, SSD_XBC), lambda b, i: (0, 0)),
                  row(SSD_XBC), row(LANES), row(SSD_WIDTH), row(SSD_WIDTH), row(SSD_WIDTH),
                  pl.BlockSpec((LANES, SSD_WIDTH), lambda b, i: (0, 0))],
        out_specs=pl.BlockSpec((ts, SSD_WIDTH), lambda b, i: (b * nt + i, 0)),
        scratch_shapes=[pltpu.VMEM((ts + SSD_HALO, SSD_XBC), F32),
                        pltpu.VMEM((SSD_STATE, SSD_WIDTH), F32),
                        pltpu.VMEM((ts, SSD_XBC), F32),
                        pltpu.VMEM((ts, SSD_WIDTH), F32),
                        pltpu.VMEM((ts, SSD_WIDTH), F32),
                        pltpu.VMEM((ts, SSD_WIDTH), F32),
                        pltpu.VMEM((ts // CHUNK, SSD_STATE, SSD_WIDTH), F32),
                        pltpu.VMEM((ts // CHUNK, 8, SSD_WIDTH), F32),
                        pltpu.VMEM((ts // CHUNK, SSD_STATE, SSD_WIDTH), BF16)],
        compiler_params=_params(2), name="ssd",
    )(proj, proj, proj, conv_w, conv_b.reshape(1, -1), dtb, _head_expand(a_log, SSD_HEAD_DIM),
      _head_expand(d_skip, SSD_HEAD_DIM), ssd_norm_w.reshape(1, -1), expand)


def _pair_sum(x, in_first):
    s0 = jnp.sum(jnp.where(in_first, x, 0.0), axis=-1, keepdims=True)
    s1 = jnp.sum(jnp.where(in_first, 0.0, x), axis=-1, keepdims=True)
    return jnp.where(in_first, s0, s1)


def _stack_heads(x, in_first):
    return jnp.concatenate([jnp.where(in_first, x, 0.0), jnp.where(in_first, 0.0, x)], axis=0)


def _rwkv_body(r_ref, k_ref, v_ref, misc_ref, g_ref, mur_ref, muk_ref, muv_ref, muwa_ref, w0_ref,
               wlb_ref, a0_ref, alb_ref, kk_ref, ka_ref, rk_ref, lnw_ref, lnb_ref, o_ref,
               prev_scr, state_scr, r_scr, lw_scr, k_scr, v_scr, a_scr, b_scr, y_scr, rg_scr, yh_scr):
    ts = r_ref.shape[0]
    first_row = _iota((8, 1), 0) == 0

    def shift(x, mu, slot):
        w = x.shape[1]
        rolled = pltpu.roll(x, 1, 0)
        head = jnp.where(first_row, prev_scr[slot:slot + 1, 0:w], rolled[0:8, :])
        prev = jnp.concatenate([head, rolled[8:, :]], axis=0)
        prev_scr[slot:slot + 1, 0:w] = x[ts - 1:ts, :]
        return x + (prev - x) * mu

    r = shift(r_ref[...].astype(F32), mur_ref[...], 0)
    k = shift(k_ref[...].astype(F32), muk_ref[...], 1)
    v = shift(v_ref[...].astype(F32), muv_ref[...], 2)
    wa = shift(misc_ref[:, 0:LANES].astype(F32), muwa_ref[...], 3)

    lw_scr[...] = -DECAY_SCALE * _sigmoid(w0_ref[...] + _dot(jnp.tanh(wa), wlb_ref[...]))
    a = _sigmoid(a0_ref[...] + _dot(wa, alb_ref[...]))

    in_first_t = _iota((ts, PAIR), 1) < RWKV_HEAD_DIM
    kk = k * kk_ref[...]
    k2 = k * (1.0 + (a - 1.0) * ka_ref[...])
    rkr = r * k2 * rk_ref[...]
    for p in range(N_PAIRS):
        sl = slice(PAIR * p, PAIR * (p + 1))
        kkp = kk[:, sl]
        kkn = kkp * lax.rsqrt(jnp.maximum(_pair_sum(kkp * kkp, in_first_t), 1e-24))
        a_scr[:, sl] = -kkn
        b_scr[:, sl] = kkn * a[:, sl]
        y_scr[:, sl] = _pair_sum(rkr[:, sl], in_first_t)
    bonus = y_scr[...] * v
    r_scr[...] = r
    k_scr[...] = k2
    v_scr[...] = v

    tri = _tri_incl_bf16()
    in_first = _iota((CHUNK, PAIR), 1) < RWKV_HEAD_DIM
    rr = _iota((PAIR, PAIR), 0)
    cc = _iota((PAIR, PAIR), 1)
    same = lax.shift_right_logical(rr, CHUNK_SHIFT) == lax.shift_right_logical(cc, CHUNK_SHIFT)
    strict = same & (rr > cc)
    incl = same & (rr >= cc)
    eye_b = rr == cc
    eye = eye_b.astype(F32)
    zeros_pp = jnp.zeros((PAIR, PAIR), BF16)
    n_chunks = ts // CHUNK
    group = RWKV_GROUP_CHUNKS if n_chunks % RWKV_GROUP_CHUNKS == 0 else 1

    def group_body(gi, carry):
        atm, rtm, vm, bkh, bke, wend = [], [], [], [], [], []
        for ci in range(group):
            rows = pl.ds((gi * group + ci) * CHUNK, CHUNK)
            lw = lw_scr[rows, :]
            cum = _dot_exact_lhs(tri, lw)
            tot = cum[CHUNK - 1:CHUNK, :]
            e_neg = jnp.exp(-cum)
            e_end = jnp.exp(tot - cum)
            w_end = jnp.exp(tot)
            rc, kc, vc, ac, bc = (r_scr[rows, :], k_scr[rows, :], v_scr[rows, :], a_scr[rows, :],
                                  b_scr[rows, :])
            a_t = ac * jnp.exp(cum - lw)
            r_t = rc * jnp.exp(cum)
            b_h = bc * e_neg
            k_h = kc * e_neg
            b_e = bc * e_end
            k_e = kc * e_end
            for p in range(N_PAIRS):
                sl = slice(PAIR * p, PAIR * (p + 1))
                st = lambda x: _stack_heads(x[:, sl], in_first)
                atm.append(st(a_t))
                rtm.append(st(r_t))
                vm.append(st(vc).astype(BF16))
                bkh.append(jnp.concatenate([st(b_h), st(k_h)], axis=0).astype(BF16))
                bke.append(jnp.concatenate([st(b_e), st(k_e)], axis=0).astype(BF16))
                wend.append(w_end[:, sl])
        ids = range(len(atm))
        amat = [_dot_nt(jnp.concatenate([atm[i], rtm[i]], axis=0), bkh[i]) for i in ids]
        n_ab = [jnp.where(strict, amat[i][0:PAIR, 0:PAIR], 0.0) for i in ids]
        a_ak = [jnp.where(strict, amat[i][0:PAIR, PAIR:], 0.0) for i in ids]
        a_r = [jnp.concatenate([jnp.where(incl, amat[i][PAIR:, 0:PAIR], 0.0),
                                jnp.where(incl, amat[i][PAIR:, PAIR:], 0.0)], axis=1).astype(BF16)
               for i in ids]
        akv = [_dot(a_ak[i], vm[i]) for i in ids]
        npow = [_dot(n_ab[i], n_ab[i]) for i in ids]
        tinv = [eye + n_ab[i] for i in ids]
        for _ in range(CHUNK_SHIFT - 2):
            both = [_dot(npow[i], jnp.concatenate([npow[i], tinv[i]], axis=1)) for i in ids]
            npow = [both[i][:, 0:PAIR] for i in ids]
            tinv = [tinv[i] + both[i][:, PAIR:] for i in ids]
        last = [_dot(npow[i], tinv[i]) for i in ids]
        tinv = [tinv[i] + last[i] for i in ids]
        au = [_dot(tinv[i], jnp.concatenate([atm[i], akv[i]], axis=1)) for i in ids]
        rhs = [jnp.concatenate([au[i].astype(BF16), jnp.concatenate([zeros_pp, vm[i]], axis=1)], axis=0)
               for i in ids]
        ry = [jnp.dot(a_r[i], rhs[i], preferred_element_type=F32) for i in ids]
        gh = [_dot_tn(bke[i], rhs[i]) for i in ids]
        for i in ids:
            c = gi * group + i // N_PAIRS
            p = i % N_PAIRS
            gmat = gh[i][:, 0:PAIR] + jnp.where(eye_b, wend[i], 0.0)
            rg_scr[c, p] = jnp.concatenate([rtm[i] + ry[i][:, 0:PAIR], gmat], axis=0).astype(BF16)
            yh_scr[c, p] = jnp.concatenate([ry[i][:, PAIR:], gh[i][:, PAIR:]], axis=0)
        return carry

    for gi in range(n_chunks // group):
        group_body(gi, 0)

    def sweep_body(c, carry):
        r0 = c * CHUNK
        s2 = []
        for p in range(N_PAIRS):
            s_hi, s_lo = _split(state_scr[p])
            s2.append(jnp.concatenate([s_hi, s_lo], axis=1))
        out = [jnp.dot(rg_scr[c, p], s2[p], preferred_element_type=F32) for p in range(N_PAIRS)]
        for p in range(N_PAIRS):
            tot = yh_scr[c, p] + out[p][:, 0:PAIR] + out[p][:, PAIR:]
            state_scr[p] = tot[PAIR:, :]
            y_scr[pl.ds(r0, CHUNK), PAIR * p:PAIR * (p + 1)] = tot[0:CHUNK, :] + tot[CHUNK:PAIR, :]
        return carry

    for c in range(n_chunks):
        sweep_body(c, 0)

    g = g_ref[...].astype(F32)
    for p in range(N_PAIRS):
        sl = slice(PAIR * p, PAIR * (p + 1))
        y = y_scr[:, sl]
        mean = _pair_sum(y, in_first_t) * (1.0 / RWKV_HEAD_DIM)
        d = y - mean
        var = _pair_sum(d * d, in_first_t) * (1.0 / RWKV_HEAD_DIM)
        yn = d * lax.rsqrt(var + GN_EPS) * lnw_ref[:, sl] + lnb_ref[:, sl]
        o_ref[:, sl] = ((yn + bonus[:, sl]) * _silu(g[:, sl])).astype(o_ref.dtype)


def _rwkv_call(proj, mu_rkv, mu_w, mu_a, w0, w_lora_b, a0, a_lora_b, k_k, k_a, r_k, lnx_w, lnx_b,
               batch, seq):
    ts = min(seq, 512)
    nt = seq // ts
    w = RWKV_WIDTH
    muwa = jnp.concatenate([mu_w, mu_a]).reshape(1, LANES)
    zl = jnp.zeros((DECAY_LORA, w), F32)
    wlb = jnp.concatenate([w_lora_b, zl], axis=0).astype(BF16)
    alb = jnp.concatenate([zl, a_lora_b], axis=0).astype(BF16)
    blk = lambda col: pl.BlockSpec((ts, PROJ_BLOCK), lambda b, i: (b * nt + i, col // PROJ_BLOCK))
    row = lambda width: pl.BlockSpec((1, width), lambda b, i: (0, 0))
    lora = pl.BlockSpec((LANES, w), lambda b, i: (0, 0))
    tile = pltpu.VMEM((ts, w), F32)
    return pl.pallas_call(
        _rwkv_kernel,
        out_shape=jax.ShapeDtypeStruct((batch * seq, w), BF16),
        grid=(batch, nt),
        in_specs=[blk(COL_R), blk(COL_K), blk(COL_V),
                  pl.BlockSpec((ts, MISC_BLOCK), lambda b, i: (b * nt + i, COL_MISC // MISC_BLOCK)),
                  blk(COL_GRWKV),
                  row(w), row(w), row(w), row(LANES), row(w), lora, row(w), lora,
                  row(w), row(w), row(w), row(w), row(w)],
        out_specs=pl.BlockSpec((ts, w), lambda b, i: (b * nt + i, 0)),
        scratch_shapes=[pltpu.VMEM((8, w), F32),
                        pltpu.VMEM((N_PAIRS, PAIR, PAIR), F32),
                        tile, tile, tile, tile, tile, tile, tile,
                        pltpu.VMEM((ts // CHUNK, N_PAIRS, 2 * PAIR, PAIR), BF16),
                        pltpu.VMEM((ts // CHUNK, N_PAIRS, 2 * PAIR, PAIR), F32)],
        compiler_params=_params(2), name="rwkv7",
    )(proj, proj, proj, proj, proj,
      mu_rkv[0:1], mu_rkv[1:2], mu_rkv[2:3], muwa, w0.reshape(1, w), wlb, a0.reshape(1, w), alb,
      k_k.reshape(1, w), k_a.reshape(1, w), r_k.reshape(1, w), lnx_w.reshape(1, w), lnx_b.reshape(1, w))


def _outproj_kernel(ym_ref, ys_ref, yr_ref, x_ref, gate_ref, w_ref, fnw_ref, o_ref, *, final):
    acc = (_dot(ym_ref[...], w_ref[0:MLA_WIDTH, :])
           + _dot(ys_ref[...], w_ref[MLA_WIDTH:MLA_WIDTH + SSD_WIDTH, :])
           + _dot(yr_ref[...], w_ref[MLA_WIDTH + SSD_WIDTH:, :]))
    xn = x_ref[...] + gate_ref[0] * acc
    o_ref[...] = _rms(xn, fnw_ref[...]) if final else xn


def _outproj_call(y_mla, y_ssd, y_rwkv, x2, gate, w_out, final_norm_w, seq, final):
    t, d = x2.shape
    tm = min(seq, 1024)
    per_batch = seq // tm
    yspec = lambda wd: pl.BlockSpec((tm, wd), lambda i: (i, 0))
    return pl.pallas_call(
        functools.partial(_outproj_kernel, final=final),
        out_shape=jax.ShapeDtypeStruct((t, d), F32),
        grid=(t // tm,),
        in_specs=[yspec(MLA_WIDTH), yspec(SSD_WIDTH), yspec(RWKV_WIDTH), yspec(d),
                  pl.BlockSpec((1, 1, d), lambda i: (i // per_batch, 0, 0)),
                  pl.BlockSpec((D_MIX, d), lambda i: (0, 0)),
                  pl.BlockSpec((1, d), lambda i: (0, 0))],
        out_specs=yspec(d),
        compiler_params=_params(1), name="out_proj",
    )(y_mla, y_ssd, y_rwkv, x2, gate, w_out.astype(BF16), final_norm_w.reshape(1, d))


def kernel(x, c, positions, ada_w, ada_b, norm_w, w_in, q_norm_w, w_uq, kv_norm_w, w_ukv, conv_w, conv_b, dt_bias, a_log, d_skip, ssd_norm_w, mu_rkv, mu_w, mu_a, w0, w_lora_b, a0, a_lora_b, k_k, k_a, r_k, lnx_w, lnx_b, w_out, final_norm_w):
    batch, seq, d = x.shape
    depth = ada_w.shape[0]
    assert d == D_MODEL and batch <= 8 and seq % CHUNK == 0
    t = batch * seq
    x2 = x.reshape(t, d)
    mod = _mod_call(jnp.pad(c, ((0, 8 - batch), (0, 0))), ada_w, ada_b)
    rope_tab = _rope_call(positions.reshape(t, 1))
    for l in range(depth):
        shift = mod[l, :batch, 0:d].reshape(batch, 1, d)
        scale = mod[l, :batch, d:2 * d].reshape(batch, 1, d)
        gate = mod[l, :batch, 2 * d:].reshape(batch, 1, d)
        proj = _inproj_call(x2, scale, shift, norm_w[l], _pack_w_in(w_in[l]), seq)
        q, k, v = _mla_prep_call(proj, rope_tab, q_norm_w[l], w_uq[l], kv_norm_w[l], w_ukv[l], batch, seq)
        y_mla = _attn_call(q, k, v, proj, batch, seq)
        y_ssd = _ssd_call(proj, conv_w[l], conv_b[l], dt_bias[l], a_log[l], d_skip[l], ssd_norm_w[l],
                          batch, seq)
        y_rwkv = _rwkv_call(proj, mu_rkv[l], mu_w[l], mu_a[l], w0[l], w_lora_b[l], a0[l], a_lora_b[l],
                            k_k[l], k_a[l], r_k[l], lnx_w[l], lnx_b[l], batch, seq)
        x2 = _outproj_call(y_mla, y_ssd, y_rwkv, x2, gate, w_out[l], final_norm_w, seq,
                           final=(l == depth - 1))
    return x2.reshape(batch, seq, d)
```

```python
import functools

import jax
import jax.numpy as jnp
import numpy as np
from jax import lax
from jax.experimental import pallas as pl
from jax.experimental.pallas import tpu as pltpu

F32 = jnp.float32
BF16 = jnp.bfloat16

D_MODEL = 1024
CHUNK = 64
CHUNK_SHIFT = 6
NORM_EPS = 1e-6

MLA_HEADS = 4
MLA_Q_RANK = 256
MLA_KV_RANK = 128
MLA_NOPE = 128
MLA_ROPE = 64
MLA_V = 128
MLA_WIDTH = MLA_HEADS * MLA_V
MLA_QK_PAD = 256
MLA_VT_ROWS = MLA_V
ATTN_Q_SUB = 256
ATTN_HEADS_PER_ITER = 4
LOG2_E = 1.4426950408889634
ROPE_THETA = 10000.0

SSD_HEADS = 8
SSD_HEAD_DIM = 64
SSD_WIDTH = SSD_HEADS * SSD_HEAD_DIM
SSD_GROUPS = 2
SSD_STATE = 128
SSD_CONV = 4
SSD_XBC = SSD_WIDTH + 2 * SSD_GROUPS * SSD_STATE

RWKV_HEADS = 8
RWKV_HEAD_DIM = 64
RWKV_WIDTH = RWKV_HEADS * RWKV_HEAD_DIM
DECAY_LORA = 64
ICLR_LORA = 64
DECAY_SCALE = 0.606531
GN_EPS = 64e-5

D_MIX = MLA_WIDTH + SSD_WIDTH + RWKV_WIDTH

LANES = 128
PAIR = 2 * RWKV_HEAD_DIM
N_PAIRS = RWKV_HEADS // 2
RWKV_GROUP_CHUNKS = 4
SSD_GROUP_CHUNKS = 8

COL_XBC = 0
COL_GMLA = 1024
COL_Z = 1536
COL_R = 2048
COL_K = 2560
COL_V = 3072
COL_GRWKV = 3584
COL_LAT = 4096
COL_MISC = 4608
MISC_BLOCK = 256
PROJ_COLS = COL_MISC + MISC_BLOCK
PROJ_BLOCK = 512

VMEM_LIMIT_BYTES = 56 * 1024 * 1024


def _params(n_axes):
    return pltpu.CompilerParams(dimension_semantics=("arbitrary",) * n_axes,
                                vmem_limit_bytes=VMEM_LIMIT_BYTES)


def _dot(a, b):
    return jnp.dot(a.astype(BF16), b.astype(BF16), preferred_element_type=F32)


def _dot_nt(a, b):
    return lax.dot_general(a.astype(BF16), b.astype(BF16), (((1,), (1,)), ((), ())),
                           preferred_element_type=F32)


def _dot_tn(a, b):
    return lax.dot_general(a.astype(BF16), b.astype(BF16), (((0,), (0,)), ((), ())),
                           preferred_element_type=F32)


def _split(x):
    hi = x.astype(BF16)
    lo = (x - hi.astype(F32)).astype(BF16)
    return hi, lo


def _dot_exact_rhs(a, b_bf16):
    hi, lo = _split(a)
    return (jnp.dot(hi, b_bf16, preferred_element_type=F32)
            + jnp.dot(lo, b_bf16, preferred_element_type=F32))


def _dot_exact_lhs(a_bf16, b):
    hi, lo = _split(b)
    return (jnp.dot(a_bf16, hi, preferred_element_type=F32)
            + jnp.dot(a_bf16, lo, preferred_element_type=F32))


def _sigmoid(x):
    return 0.5 * jnp.tanh(0.5 * x) + 0.5


def _silu(x):
    h = 0.5 * x
    return h + h * jnp.tanh(h)


def _iota(shape, dim):
    return lax.broadcasted_iota(jnp.int32, shape, dim)


def _mod_kernel(c_ref, w_ref, b_ref, o_ref):
    c = c_ref[...]
    ch, cl = _split(_silu(c))
    wh, wl = _split(w_ref[0])
    acc = (jnp.dot(ch, wh, preferred_element_type=F32)
           + jnp.dot(cl, wh, preferred_element_type=F32)
           + jnp.dot(ch, wl, preferred_element_type=F32))
    o_ref[0] = acc + b_ref[0]


def _mod_call(c8, ada_w, ada_b):
    depth, d, d3 = ada_w.shape
    tn = 512
    return pl.pallas_call(
        _mod_kernel,
        out_shape=jax.ShapeDtypeStruct((depth, 8, d3), F32),
        grid=(depth, d3 // tn),
        in_specs=[pl.BlockSpec((8, d), lambda l, j: (0, 0)),
                  pl.BlockSpec((1, d, tn), lambda l, j: (l, 0, j)),
                  pl.BlockSpec((1, 1, tn), lambda l, j: (l, 0, j))],
        out_specs=pl.BlockSpec((1, 8, tn), lambda l, j: (l, 0, j)),
        compiler_params=_params(2), name="adaln_mod",
    )(c8, ada_w, ada_b.reshape(depth, 1, d3))


def _rope_kernel(pos_ref, freq_ref, sign_ref, o_ref):
    ang = pos_ref[...].astype(F32) * freq_ref[...]
    o_ref[:, 0:MLA_ROPE] = jnp.cos(ang)
    o_ref[:, MLA_ROPE:2 * MLA_ROPE] = jnp.sin(ang) * sign_ref[...]


def _rope_call(pos_col):
    t = pos_col.shape[0]
    tr = min(t, 1024)
    half = MLA_ROPE // 2
    inv_freq = ROPE_THETA ** (-jnp.arange(half, dtype=F32) / half)
    freq = jnp.concatenate([inv_freq, inv_freq]).reshape(1, MLA_ROPE)
    sign = jnp.concatenate([-jnp.ones((half,), F32), jnp.ones((half,), F32)]).reshape(1, MLA_ROPE)
    return pl.pallas_call(
        _rope_kernel,
        out_shape=jax.ShapeDtypeStruct((t, 2 * MLA_ROPE), F32),
        grid=(t // tr,),
        in_specs=[pl.BlockSpec((tr, 1), lambda i: (i, 0)),
                  pl.BlockSpec((1, MLA_ROPE), lambda i: (0, 0)),
                  pl.BlockSpec((1, MLA_ROPE), lambda i: (0, 0))],
        out_specs=pl.BlockSpec((tr, 2 * MLA_ROPE), lambda i: (i, 0)),
        compiler_params=_params(1), name="rope_table",
    )(pos_col, freq, sign)


def _inproj_kernel(x_ref, scale_ref, shift_ref, nw_ref, w_ref, o_ref):
    x = x_ref[...]
    ms = jnp.mean(x * x, axis=-1, keepdims=True)
    y = x * lax.rsqrt(ms + NORM_EPS) * nw_ref[...]
    h = (y * (1.0 + scale_ref[0]) + shift_ref[0]).astype(BF16)
    o_ref[...] = jnp.dot(h, w_ref[...], preferred_element_type=F32).astype(o_ref.dtype)


def _inproj_call(x2, scale, shift, norm_w, wp, seq):
    t, d = x2.shape
    tm = min(seq, 512)
    per_batch = seq // tm
    return pl.pallas_call(
        _inproj_kernel,
        out_shape=jax.ShapeDtypeStruct((t, PROJ_COLS), BF16),
        grid=(t // tm,),
        in_specs=[pl.BlockSpec((tm, d), lambda i: (i, 0)),
                  pl.BlockSpec((1, 1, d), lambda i: (i // per_batch, 0, 0)),
                  pl.BlockSpec((1, 1, d), lambda i: (i // per_batch, 0, 0)),
                  pl.BlockSpec((1, d), lambda i: (0, 0)),
                  pl.BlockSpec((d, PROJ_COLS), lambda i: (0, 0))],
        out_specs=pl.BlockSpec((tm, PROJ_COLS), lambda i: (i, 0)),
        compiler_params=_params(1), name="in_proj",
    )(x2, scale, shift, norm_w.reshape(1, d), wp)


def _pack_w_in(w):
    o = np.cumsum((0, MLA_Q_RANK, MLA_KV_RANK, MLA_ROPE, MLA_WIDTH, SSD_WIDTH, SSD_XBC, SSD_HEADS,
                   RWKV_WIDTH, RWKV_WIDTH, RWKV_WIDTH, DECAY_LORA, ICLR_LORA, RWKV_WIDTH))
    q_lat, kv_lat, k_pe, g_mla, z, xbc, dt, r, k, v, w_lo, a_lo, g_rwkv = (
        w[..., o[i]:o[i + 1]] for i in range(13))
    half = MLA_ROPE // 2
    misc_pad = jnp.zeros(w.shape[:-1] + (MISC_BLOCK - DECAY_LORA - ICLR_LORA - SSD_HEADS,), w.dtype)
    return jnp.concatenate([xbc, g_mla, z, r, k, v, g_rwkv, q_lat, kv_lat, k_pe, k_pe[..., half:],
                            k_pe[..., :half], w_lo, a_lo, dt, misc_pad], axis=-1).astype(BF16)


def _rms(x, w):
    ms = jnp.mean(x * x, axis=-1, keepdims=True)
    return x * lax.rsqrt(ms + NORM_EPS) * w


def _mla_prep_kernel(lat_ref, rope_ref, qnw_ref, wuq_ref, kvnw_ref, wuk_ref, wuvt_ref, q_ref, k_ref, v_ref):
    lat = lat_ref[...].astype(F32)
    cs = rope_ref[...]
    q = _dot(_rms(lat[:, 0:MLA_Q_RANK], qnw_ref[...]), wuq_ref[...])
    kvn = _rms(lat[:, MLA_Q_RANK:MLA_Q_RANK + MLA_KV_RANK], kvnw_ref[...]).astype(BF16)
    k_nope = jnp.dot(kvn, wuk_ref[...], preferred_element_type=F32)
    v_t = _dot_nt(wuvt_ref[...], kvn)
    kp = lat[:, MLA_Q_RANK + MLA_KV_RANK:] * cs
    k_pe = (kp[:, 0:MLA_ROPE] + kp[:, MLA_ROPE:]).astype(BF16)
    rows = lat.shape[0]
    zpad = jnp.zeros((rows, MLA_QK_PAD - MLA_NOPE - MLA_ROPE), BF16)
    scale = (MLA_NOPE + MLA_ROPE) ** -0.5 * LOG2_E
    for h in range(MLA_HEADS):
        qh = q[:, 256 * h:256 * (h + 1)]
        qp = qh[:, MLA_NOPE:] * cs
        q_pe = qp[:, 0:MLA_ROPE] + qp[:, MLA_ROPE:]
        q_ref[0, h] = jnp.concatenate([(qh[:, 0:MLA_NOPE] * scale).astype(BF16),
                                       (q_pe * scale).astype(BF16), zpad], axis=1)
        k_ref[0, h] = jnp.concatenate([k_nope[:, MLA_NOPE * h:MLA_NOPE * (h + 1)].astype(BF16), k_pe, zpad],
                                      axis=1)
        v_ref[0, h] = v_t[MLA_V * h:MLA_V * (h + 1), :].astype(BF16)


def _split_w_ukv(w_ukv):
    w = w_ukv.astype(BF16).reshape(MLA_KV_RANK, MLA_HEADS, MLA_NOPE + MLA_V)
    w_k = w[:, :, :MLA_NOPE].reshape(MLA_KV_RANK, MLA_HEADS * MLA_NOPE)
    w_vt = w[:, :, MLA_NOPE:].reshape(MLA_KV_RANK, MLA_HEADS * MLA_V).T
    return w_k, w_vt


def _pack_w_uq(w_uq):
    half = MLA_ROPE // 2
    per = MLA_NOPE + MLA_ROPE
    cols = []
    for h in range(MLA_HEADS):
        wh = w_uq[:, per * h:per * (h + 1)]
        pe = wh[:, MLA_NOPE:]
        cols += [wh, pe[:, half:], pe[:, :half]]
    return jnp.concatenate(cols, axis=1).astype(BF16)


def _mla_prep_call(proj, rope_tab, q_norm_w, w_uq, kv_norm_w, w_ukv, batch, seq):
    tm = min(seq, 1024)
    per_batch = seq // tm
    lat_blk = COL_LAT // PROJ_BLOCK
    qk_shape = jax.ShapeDtypeStruct((batch, MLA_HEADS, seq, MLA_QK_PAD), BF16)
    qk_spec = pl.BlockSpec((1, MLA_HEADS, tm, MLA_QK_PAD), lambda i: (i // per_batch, 0, i % per_batch, 0))
    return pl.pallas_call(
        _mla_prep_kernel,
        out_shape=(qk_shape, qk_shape, jax.ShapeDtypeStruct((batch, MLA_HEADS, MLA_VT_ROWS, seq), BF16)),
        grid=(batch * per_batch,),
        in_specs=[pl.BlockSpec((tm, PROJ_BLOCK), lambda i: (i, lat_blk)),
                  pl.BlockSpec((tm, 2 * MLA_ROPE), lambda i: (i, 0)),
                  pl.BlockSpec((1, MLA_Q_RANK), lambda i: (0, 0)),
                  pl.BlockSpec((MLA_Q_RANK, 256 * MLA_HEADS), lambda i: (0, 0)),
                  pl.BlockSpec((1, MLA_KV_RANK), lambda i: (0, 0)),
                  pl.BlockSpec((MLA_KV_RANK, MLA_NOPE * MLA_HEADS), lambda i: (0, 0)),
                  pl.BlockSpec((MLA_V * MLA_HEADS, MLA_KV_RANK), lambda i: (0, 0))],
        out_specs=(qk_spec, qk_spec,
                   pl.BlockSpec((1, MLA_HEADS, MLA_VT_ROWS, tm), lambda i: (i // per_batch, 0, 0, i % per_batch))),
        compiler_params=_params(1), name="mla_prep",
    )(proj, rope_tab, q_norm_w.reshape(1, -1), _pack_w_uq(w_uq), kv_norm_w.reshape(1, -1),
      *_split_w_ukv(w_ukv))


def _attn_kernel(qi_tab, ki_tab, q_ref, k_ref, vt_ref, g_ref, o_ref, m_scr, l_scr, acc_scr):
    p = pl.program_id(1)
    qi = qi_tab[p]
    ki = ki_tab[p]
    tq = q_ref.shape[2]
    sub = min(ATTN_Q_SUB, tq)

    @pl.when(ki == 0)
    def _():
        m_scr[...] = jnp.full(m_scr.shape, -jnp.inf, F32)
        l_scr[...] = jnp.zeros(l_scr.shape, F32)
        acc_scr[...] = jnp.zeros(acc_scr.shape, F32)

    def step(diagonal):
        def head_body(hi, carry):
            chains = [(hi * ATTN_HEADS_PER_ITER + hh, j) for hh in range(ATTN_HEADS_PER_ITER)
                      for j in range(tq // sub)]
            ids = range(len(chains))
            head = [h for h, _ in chains]
            cols = [slice(sub * j, sub * (j + 1)) for _, j in chains]
            n_keys = [sub * (j + 1) if diagonal else tq for _, j in chains]
            s = [lax.dot_general(k_ref[0, head[i], 0:n_keys[i], :], q_ref[0, head[i], cols[i], :],
                                 (((1,), (1,)), ((), ())), preferred_element_type=F32) for i in ids]
            if diagonal:
                for i in ids:
                    key = lax.shift_right_logical(_iota(s[i].shape, 0), CHUNK_SHIFT)
                    qry = lax.shift_right_logical(_iota(s[i].shape, 1) + sub * chains[i][1], CHUNK_SHIFT)
                    s[i] = jnp.where(key <= qry, s[i], -jnp.inf)
            m_prev = [m_scr[head[i], 0:1, cols[i]] for i in ids]
            m_new = [jnp.maximum(m_prev[i], jnp.max(s[i], axis=0, keepdims=True)) for i in ids]
            p32 = [jnp.exp2(s[i] - m_new[i]) for i in ids]
            psum = [jnp.sum(p32[i], axis=0, keepdims=True) for i in ids]
            pv = [jnp.dot(vt_ref[0, head[i], :, 0:n_keys[i]], p32[i].astype(BF16), preferred_element_type=F32)
                  for i in ids]
            for i in ids:
                alpha = jnp.exp2(m_prev[i] - m_new[i])
                acc_scr[head[i], :, cols[i]] = alpha * acc_scr[head[i], :, cols[i]] + pv[i]
                l_scr[head[i], 0:1, cols[i]] = alpha * l_scr[head[i], 0:1, cols[i]] + psum[i]
                m_scr[head[i], 0:1, cols[i]] = m_new[i]
            return carry

        lax.fori_loop(0, MLA_HEADS // ATTN_HEADS_PER_ITER, head_body, 0)

    @pl.when(ki < qi)
    def _():
        step(False)

    @pl.when(ki == qi)
    def _():
        step(True)
        g = g_ref[...].astype(F32)
        for h in range(MLA_HEADS):
            o = (acc_scr[h] / l_scr[h, 0:1, :]).T
            o_ref[:, MLA_V * h:MLA_V * (h + 1)] = (o * _silu(g[:, MLA_V * h:MLA_V * (h + 1)])).astype(o_ref.dtype)


def _attn_call(q, k, vt, proj, batch, seq):
    tq = min(seq, 1024)
    nq = seq // tq
    pairs = [(a, b) for a in range(nq) for b in range(a + 1)]
    qi_tab = jnp.asarray([a for a, _ in pairs], jnp.int32)
    ki_tab = jnp.asarray([b for _, b in pairs], jnp.int32)
    grid_spec = pltpu.PrefetchScalarGridSpec(
        num_scalar_prefetch=2,
        grid=(batch, len(pairs)),
        in_specs=[pl.BlockSpec((1, MLA_HEADS, tq, MLA_QK_PAD), lambda b, p, qt, kt: (b, 0, qt[p], 0)),
                  pl.BlockSpec((1, MLA_HEADS, tq, MLA_QK_PAD), lambda b, p, qt, kt: (b, 0, kt[p], 0)),
                  pl.BlockSpec((1, MLA_HEADS, MLA_VT_ROWS, tq), lambda b, p, qt, kt: (b, 0, 0, kt[p])),
                  pl.BlockSpec((tq, MLA_WIDTH), lambda b, p, qt, kt: (b * nq + qt[p], COL_GMLA // MLA_WIDTH))],
        out_specs=pl.BlockSpec((tq, MLA_WIDTH), lambda b, p, qt, kt: (b * nq + qt[p], 0)),
        scratch_shapes=[pltpu.VMEM((MLA_HEADS, 8, tq), F32), pltpu.VMEM((MLA_HEADS, 8, tq), F32),
                        pltpu.VMEM((MLA_HEADS, MLA_VT_ROWS, tq), F32)],
    )
    return pl.pallas_call(
        _attn_kernel,
        out_shape=jax.ShapeDtypeStruct((batch * seq, MLA_WIDTH), BF16),
        grid_spec=grid_spec,
        compiler_params=_params(2), name="mla_attn",
    )(qi_tab, ki_tab, q, k, vt, proj)


def _tri_incl_bf16():
    return (_iota((CHUNK, CHUNK), 0) >= _iota((CHUNK, CHUNK), 1)).astype(BF16)


SSD_HALO = 8
N_SSD_IN = 10
N_RWKV_IN = 18


def _ssd_init(xpad_scr, state_scr):
    xpad_scr[0:SSD_HALO, :] = jnp.zeros((SSD_HALO, SSD_XBC), F32)
    state_scr[...] = jnp.zeros(state_scr.shape, F32)


def _rwkv_init(prev_scr, state_scr):
    prev_scr[...] = jnp.zeros(prev_scr.shape, F32)
    state_scr[...] = jnp.zeros(state_scr.shape, F32)


def _ssd_kernel(*refs):
    @pl.when(pl.program_id(1) == 0)
    def _():
        _ssd_init(refs[N_SSD_IN + 1], refs[N_SSD_IN + 2])

    _ssd_body(*refs)


def _rwkv_kernel(*refs):
    @pl.when(pl.program_id(1) == 0)
    def _():
        _rwkv_init(refs[N_RWKV_IN + 1], refs[N_RWKV_IN + 2])

    _rwkv_body(*refs)


def _ssd_body(xbc_ref, z_ref, misc_ref, convw_ref, convb_ref, dtb_ref, alog_ref, dskip_ref, nw_ref,
              expand_ref, o_ref, xpad_scr, state_scr, xc_scr, xdt_scr, dta_scr, y_scr, inc_scr, dec_scr,
              pre_scr):
    ts = xbc_ref.shape[0]
    halo = SSD_HALO
    xpad_scr[halo:halo + ts, :] = xbc_ref[...].astype(F32)
    conv = convb_ref[...]
    for i in range(SSD_CONV):
        start = halo - (SSD_CONV - 1) + i
        conv = conv + convw_ref[i:i + 1, :] * xpad_scr[start:start + ts, :]
    xpad_scr[0:halo, :] = xpad_scr[ts:ts + halo, :]
    xc_scr[...] = _silu(conv)

    dt = jax.nn.softplus(misc_ref[:, LANES:2 * LANES].astype(F32) + dtb_ref[...])
    dt_x = _dot_exact_rhs(dt, expand_ref[...])
    xdt_scr[...] = xc_scr[:, 0:SSD_WIDTH] * dt_x
    dta_scr[...] = dt_x * (-jnp.exp(alog_ref[...]))

    tri = _tri_incl_bf16()
    lane = _iota((CHUNK, PAIR), 1)
    row = _iota((CHUNK, PAIR), 0)
    in_first = lane < SSD_HEAD_DIM
    src = jnp.where(in_first, lane, lane - SSD_HEAD_DIM)
    diag2 = row == src
    causal2 = row >= src
    gw = SSD_WIDTH // SSD_GROUPS

    n_chunks = ts // CHUNK
    group = SSD_GROUP_CHUNKS if n_chunks % SSD_GROUP_CHUNKS == 0 else 1
    b_cols = [slice(SSD_WIDTH + SSD_STATE * g, SSD_WIDTH + SSD_STATE * (g + 1)) for g in range(SSD_GROUPS)]
    c_cols = [slice(SSD_WIDTH + SSD_STATE * (SSD_GROUPS + g), SSD_WIDTH + SSD_STATE * (SSD_GROUPS + g + 1))
              for g in range(SSD_GROUPS)]
    g_lanes = [slice(gw * g, gw * (g + 1)) for g in range(SSD_GROUPS)]

    def group_body(gi, carry):
        cs = range(group)
        rows = [pl.ds(pl.multiple_of((gi * group + ci) * CHUNK, CHUNK), CHUNK) for ci in cs]
        cum = [_dot_exact_lhs(tri, dta_scr[rows[ci], :]) for ci in cs]
        xdt = [xdt_scr[rows[ci], :] for ci in cs]
        bm = [[xc_scr[rows[ci], b_cols[g]] for g in range(SSD_GROUPS)] for ci in cs]
        cm = [[xc_scr[rows[ci], c_cols[g]] for g in range(SSD_GROUPS)] for ci in cs]
        tot = [cum[ci][CHUNK - 1:CHUNK, :] for ci in cs]
        xw = [xdt[ci] * jnp.exp(tot[ci] - cum[ci]) for ci in cs]
        cb2 = [[_dot_nt(cm[ci][g], jnp.concatenate([bm[ci][g], bm[ci][g]], axis=0))
                for g in range(SSD_GROUPS)] for ci in cs]
        inc = [[_dot_tn(bm[ci][g], xw[ci][:, g_lanes[g]]) for g in range(SSD_GROUPS)] for ci in cs]
        lhs, rhs = [], []
        for ci in cs:
            for p in range(N_PAIRS):
                pl_ = slice(PAIR * p, PAIR * (p + 1))
                colp = cum[ci][:, pl_]
                rowp = jnp.sum(jnp.where(diag2, colp, 0.0), axis=0, keepdims=True)
                dec = jnp.exp(jnp.where(causal2, colp - rowp, -jnp.inf))
                lhs.append(cb2[ci][p // 2] * dec)
                rhs.append(_stack_heads(xdt[ci][:, pl_], in_first))
        y_diag = [_dot(lhs[i], rhs[i]) for i in range(len(lhs))]
        for ci in cs:
            c = gi * group + ci
            for p in range(N_PAIRS):
                y_scr[rows[ci], PAIR * p:PAIR * (p + 1)] = y_diag[ci * N_PAIRS + p]
            for g in range(SSD_GROUPS):
                inc_scr[c, :, g_lanes[g]] = inc[ci][g]
            dec_scr[c] = jnp.broadcast_to(jnp.exp(tot[ci]), (8, SSD_WIDTH))
            dta_scr[rows[ci], :] = jnp.exp(cum[ci])
        return carry

    lax.fori_loop(0, n_chunks // group, group_body, 0)

    state = state_scr[...]
    for c in range(n_chunks):
        pre_scr[c] = state.astype(BF16)
        state = state * dec_scr[c, 0:1, :] + inc_scr[c]
    state_scr[...] = state

    def off_body(gi, carry):
        cs = range(group)
        rows = [pl.ds(pl.multiple_of((gi * group + ci) * CHUNK, CHUNK), CHUNK) for ci in cs]
        y_off = [[jnp.dot(xc_scr[rows[ci], c_cols[g]].astype(BF16), pre_scr[gi * group + ci, :, g_lanes[g]],
                          preferred_element_type=F32) for g in range(SSD_GROUPS)] for ci in cs]
        for ci in cs:
            for g in range(SSD_GROUPS):
                y_scr[rows[ci], g_lanes[g]] = (y_scr[rows[ci], g_lanes[g]]
                                               + y_off[ci][g] * dta_scr[rows[ci], g_lanes[g]])
        return carry

    lax.fori_loop(0, n_chunks // group, off_body, 0)

    y = y_scr[...] + dskip_ref[...] * xc_scr[:, 0:SSD_WIDTH]
    o_ref[...] = _rms(y * _silu(z_ref[...].astype(F32)), nw_ref[...]).astype(o_ref.dtype)


def _head_expand(vec, width):
    return jnp.repeat(vec.astype(F32), width).reshape(1, -1)


def _ssd_call(proj, conv_w, conv_b, dt_bias, a_log, d_skip, ssd_norm_w, batch, seq):
    ts = min(seq, 512)
    nt = seq // ts
    dtb = jnp.zeros((1, LANES), F32).at[0, :SSD_HEADS].set(dt_bias)
    expand = (jnp.arange(LANES)[:, None] == (jnp.arange(SSD_WIDTH) // SSD_HEAD_DIM)[None, :]).astype(BF16)
    row = lambda w: pl.BlockSpec((1, w), lambda b, i: (0, 0))
    return pl.pallas_call(
        _ssd_kernel,
        out_shape=jax.ShapeDtypeStruct((batch * seq, SSD_WIDTH), BF16),
        grid=(batch, nt),
        in_specs=[pl.BlockSpec((ts, SSD_XBC), lambda b, i: (b * nt + i, COL_XBC // SSD_XBC)),
                  pl.BlockSpec((ts, PROJ_BLOCK), lambda b, i: (b * nt + i, COL_Z // PROJ_BLOCK)),
                  pl.BlockSpec((ts, MISC_BLOCK), lambda b, i: (b * nt + i, COL_MISC // MISC_BLOCK)),
                  pl.BlockSpec((SSD_CONV, SSD_XBC), lambda b, i: (0, 0)),
                  row(SSD_XBC), row(LANES), row(SSD_WIDTH), row(SSD_WIDTH), row(SSD_WIDTH),
                  pl.BlockSpec((LANES, SSD_WIDTH), lambda b, i: (0, 0))],
        out_specs=pl.BlockSpec((ts, SSD_WIDTH), lambda b, i: (b * nt + i, 0)),
        scratch_shapes=[pltpu.VMEM((ts + SSD_HALO, SSD_XBC), F32),
                        pltpu.VMEM((SSD_STATE, SSD_WIDTH), F32),
                        pltpu.VMEM((ts, SSD_XBC), F32),
                        pltpu.VMEM((ts, SSD_WIDTH), F32),
                        pltpu.VMEM((ts, SSD_WIDTH), F32),
                        pltpu.VMEM((ts, SSD_WIDTH), F32),
                        pltpu.VMEM((ts // CHUNK, SSD_STATE, SSD_WIDTH), F32),
                        pltpu.VMEM((ts // CHUNK, 8, SSD_WIDTH), F32),
                        pltpu.VMEM((ts // CHUNK, SSD_STATE, SSD_WIDTH), BF16)],
        compiler_params=_params(2), name="ssd",
    )(proj, proj, proj, conv_w, conv_b.reshape(1, -1), dtb, _head_expand(a_log, SSD_HEAD_DIM),
      _head_expand(d_skip, SSD_HEAD_DIM), ssd_norm_w.reshape(1, -1), expand)


def _pair_sum(x, in_first):
    s0 = jnp.sum(jnp.where(in_first, x, 0.0), axis=-1, keepdims=True)
    s1 = jnp.sum(jnp.where(in_first, 0.0, x), axis=-1, keepdims=True)
    return jnp.where(in_first, s0, s1)


def _stack_heads(x, in_first):
    return jnp.concatenate([jnp.where(in_first, x, 0.0), jnp.where(in_first, 0.0, x)], axis=0)


def _rwkv_body(r_ref, k_ref, v_ref, misc_ref, g_ref, mur_ref, muk_ref, muv_ref, muwa_ref, w0_ref,
               wlb_ref, a0_ref, alb_ref, kk_ref, ka_ref, rk_ref, lnw_ref, lnb_ref, o_ref,
               prev_scr, state_scr, r_scr, lw_scr, k_scr, v_scr, a_scr, b_scr, y_scr, rg_scr, yh_scr):
    ts = r_ref.shape[0]
    first_row = _iota((8, 1), 0) == 0

    def shift(x, mu, slot):
        w = x.shape[1]
        rolled = pltpu.roll(x, 1, 0)
        head = jnp.where(first_row, prev_scr[slot:slot + 1, 0:w], rolled[0:8, :])
        prev = jnp.concatenate([head, rolled[8:, :]], axis=0)
        prev_scr[slot:slot + 1, 0:w] = x[ts - 1:ts, :]
        return x + (prev - x) * mu

    r = shift(r_ref[...].astype(F32), mur_ref[...], 0)
    k = shift(k_ref[...].astype(F32), muk_ref[...], 1)
    v = shift(v_ref[...].astype(F32), muv_ref[...], 2)
    wa = shift(misc_ref[:, 0:LANES].astype(F32), muwa_ref[...], 3)

    lw_scr[...] = -DECAY_SCALE * _sigmoid(w0_ref[...] + _dot(jnp.tanh(wa), wlb_ref[...]))
    a = _sigmoid(a0_ref[...] + _dot(wa, alb_ref[...]))

    in_first_t = _iota((ts, PAIR), 1) < RWKV_HEAD_DIM
    kk = k * kk_ref[...]
    k2 = k * (1.0 + (a - 1.0) * ka_ref[...])
    rkr = r * k2 * rk_ref[...]
    for p in range(N_PAIRS):
        sl = slice(PAIR * p, PAIR * (p + 1))
        kkp = kk[:, sl]
        kkn = kkp * lax.rsqrt(jnp.maximum(_pair_sum(kkp * kkp, in_first_t), 1e-24))
        a_scr[:, sl] = -kkn
        b_scr[:, sl] = kkn * a[:, sl]
        y_scr[:, sl] = _pair_sum(rkr[:, sl], in_first_t)
    bonus = y_scr[...] * v
    r_scr[...] = r
    k_scr[...] = k2
    v_scr[...] = v

    tri = _tri_incl_bf16()
    in_first = _iota((CHUNK, PAIR), 1) < RWKV_HEAD_DIM
    rr = _iota((PAIR, PAIR), 0)
    cc = _iota((PAIR, PAIR), 1)
    same = lax.shift_right_logical(rr, CHUNK_SHIFT) == lax.shift_right_logical(cc, CHUNK_SHIFT)
    strict = same & (rr > cc)
    incl = same & (rr >= cc)
    eye_b = rr == cc
    eye = eye_b.astype(F32)
    zeros_pp = jnp.zeros((PAIR, PAIR), BF16)
    n_chunks = ts // CHUNK
    group = RWKV_GROUP_CHUNKS if n_chunks % RWKV_GROUP_CHUNKS == 0 else 1

    def group_body(gi, carry):
        atm, rtm, vm, bkh, bke, wend = [], [], [], [], [], []
        for ci in range(group):
            rows = pl.ds((gi * group + ci) * CHUNK, CHUNK)
            lw = lw_scr[rows, :]
            cum = _dot_exact_lhs(tri, lw)
            tot = cum[CHUNK - 1:CHUNK, :]
            e_neg = jnp.exp(-cum)
            e_end = jnp.exp(tot - cum)
            w_end = jnp.exp(tot)
            rc, kc, vc, ac, bc = (r_scr[rows, :], k_scr[rows, :], v_scr[rows, :], a_scr[rows, :],
                                  b_scr[rows, :])
            a_t = ac * jnp.exp(cum - lw)
            r_t = rc * jnp.exp(cum)
            b_h = bc * e_neg
            k_h = kc * e_neg
            b_e = bc * e_end
            k_e = kc * e_end
            for p in range(N_PAIRS):
                sl = slice(PAIR * p, PAIR * (p + 1))
                st = lambda x: _stack_heads(x[:, sl], in_first)
                atm.append(st(a_t))
                rtm.append(st(r_t))
                vm.append(st(vc).astype(BF16))
                bkh.append(jnp.concatenate([st(b_h), st(k_h)], axis=0).astype(BF16))
                bke.append(jnp.concatenate([st(b_e), st(k_e)], axis=0).astype(BF16))
                wend.append(w_end[:, sl])
        ids = range(len(atm))
        amat = [_dot_nt(jnp.concatenate([atm[i], rtm[i]], axis=0), bkh[i]) for i in ids]
        n_ab = [jnp.where(strict, amat[i][0:PAIR, 0:PAIR], 0.0) for i in ids]
        a_ak = [jnp.where(strict, amat[i][0:PAIR, PAIR:], 0.0) for i in ids]
        a_r = [jnp.concatenate([jnp.where(incl, amat[i][PAIR:, 0:PAIR], 0.0),
                                jnp.where(incl, amat[i][PAIR:, PAIR:], 0.0)], axis=1).astype(BF16)
               for i in ids]
        akv = [_dot(a_ak[i], vm[i]) for i in ids]
        npow = [_dot(n_ab[i], n_ab[i]) for i in ids]
        tinv = [eye + n_ab[i] for i in ids]
        for _ in range(CHUNK_SHIFT - 2):
            both = [_dot(npow[i], jnp.concatenate([npow[i], tinv[i]], axis=1)) for i in ids]
            npow = [both[i][:, 0:PAIR] for i in ids]
            tinv = [tinv[i] + both[i][:, PAIR:] for i in ids]
        last = [_dot(npow[i], tinv[i]) for i in ids]
        tinv = [tinv[i] + last[i] for i in ids]
        au = [_dot(tinv[i], jnp.concatenate([atm[i], akv[i]], axis=1)) for i in ids]
        rhs = [jnp.concatenate([au[i].astype(BF16), jnp.concatenate([zeros_pp, vm[i]], axis=1)], axis=0)
               for i in ids]
        ry = [jnp.dot(a_r[i], rhs[i], preferred_element_type=F32) for i in ids]
        gh = [_dot_tn(bke[i], rhs[i]) for i in ids]
        for i in ids:
            c = gi * group + i // N_PAIRS
            p = i % N_PAIRS
            gmat = gh[i][:, 0:PAIR] + jnp.where(eye_b, wend[i], 0.0)
            rg_scr[c, p] = jnp.concatenate([rtm[i] + ry[i][:, 0:PAIR], gmat], axis=0).astype(BF16)
            yh_scr[c, p] = jnp.concatenate([ry[i][:, PAIR:], gh[i][:, PAIR:]], axis=0)
        return carry

    for gi in range(n_chunks // group):
        group_body(gi, 0)

    def sweep_body(c, carry):
        r0 = c * CHUNK
        s2 = []
        for p in range(N_PAIRS):
            s_hi, s_lo = _split(state_scr[p])
            s2.append(jnp.concatenate([s_hi, s_lo], axis=1))
        out = [jnp.dot(rg_scr[c, p], s2[p], preferred_element_type=F32) for p in range(N_PAIRS)]
        for p in range(N_PAIRS):
            tot = yh_scr[c, p] + out[p][:, 0:PAIR] + out[p][:, PAIR:]
            state_scr[p] = tot[PAIR:, :]
            y_scr[pl.ds(r0, CHUNK), PAIR * p:PAIR * (p + 1)] = tot[0:CHUNK, :] + tot[CHUNK:PAIR, :]
        return carry

    for c in range(n_chunks):
        sweep_body(c, 0)

    g = g_ref[...].astype(F32)
    for p in range(N_PAIRS):
        sl = slice(PAIR * p, PAIR * (p + 1))
        y = y_scr[:, sl]
        mean = _pair_sum(y, in_first_t) * (1.0 / RWKV_HEAD_DIM)
        d = y - mean
        var = _pair_sum(d * d, in_first_t) * (1.0 / RWKV_HEAD_DIM)
        yn = d * lax.rsqrt(var + GN_EPS) * lnw_ref[:, sl] + lnb_ref[:, sl]
        o_ref[:, sl] = ((yn + bonus[:, sl]) * _silu(g[:, sl])).astype(o_ref.dtype)


def _rwkv_call(proj, mu_rkv, mu_w, mu_a, w0, w_lora_b, a0, a_lora_b, k_k, k_a, r_k, lnx_w, lnx_b,
               batch, seq):
    ts = min(seq, 512)
    nt = seq // ts
    w = RWKV_WIDTH
    muwa = jnp.concatenate([mu_w, mu_a]).reshape(1, LANES)
    zl = jnp.zeros((DECAY_LORA, w), F32)
    wlb = jnp.concatenate([w_lora_b, zl], axis=0).astype(BF16)
    alb = jnp.concatenate([zl, a_lora_b], axis=0).astype(BF16)
    blk = lambda col: pl.BlockSpec((ts, PROJ_BLOCK), lambda b, i: (b * nt + i, col // PROJ_BLOCK))
    row = lambda width: pl.BlockSpec((1, width), lambda b, i: (0, 0))
    lora = pl.BlockSpec((LANES, w), lambda b, i: (0, 0))
    tile = pltpu.VMEM((ts, w), F32)
    return pl.pallas_call(
        _rwkv_kernel,
        out_shape=jax.ShapeDtypeStruct((batch * seq, w), BF16),
        grid=(batch, nt),
        in_specs=[blk(COL_R), blk(COL_K), blk(COL_V),
                  pl.BlockSpec((ts, MISC_BLOCK), lambda b, i: (b * nt + i, COL_MISC // MISC_BLOCK)),
                  blk(COL_GRWKV),
                  row(w), row(w), row(w), row(LANES), row(w), lora, row(w), lora,
                  row(w), row(w), row(w), row(w), row(w)],
        out_specs=pl.BlockSpec((ts, w), lambda b, i: (b * nt + i, 0)),
        scratch_shapes=[pltpu.VMEM((8, w), F32),
                        pltpu.VMEM((N_PAIRS, PAIR, PAIR), F32),
                        tile, tile, tile, tile, tile, tile, tile,
                        pltpu.VMEM((ts // CHUNK, N_PAIRS, 2 * PAIR, PAIR), BF16),
                        pltpu.VMEM((ts // CHUNK, N_PAIRS, 2 * PAIR, PAIR), F32)],
        compiler_params=_params(2), name="rwkv7",
    )(proj, proj, proj, proj, proj,
      mu_rkv[0:1], mu_rkv[1:2], mu_rkv[2:3], muwa, w0.reshape(1, w), wlb, a0.reshape(1, w), alb,
      k_k.reshape(1, w), k_a.reshape(1, w), r_k.reshape(1, w), lnx_w.reshape(1, w), lnx_b.reshape(1, w))


def _outproj_kernel(ym_ref, ys_ref, yr_ref, x_ref, gate_ref, w_ref, fnw_ref, o_ref, *, final):
    acc = (_dot(ym_ref[...], w_ref[0:MLA_WIDTH, :])
           + _dot(ys_ref[...], w_ref[MLA_WIDTH:MLA_WIDTH + SSD_WIDTH, :])
           + _dot(yr_ref[...], w_ref[MLA_WIDTH + SSD_WIDTH:, :]))
    xn = x_ref[...] + gate_ref[0] * acc
    o_ref[...] = _rms(xn, fnw_ref[...]) if final else xn


def _outproj_call(y_mla, y_ssd, y_rwkv, x2, gate, w_out, final_norm_w, seq, final):
    t, d = x2.shape
    tm = min(seq, 1024)
    per_batch = seq // tm
    yspec = lambda wd: pl.BlockSpec((tm, wd), lambda i: (i, 0))
    return pl.pallas_call(
        functools.partial(_outproj_kernel, final=final),
        out_shape=jax.ShapeDtypeStruct((t, d), F32),
        grid=(t // tm,),
        in_specs=[yspec(MLA_WIDTH), yspec(SSD_WIDTH), yspec(RWKV_WIDTH), yspec(d),
                  pl.BlockSpec((1, 1, d), lambda i: (i // per_batch, 0, 0)),
                  pl.BlockSpec((D_MIX, d), lambda i: (0, 0)),
                  pl.BlockSpec((1, d), lambda i: (0, 0))],
        out_specs=yspec(d),
        compiler_params=_params(1), name="out_proj",
    )(y_mla, y_ssd, y_rwkv, x2, gate, w_out.astype(BF16), final_norm_w.reshape(1, d))


def kernel(x, c, positions, ada_w, ada_b, norm_w, w_in, q_norm_w, w_uq, kv_norm_w, w_ukv, conv_w, conv_b, dt_bias, a_log, d_skip, ssd_norm_w, mu_rkv, mu_w, mu_a, w0, w_lora_b, a0, a_lora_b, k_k, k_a, r_k, lnx_w, lnx_b, w_out, final_norm_w):
    batch, seq, d = x.shape
    depth = ada_w.shape[0]
    assert d == D_MODEL and batch <= 8 and seq % CHUNK == 0
    t = batch * seq
    x2 = x.reshape(t, d)
    mod = _mod_call(jnp.pad(c, ((0, 8 - batch), (0, 0))), ada_w, ada_b)
    rope_tab = _rope_call(positions.reshape(t, 1))
    for l in range(depth):
        shift = mod[l, :batch, 0:d].reshape(batch, 1, d)
        scale = mod[l, :batch, d:2 * d].reshape(batch, 1, d)
        gate = mod[l, :batch, 2 * d:].reshape(batch, 1, d)
        proj = _inproj_call(x2, scale, shift, norm_w[l], _pack_w_in(w_in[l]), seq)
        q, k, v = _mla_prep_call(proj, rope_tab, q_norm_w[l], w_uq[l], kv_norm_w[l], w_ukv[l], batch, seq)
        y_mla = _attn_call(q, k, v, proj, batch, seq)
        y_ssd = _ssd_call(proj, conv_w[l], conv_b[l], dt_bias[l], a_log[l], d_skip[l], ssd_norm_w[l],
                          batch, seq)
        y_rwkv = _rwkv_call(proj, mu_rkv[l], mu_w[l], mu_a[l], w0[l], w_lora_b[l], a0[l], a_lora_b[l],
                            k_k[l], k_a[l], r_k[l], lnx_w[l], lnx_b[l], batch, seq)
        x2 = _outproj_call(y_mla, y_ssd, y_rwkv, x2, gate, w_out[l], final_norm_w, seq,
                           final=(l == depth - 1))
    return x2.reshape(batch, seq, d)
```
